```python
import math
import functools
import jax
import jax.numpy as jnp
from jax import lax
import numpy as np

D_MODEL = 1024
BATCH = 2
SEQ = 8192
DEPTH = 2
DEC_BATCH = 128
DEC_SEQ = 1
PAST_LEN = 2048
PAGE_SIZE = 128

DA_HEADS = 4
DA_WIDTH = D_MODEL // 2
DA_DV = DA_WIDTH // DA_HEADS
DA_DK = DA_DV // 2
ROT_DIM = DA_DK // 4
ROPE_THETA = 500000.0
Q_BLOCK = 128
SSM_WIDTH = D_MODEL // 4
SSM_HEADS = 4
SSM_HEADDIM = SSM_WIDTH // SSM_HEADS
SSM_GROUPS = 2
SSM_STATE = 128
SSM_CONV = 4
SSM_CHUNK = 128
CONV_CH = SSM_WIDTH + 2 * SSM_GROUPS * SSM_STATE
POOL_WIDTH = D_MODEL // 4
POOL_WINDOWS = (2, 4, 8, 16)
POOL_GROUPS = len(POOL_WINDOWS)
POOL_GDIM = POOL_WIDTH // POOL_GROUPS
POOL_BUF = max(POOL_WINDOWS) - 1
MIX_WIDTH = DA_WIDTH + SSM_WIDTH + POOL_WIDTH
IN_SPLITS = (DA_HEADS * 2 * DA_DK, DA_HEADS * 2 * DA_DK, DA_WIDTH, SSM_WIDTH, CONV_CH, SSM_HEADS, POOL_WIDTH)
IN_COLS = sum(IN_SPLITS)
MEM_LEN = 256
MEM_HEADS = 4
MEM_HD = D_MODEL // MEM_HEADS
D_FF = 128 * ((8 * D_MODEL // 3 + 127) // 128)
N_EXPERTS = 8
TOP_K = 2
D_FF_E = 7 * D_MODEL // 2
N_DENSE = (DEPTH + 1) // 2
N_MOE = DEPTH // 2
ALPHA = (2.0 * DEPTH) ** 0.25
BETA = (8.0 * DEPTH) ** -0.25
LN_EPS = 1e-5
RMS_EPS = 1e-6

kernel_name = 'hymba_diffattn_ssd_pool_decode_step'


def layer_norm(x, g, b):
    xf = x.astype(jnp.float32)
    mu = jnp.mean(xf, -1, keepdims=True)
    var = jnp.mean(jnp.square(xf - mu), -1, keepdims=True)
    return ((xf - mu) * lax.rsqrt(var + LN_EPS) * g.astype(jnp.float32) + b.astype(jnp.float32)).astype(x.dtype)


def rms_norm(x, w):
    xf = x.astype(jnp.float32)
    return xf * lax.rsqrt(jnp.mean(jnp.square(xf), -1, keepdims=True) + RMS_EPS) * w.astype(jnp.float32)


def rope_partial(t, pos):
    half = ROT_DIM // 2
    inv = ROPE_THETA ** (-jnp.arange(half, dtype=jnp.float32) * 2.0 / ROT_DIM)
    ang = pos.astype(jnp.float32)[:, None] * inv[None, :]
    cos = jnp.cos(ang)[None, :, None, None, :]
    sin = jnp.sin(ang)[None, :, None, None, :]
    tr = t[..., :ROT_DIM].astype(jnp.float32)
    t1, t2 = tr[..., :half], tr[..., half:]
    rot = jnp.concatenate([t1 * cos - t2 * sin, t1 * sin + t2 * cos], -1).astype(t.dtype)
    return jnp.concatenate([rot, t[..., ROT_DIM:]], -1)


def diff_attn_core(q, k, v, mask, lam):
    s = jnp.einsum('bqhcd,bkhcd->bhcqk', q, k).astype(jnp.float32) * DA_DK ** -0.5
    p = jax.nn.softmax(jnp.where(mask, s, -jnp.inf), axis=-1)
    w = p[:, :, 0] - lam * p[:, :, 1]
    return jnp.einsum('bhqk,bkhd->bqhd', w.astype(v.dtype), v)


def diff_attn_prompt(q, k, v, lam):
    b, S = q.shape[:2]
    nb = S // Q_BLOCK
    qb = jnp.moveaxis(q.reshape(b, nb, Q_BLOCK, DA_HEADS, 2, DA_DK), 1, 0)
    kpos = jnp.arange(S)

    def block(args):
        i, qi = args
        qpos = i * Q_BLOCK + jnp.arange(Q_BLOCK)
        return diff_attn_core(qi, k, v, kpos[None, :] <= qpos[:, None], lam)

    o = lax.map(block, (jnp.arange(nb), qb))
    return jnp.moveaxis(o, 0, 1).reshape(b, S, DA_HEADS, DA_DV)


def diff_attn_cached(q, k, v, lam, k_past, v_past):
    past = k_past.shape[1]
    L = q.shape[1]
    kk = jnp.concatenate([k_past.astype(k.dtype), k], axis=1)
    vv = jnp.concatenate([v_past.astype(v.dtype), v], axis=1)
    qpos = past + jnp.arange(L)
    kpos = jnp.arange(past + L)
    return diff_attn_core(q, kk, vv, kpos[None, :] <= qpos[:, None], lam)


def causal_dwconv(u, buf, w, b):
    full = jnp.concatenate([buf.astype(u.dtype), u], axis=1)
    out = lax.conv_general_dilated(full, w[:, None, :].astype(u.dtype), window_strides=(1,), padding='VALID',
                                   dimension_numbers=('NWC', 'WIO', 'NWC'), feature_group_count=u.shape[-1])
    return jax.nn.silu(out + b.astype(u.dtype)), full[:, -(SSM_CONV - 1):]


def ssd_scan(x, dt, a, bm, cm, h0):
    b, L, H, P = x.shape
    Q = SSM_CHUNK if L % SSM_CHUNK == 0 else L
    nc = L // Q
    rep = H // SSM_GROUPS
    bh = jnp.repeat(bm, rep, axis=2).reshape(b, nc, Q, H, SSM_STATE)
    ch = jnp.repeat(cm, rep, axis=2).reshape(b, nc, Q, H, SSM_STATE)
    xdt = (x * dt[..., None]).reshape(b, nc, Q, H, P)
    a_cs = jnp.cumsum((dt * a).reshape(b, nc, Q, H), axis=2)
    causal = jnp.tril(jnp.ones((Q, Q), bool))[None, None, :, :, None]
    seg = a_cs[:, :, :, None, :] - a_cs[:, :, None, :, :]
    decay = jnp.exp(jnp.where(causal, seg, -jnp.inf))
    cb = jnp.einsum('bcthn,bcshn->bctsh', ch, bh)
    y_diag = jnp.einsum('bctsh,bcshp->bcthp', cb * decay, xdt)
    decay_end = jnp.exp(a_cs[:, :, -1:, :] - a_cs)
    chunk_states = jnp.einsum('bcsh,bcshn,bcshp->bchpn', decay_end, bh, xdt)
    chunk_decay = jnp.exp(a_cs[:, :, -1, :])

    def step(h, inp):
        st, dec = inp
        return h * dec[:, :, None, None] + st, h

    h_last, h_prev = lax.scan(step, h0, (jnp.moveaxis(chunk_states, 1, 0), jnp.moveaxis(chunk_decay, 1, 0)))
    h_prev = jnp.moveaxis(h_prev, 0, 1)
    y_off = jnp.einsum('bcthn,bchpn,bcth->bcthp', ch, h_prev, jnp.exp(a_cs))
    return (y_diag + y_off).reshape(b, L, H, P), h_last


def pool_mix(u, buf, pos, pool_w, pool_scale):
    b, L, _ = u.shape
    full = jnp.concatenate([buf.astype(jnp.float32), u.astype(jnp.float32)], axis=1)
    cs = jnp.pad(jnp.cumsum(full, axis=1), ((0, 0), (1, 0), (0, 0)))
    end = cs[:, POOL_BUF + 1:]
    means = []
    for g, w in enumerate(POOL_WINDOWS):
        chs = slice(g * POOL_GDIM, (g + 1) * POOL_GDIM)
        start = cs[:, POOL_BUF + 1 - w: POOL_BUF + 1 - w + L, chs]
        cnt = jnp.minimum(w, pos + 1).astype(jnp.float32)[None, :, None]
        means.append((end[..., chs] - start) / cnt)
    d = (jnp.concatenate(means, -1) - full[:, POOL_BUF:]).reshape(b, L, POOL_GROUPS, POOL_GDIM)
    out = jnp.einsum('blgc,gcd->blgd', d, pool_w.astype(jnp.float32)).reshape(b, L, POOL_WIDTH)
    out = out * pool_scale.astype(jnp.float32)
    return out.astype(u.dtype), full[:, -POOL_BUF:].astype(u.dtype)


def mixer(x, pos0, attn_fn, conv_buf, ssm_h, pool_buf, lam_init,
          w_in, lam_q1, lam_k1, lam_q2, lam_k2, da_norm_w, conv_w, conv_b,
          dt_bias, a_log, d_skip, ssm_norm_w, pool_w, pool_scale, w_out):
    f32 = jnp.float32
    b, L, _ = x.shape
    pos = pos0 + jnp.arange(L)
    idx = np.cumsum(IN_SPLITS)[:-1].tolist()
    q, k, v, z, xbc, dt_raw, pu = jnp.split(x @ w_in, idx, axis=-1)
    q = rope_partial(q.reshape(b, L, DA_HEADS, 2, DA_DK), pos)
    k = rope_partial(k.reshape(b, L, DA_HEADS, 2, DA_DK), pos)
    v = v.reshape(b, L, DA_HEADS, DA_DV)
    lam = (jnp.exp(jnp.sum(lam_q1.astype(f32) * lam_k1.astype(f32)))
           - jnp.exp(jnp.sum(lam_q2.astype(f32) * lam_k2.astype(f32))) + lam_init)
    o_a = rms_norm(attn_fn(q, k, v, lam), da_norm_w) * (1.0 - lam_init)
    xbc, new_conv = causal_dwconv(xbc, conv_buf, conv_w, conv_b)
    xs, bm, cm = jnp.split(xbc.astype(f32), [SSM_WIDTH, SSM_WIDTH + SSM_GROUPS * SSM_STATE], axis=-1)
    xs = xs.reshape(b, L, SSM_HEADS, SSM_HEADDIM)
    dt = jax.nn.softplus(dt_raw.astype(f32) + dt_bias.astype(f32))
    a = -jnp.exp(a_log.astype(f32))
    y, new_h = ssd_scan(xs, dt, a, bm.reshape(b, L, SSM_GROUPS, SSM_STATE),
                        cm.reshape(b, L, SSM_GROUPS, SSM_STATE), ssm_h.astype(f32))
    y = (y + d_skip.astype(f32)[:, None] * xs).reshape(b, L, SSM_WIDTH)
    y = rms_norm(y * jax.nn.silu(z.astype(f32)), ssm_norm_w)
    y_pool, new_pool = pool_mix(pu, pool_buf, pos, pool_w, pool_scale)
    heads = jnp.concatenate([o_a.reshape(b, L, DA_WIDTH).astype(x.dtype), y.astype(x.dtype), y_pool], axis=-1)
    return heads @ w_out, (k, v, new_conv, new_h, new_pool)


def mem_attn(x, mk, mv, wq, wo):
    b, L, _ = x.shape
    q = (x @ wq).reshape(b, L, MEM_HEADS, MEM_HD)
    s = jnp.einsum('blhd,bmhd->bhlm', q, mk.astype(x.dtype)).astype(jnp.float32) * MEM_HD ** -0.5
    p = jax.nn.softmax(s, axis=-1).astype(x.dtype)
    o = jnp.einsum('bhlm,bmhd->blhd', p, mv.astype(x.dtype)).reshape(b, L, D_MODEL)
    return o @ wo


def swiglu(x, w1, w3, w2):
    return (jax.nn.silu(x @ w1) * (x @ w3)) @ w2


def moe_swiglu(x, w_router, b_router, w1, w3, w2):
    logits = (x @ w_router).astype(jnp.float32) + b_router.astype(jnp.float32)
    top_v, top_i = lax.top_k(logits, TOP_K)
    gates = jax.nn.softmax(top_v, axis=-1)
    combine = jnp.sum(jax.nn.one_hot(top_i, N_EXPERTS, dtype=jnp.float32) * gates[..., None], axis=-2).astype(x.dtype)
    out = jnp.zeros_like(x)
    for e in range(N_EXPERTS):
        out = out + combine[..., e:e + 1] * swiglu(x, w1[e], w3[e], w2[e])
    return out


def setup_inputs(seed: int = 0) -> dict:
    key = jax.random.key(seed)
    keys = iter(jax.random.split(key, 64))

    def nrm(shape, scale):
        return jax.random.normal(next(keys), shape, jnp.float32) * scale

    def gain(shape):
        return 1.0 + nrm(shape, 0.02)

    n_pages = PAST_LEN // PAGE_SIZE
    n_used = DEC_BATCH * n_pages
    n_phys = n_used + (n_used + 3) // 4
    page_table = jax.random.permutation(next(keys), n_phys)[:n_used].reshape(DEC_BATCH, n_pages).astype(jnp.int32)
    dt0 = jnp.exp(jax.random.uniform(next(keys), (DEPTH, SSM_HEADS), jnp.float32, math.log(1e-3), math.log(1e-1)))
    return {
        'x_prompt': nrm((BATCH, SEQ, D_MODEL), 1.0),
        'x_sample': nrm((DEC_BATCH, DEC_SEQ, D_MODEL), 1.0),
        'cache_k': nrm((DEPTH, n_phys, PAGE_SIZE, DA_HEADS, 2, DA_DK), 1.0),
        'cache_v': nrm((DEPTH, n_phys, PAGE_SIZE, DA_HEADS, DA_DV), 1.0),
        'cache_mem_k': nrm((DEPTH, DEC_BATCH, MEM_LEN, MEM_HEADS, MEM_HD), 1.0),
        'cache_mem_v': nrm((DEPTH, DEC_BATCH, MEM_LEN, MEM_HEADS, MEM_HD), 1.0),
        'state_ssm': nrm((DEPTH, DEC_BATCH, SSM_HEADS, SSM_HEADDIM, SSM_STATE), 0.1),
        'state_conv': nrm((DEPTH, DEC_BATCH, SSM_CONV - 1, CONV_CH), 1.0),
        'state_pool': nrm((DEPTH, DEC_BATCH, POOL_BUF, POOL_WIDTH), 1.0),
        'page_table': page_table,
        'mem_prompt': nrm((BATCH, MEM_LEN, D_MODEL), 1.0),
        'w_in': nrm((DEPTH, D_MODEL, IN_COLS), D_MODEL ** -0.5),
        'lam_q1': nrm((DEPTH, DA_DK), 0.1),
        'lam_k1': nrm((DEPTH, DA_DK), 0.1),
        'lam_q2': nrm((DEPTH, DA_DK), 0.1),
        'lam_k2': nrm((DEPTH, DA_DK), 0.1),
        'da_norm_w': gain((DEPTH, DA_DV)),
        'conv_w': nrm((DEPTH, SSM_CONV, CONV_CH), SSM_CONV ** -0.5),
        'conv_b': nrm((DEPTH, CONV_CH), 0.02),
        'dt_bias': dt0 + jnp.log(-jnp.expm1(-dt0)),
        'a_log': jnp.log(jax.random.uniform(next(keys), (DEPTH, SSM_HEADS), jnp.float32, 1.0, 16.0)),
        'd_skip': gain((DEPTH, SSM_HEADS)),
        'ssm_norm_w': gain((DEPTH, SSM_WIDTH)),
        'pool_w': nrm((DEPTH, POOL_GROUPS, POOL_GDIM, POOL_GDIM), POOL_GDIM ** -0.5),
        'pool_scale': gain((DEPTH, POOL_WIDTH)),
        'w_out': nrm((DEPTH, MIX_WIDTH, D_MODEL), BETA * MIX_WIDTH ** -0.5),
        'ln1_g': gain((DEPTH, D_MODEL)),
        'ln1_b': nrm((DEPTH, D_MODEL), 0.02),
        'wq_mem': nrm((DEPTH, D_MODEL, D_MODEL), D_MODEL ** -0.5),
        'wk_mem': nrm((DEPTH, D_MODEL, D_MODEL), D_MODEL ** -0.5),
        'wv_mem': nrm((DEPTH, D_MODEL, D_MODEL), D_MODEL ** -0.5),
        'wo_mem': nrm((DEPTH, D_MODEL, D_MODEL), BETA * D_MODEL ** -0.5),
        'ln2_g': gain((DEPTH, D_MODEL)),
        'ln2_b': nrm((DEPTH, D_MODEL), 0.02),
        'ffn_w1': nrm((N_DENSE, D_MODEL, D_FF), D_MODEL ** -0.5),
        'ffn_w3': nrm((N_DENSE, D_MODEL, D_FF), D_MODEL ** -0.5),
        'ffn_w2': nrm((N_DENSE, D_FF, D_MODEL), BETA * D_FF ** -0.5),
        'moe_router': nrm((N_MOE, D_MODEL, N_EXPERTS), D_MODEL ** -0.5),
        'moe_router_b': nrm((N_MOE, N_EXPERTS), 0.01),
        'moe_w1': nrm((N_MOE, N_EXPERTS, D_MODEL, D_FF_E), D_MODEL ** -0.5),
        'moe_w3': nrm((N_MOE, N_EXPERTS, D_MODEL, D_FF_E), D_MODEL ** -0.5),
        'moe_w2': nrm((N_MOE, N_EXPERTS, D_FF_E, D_MODEL), BETA * D_FF_E ** -0.5),
        'ln3_g': gain((DEPTH, D_MODEL)),
        'ln3_b': nrm((DEPTH, D_MODEL), 0.02),
    }


def reference(x_prompt, x_sample, cache_k, cache_v, cache_mem_k, cache_mem_v, state_ssm, state_conv, state_pool,
              page_table, mem_prompt, w_in, lam_q1, lam_k1, lam_q2, lam_k2, da_norm_w, conv_w, conv_b,
              dt_bias, a_log, d_skip, ssm_norm_w, pool_w, pool_scale, w_out, ln1_g, ln1_b,
              wq_mem, wk_mem, wv_mem, wo_mem, ln2_g, ln2_b, ffn_w1, ffn_w3, ffn_w2,
              moe_router, moe_router_b, moe_w1, moe_w3, moe_w2, ln3_g, ln3_b):
    bp = x_prompt.shape[0]
    bs = x_sample.shape[0]
    past = page_table.shape[1] * PAGE_SIZE
    n_mem = mem_prompt.shape[1]
    xp, xs = x_prompt, x_sample
    kp_l, vp_l, ks_l, vs_l, mk_l, mv_l = [], [], [], [], [], []
    hp_l, hs_l, cp_l, cs_l, pp_l, ps_l = [], [], [], [], [], []
    for l in range(DEPTH):
        lam_init = 0.8 - 0.6 * math.exp(-0.3 * l)
        mw = (w_in[l], lam_q1[l], lam_k1[l], lam_q2[l], lam_k2[l], da_norm_w[l], conv_w[l], conv_b[l],
              dt_bias[l], a_log[l], d_skip[l], ssm_norm_w[l], pool_w[l], pool_scale[l], w_out[l])
        zero_conv = jnp.zeros((bp, SSM_CONV - 1, CONV_CH), xp.dtype)
        zero_h = jnp.zeros((bp, SSM_HEADS, SSM_HEADDIM, SSM_STATE), jnp.float32)
        zero_pool = jnp.zeros((bp, POOL_BUF, POOL_WIDTH), xp.dtype)
        hp, (kp, vp, cp, sp, pp) = mixer(xp, 0, diff_attn_prompt, zero_conv, zero_h, zero_pool, lam_init, *mw)
        k_past = cache_k[l][page_table].reshape(bs, past, DA_HEADS, 2, DA_DK)
        v_past = cache_v[l][page_table].reshape(bs, past, DA_HEADS, DA_DV)
        attn_s = functools.partial(diff_attn_cached, k_past=k_past, v_past=v_past)
        hs, (ks, vs, cs, ss, ps) = mixer(xs, past, attn_s, state_conv[l], state_ssm[l], state_pool[l], lam_init, *mw)
        xp = layer_norm(ALPHA * xp + hp, ln1_g[l], ln1_b[l])
        xs = layer_norm(ALPHA * xs + hs, ln1_g[l], ln1_b[l])
        mk = (mem_prompt @ wk_mem[l]).reshape(bp, n_mem, MEM_HEADS, MEM_HD)
        mv = (mem_prompt @ wv_mem[l]).reshape(bp, n_mem, MEM_HEADS, MEM_HD)
        xp = layer_norm(ALPHA * xp + mem_attn(xp, mk, mv, wq_mem[l], wo_mem[l]), ln2_g[l], ln2_b[l])
        xs = layer_norm(ALPHA * xs + mem_attn(xs, cache_mem_k[l], cache_mem_v[l], wq_mem[l], wo_mem[l]), ln2_g[l], ln2_b[l])
        j = l // 2
        if l % 2 == 0:
            fp = swiglu(xp, ffn_w1[j], ffn_w3[j], ffn_w2[j])
            fs = swiglu(xs, ffn_w1[j], ffn_w3[j], ffn_w2[j])
        else:
            fp = moe_swiglu(xp, moe_router[j], moe_router_b[j], moe_w1[j], moe_w3[j], moe_w2[j])
            fs = moe_swiglu(xs, moe_router[j], moe_router_b[j], moe_w1[j], moe_w3[j], moe_w2[j])
        xp = layer_norm(ALPHA * xp + fp, ln3_g[l], ln3_b[l])
        xs = layer_norm(ALPHA * xs + fs, ln3_g[l], ln3_b[l])
        kp_l.append(kp.reshape(bp, -1, PAGE_SIZE, DA_HEADS, 2, DA_DK))
        vp_l.append(vp.reshape(bp, -1, PAGE_SIZE, DA_HEADS, DA_DV))
        ks_l.append(ks)
        vs_l.append(vs)
        mk_l.append(mk)
        mv_l.append(mv)
        hp_l.append(sp)
        hs_l.append(ss)
        cp_l.append(cp)
        cs_l.append(cs)
        pp_l.append(pp)
        ps_l.append(ps)
    k_prompt = jnp.stack(kp_l)
    v_prompt = jnp.stack(vp_l)
    k_sample = jnp.stack(ks_l)
    v_sample = jnp.stack(vs_l)
    mem_k_prompt = jnp.stack(mk_l)
    mem_v_prompt = jnp.stack(mv_l)
    ssm_prompt = jnp.stack(hp_l)
    ssm_sample = jnp.stack(hs_l)
    conv_prompt = jnp.stack(cp_l)
    conv_sample = jnp.stack(cs_l)
    pool_prompt = jnp.stack(pp_l)
    pool_sample = jnp.stack(ps_l)
    return (xp, xs, k_prompt, v_prompt, k_sample, v_sample, mem_k_prompt, mem_v_prompt,
            ssm_prompt, ssm_sample, conv_prompt, conv_sample, pool_prompt, pool_sample)
```

```python
import functools
import math

import jax
import jax.numpy as jnp
import numpy as np
from jax import lax
from jax.experimental import pallas as pl
from jax.experimental.pallas import tpu as pltpu

F32 = jnp.float32
BF16 = jnp.bfloat16

D_MODEL = 1024
DEPTH = 2
PAGE_SIZE = 128
DA_HEADS = 4
DA_WIDTH = 512
DA_DV = 128
DA_DK = 64
ROT_DIM = 16
ROPE_THETA = 500000.0
SSM_WIDTH = 256
SSM_HEADS = 4
SSM_HEADDIM = 64
SSM_STATE = 128
SSM_CONV = 4
SSM_CHUNK = 128
CONV_CH = 768
POOL_WIDTH = 256
POOL_WINDOWS = (2, 4, 8, 16)
POOL_GDIM = 64
POOL_BUF = 15
MEM_HEADS = 4
MEM_HD = 256
N_EXPERTS = 8
ALPHA = (2.0 * DEPTH) ** 0.25
LN_EPS = 1e-5
RMS_EPS = 1e-6

LANES = 128
DT_PAD = LANES
_C_Q, _C_K, _C_V, _C_Z, _C_X, _C_P, _C_DT, _C_END = 0, 512, 1024, 1536, 1792, 2560, 2816, 2944

NEG_INF = float("-inf")


def _cparams(sem, vmem_mb=48):
    return pltpu.CompilerParams(dimension_semantics=sem, vmem_limit_bytes=vmem_mb * 1024 * 1024)


def _dot(a, b):
    return jnp.dot(a.astype(BF16), b.astype(BF16), preferred_element_type=F32)


def _dot_nt(a, b):
    return lax.dot_general(a.astype(BF16), b.astype(BF16), (((1,), (1,)), ((), ())),
                           preferred_element_type=F32)


def _layer_norm(x, g, b):
    mu = jnp.mean(x, axis=-1, keepdims=True)
    xc = x - mu
    var = jnp.mean(xc * xc, axis=-1, keepdims=True)
    return xc * lax.rsqrt(var + LN_EPS) * g + b


def _silu(x):
    return x * (1.0 / (1.0 + jnp.exp(-x)))


def _softplus(x):
    return jnp.maximum(x, 0.0) + jnp.log(1.0 + jnp.exp(-jnp.abs(x)))


def _const_spec(shape):
    nd = len(shape)
    return pl.BlockSpec(shape, lambda *_: (0,) * nd)


def _rope_tables(pos):
    half = ROT_DIM // 2
    inv = ROPE_THETA ** (-jnp.arange(half, dtype=F32) * 2.0 / ROT_DIM)
    ang = pos.astype(F32)[:, None] * inv[None, :]
    cos, sin = jnp.cos(ang), jnp.sin(ang)
    n = pos.shape[0]
    c64 = jnp.concatenate([cos, cos, jnp.ones((n, DA_DK - ROT_DIM), F32)], axis=1)
    s1 = jnp.concatenate([-sin, jnp.zeros((n, DA_DK - half), F32)], axis=1)
    s2 = jnp.concatenate([jnp.zeros((n, half), F32), sin, jnp.zeros((n, DA_DK - ROT_DIM), F32)], axis=1)
    two = lambda t: jnp.concatenate([t, t], axis=1)
    return two(c64), two(s1), two(s2)


def _proj_in_kernel(x_ref, w_ref, c_ref, s1_ref, s2_ref, *out_refs, prompt):
    xb = x_ref[...].astype(BF16)
    cc, s1, s2 = c_ref[...], s1_ref[...], s2_ref[...]

    def rope(t):
        outs = []
        for h in range(DA_HEADS):
            th = t[:, h * LANES:(h + 1) * LANES]
            outs.append(th * cc + pltpu.roll(th, LANES - ROT_DIM // 2, 1) * s1
                        + pltpu.roll(th, ROT_DIM // 2, 1) * s2)
        return jnp.concatenate(outs, axis=1)

    def seg(a, b):
        return jnp.dot(xb, w_ref[:, a:b], preferred_element_type=F32)

    q = rope(seg(_C_Q, _C_K)) * (DA_DK ** -0.5)
    k = rope(seg(_C_K, _C_V))
    v = seg(_C_V, _C_Z)
    if prompt:
        qb_ref, k_ref, kb_ref, v_ref, vb_ref, z_ref, xbc_ref, pu_ref, dt_ref = out_refs
        qb_ref[...] = q.astype(BF16)
        kb_ref[...] = k.astype(BF16)
        vb_ref[...] = v.astype(BF16)
    else:
        q_ref, k_ref, v_ref, z_ref, xbc_ref, pu_ref, dt_ref = out_refs
        q_ref[...] = q
    k_ref[...] = k
    v_ref[...] = v
    z_ref[...] = seg(_C_Z, _C_X)
    xbc_ref[...] = seg(_C_X, _C_P)
    pu_ref[...] = seg(_C_P, _C_DT)
    dt_ref[...] = seg(_C_DT, _C_END)


def _proj_in(x, w_cat, tables, *, prompt, tm, seq):
    m = x.shape[0]
    grid = (m // tm,)
    row = lambda w: pl.BlockSpec((tm, w), lambda i: (i, 0))
    if prompt:
        npos = seq // tm
        tab = pl.BlockSpec((tm, LANES), lambda i: (i % npos, 0))
        widths = [(DA_WIDTH, BF16), (DA_WIDTH, F32), (DA_WIDTH, BF16), (DA_WIDTH, F32), (DA_WIDTH, BF16)]
    else:
        tab = _const_spec((1, LANES))
        widths = [(DA_WIDTH, F32), (DA_WIDTH, F32), (DA_WIDTH, F32)]
    widths += [(SSM_WIDTH, F32), (CONV_CH, F32), (POOL_WIDTH, F32), (DT_PAD, F32)]
    return pl.pallas_call(
        functools.partial(_proj_in_kernel, prompt=prompt),
        grid=grid,
        in_specs=[row(D_MODEL), _const_spec(w_cat.shape), tab, tab, tab],
        out_specs=[row(w) for w, _ in widths],
        out_shape=[jax.ShapeDtypeStruct((m, w), dt) for w, dt in widths],
        compiler_params=_cparams(("parallel",)),
        name="proj_in",
    )(x, w_cat, *tables)


def _lam_value(lq1, lk1, lq2, lk2, lam_init):
    return (jnp.exp(jnp.sum(lq1 * lk1, axis=1, keepdims=True))
            - jnp.exp(jnp.sum(lq2 * lk2, axis=1, keepdims=True)) + lam_init)


def _flash_kernel(q_ref, k_ref, v_ref, lq1_ref, lk1_ref, lq2_ref, lk2_ref, nw_ref, o_ref, *, tq, tk, lam_init):
    i = pl.program_id(2)
    q = q_ref[...]
    lane = lax.broadcasted_iota(jnp.int32, q.shape, 1)
    zero = jnp.zeros_like(q)
    qs = (jnp.where(lane < DA_DK, q, zero), jnp.where(lane >= DA_DK, q, zero))

    def step(kt, vt, carry, mask):
        new = []
        for c in range(2):
            m, l, acc = carry[c]
            s = lax.dot_general(qs[c], kt, (((1,), (1,)), ((), ())), preferred_element_type=F32)
            if mask is not None:
                s = jnp.where(mask, s, NEG_INF)
            m_new = jnp.maximum(m, jnp.max(s, axis=1, keepdims=True))
            p = jnp.exp(s - m_new)
            alpha = jnp.exp(m - m_new)
            l = alpha * l + jnp.sum(p, axis=1, keepdims=True)
            acc = alpha * acc + jnp.dot(p.astype(BF16), vt, preferred_element_type=F32)
            new.append((m_new, l, acc))
        return tuple(new)

    one = (jnp.full((tq, 1), NEG_INF, F32), jnp.zeros((tq, 1), F32), jnp.zeros((tq, DA_DV), F32))
    ratio = tq // tk

    def body(j, carry):
        off = pl.multiple_of(j * tk, tk)
        return step(k_ref[pl.ds(off, tk), :], v_ref[pl.ds(off, tk), :], carry, None)

    carry = lax.fori_loop(0, i * ratio, body, (one, one))
    row = lax.broadcasted_iota(jnp.int32, (tq, tk), 0)
    col = lax.broadcasted_iota(jnp.int32, (tq, tk), 1)
    for d in range(ratio):
        off = pl.multiple_of(i * tq + d * tk, tk)
        carry = step(k_ref[pl.ds(off, tk), :], v_ref[pl.ds(off, tk), :], carry, col + d * tk <= row)

    (_, l0, a0), (_, l1, a1) = carry
    lam = _lam_value(lq1_ref[...], lk1_ref[...], lq2_ref[...], lk2_ref[...], lam_init)
    o = a0 / l0 - lam * (a1 / l1)
    ms = jnp.mean(o * o, axis=1, keepdims=True)
    o_ref[...] = o * lax.rsqrt(ms + RMS_EPS) * nw_ref[...] * (1.0 - lam_init)


def _flash_attn(qb, kb, vb, lam_vecs, norm_w, *, batch, seq, tq, tk, lam_init):
    q3 = qb.reshape(batch, seq, DA_WIDTH)
    k3 = kb.reshape(batch, seq, DA_WIDTH)
    v3 = vb.reshape(batch, seq, DA_WIDTH)
    qspec = pl.BlockSpec((None, tq, DA_DV), lambda b, h, i: (b, i, h))
    kvspec = pl.BlockSpec((None, seq, DA_DV), lambda b, h, i: (b, 0, h))
    vec = _const_spec((1, DA_DK))
    out = pl.pallas_call(
        functools.partial(_flash_kernel, tq=tq, tk=tk, lam_init=lam_init),
        grid=(batch, DA_HEADS, seq // tq),
        in_specs=[qspec, kvspec, kvspec, vec, vec, vec, vec, _const_spec((1, DA_DV))],
        out_specs=qspec,
        out_shape=jax.ShapeDtypeStruct((batch, seq, DA_WIDTH), F32),
        compiler_params=_cparams(("parallel", "parallel", "arbitrary")),
        name="flash_diff_attn",
    )(q3, k3, v3, *lam_vecs, norm_w)
    return out.reshape(batch * seq, DA_WIDTH)


def _paged_kernel(pt_ref, q_ref, ks_ref, vs_ref, lq1_ref, lk1_ref, lq2_ref, lk2_ref, nw_ref, *rest,
                  n_pages, lam_init):
    k_refs = rest[:n_pages]
    v_refs = rest[n_pages:2 * n_pages]
    o_ref = rest[2 * n_pages]
    q = q_ref[...]
    nhc = 2 * DA_HEADS
    jj = lax.broadcasted_iota(jnp.int32, (2 * nhc, DA_WIDTH), 0)
    rr = lax.broadcasted_iota(jnp.int32, (2 * nhc, DA_WIDTH), 1)
    sel = jnp.where(jj == ((rr >> 6) & 1) * DA_HEADS + (rr >> 7), 1.0, 0.0).astype(BF16)

    def scores(kt):
        prod = (kt * q).astype(BF16)
        return lax.dot_general(sel, prod, (((1,), (1,)), ((), ())), preferred_element_type=F32)[:nhc]

    s_all = jnp.concatenate([scores(k_refs[j][...]) for j in range(n_pages)], axis=1)
    s_self = scores(jnp.broadcast_to(ks_ref[...], (PAGE_SIZE, DA_WIDTH)))[:, 0:1]
    m = jnp.maximum(jnp.max(s_all, axis=1, keepdims=True), s_self)
    p = jnp.exp(s_all - m)
    p_self = jnp.exp(s_self - m)
    inv_l = 1.0 / (jnp.sum(p, axis=1, keepdims=True) + p_self)
    lam = _lam_value(lq1_ref[...], lk1_ref[...], lq2_ref[...], lk2_ref[...], lam_init)
    pn = p * inv_l
    pn_self = p_self * inv_l
    w = pn - lam * pltpu.roll(pn, DA_HEADS, 0)
    w_self = pn_self - lam * pltpu.roll(pn_self, DA_HEADS, 0)
    w16 = jnp.concatenate([w, jnp.zeros_like(w)], axis=0).astype(BF16)
    acc = w_self * vs_ref[...]
    for j in range(n_pages):
        acc = acc + jnp.dot(w16[:, j * PAGE_SIZE:(j + 1) * PAGE_SIZE], v_refs[j][...].astype(BF16),
                            preferred_element_type=F32)[:nhc]
    hh = lax.broadcasted_iota(jnp.int32, (nhc, DA_WIDTH), 0)
    ll = lax.broadcasted_iota(jnp.int32, (nhc, DA_WIDTH), 1)
    om = jnp.where(hh == (ll >> 7), acc, 0.0)
    ms = jnp.sum(om * om, axis=1, keepdims=True) * (1.0 / DA_DV)
    on = om * lax.rsqrt(ms + RMS_EPS)
    o_ref[...] = jnp.sum(on, axis=0, keepdims=True) * nw_ref[...] * (1.0 - lam_init)


def _paged_attn(q, k_self, v_self, cache_k4, cache_v4, page_table, lam_vecs, norm_w4, *, layer, lam_init):
    bs, n_pages = page_table.shape
    pt = page_table.reshape(-1)
    one = pl.BlockSpec((None, 1, DA_WIDTH), lambda b, pt: (b, 0, 0))
    vec = pl.BlockSpec((1, DA_DK), lambda b, pt: (0, 0))

    def page_spec(j):
        return pl.BlockSpec((None, None, PAGE_SIZE, DA_WIDTH), lambda b, pt: (layer, pt[b * n_pages + j], 0, 0))

    specs = [one, one, one, vec, vec, vec, vec, pl.BlockSpec((1, DA_WIDTH), lambda b, pt: (0, 0))]
    specs += [page_spec(j) for j in range(n_pages)] * 2
    out = pl.pallas_call(
        functools.partial(_paged_kernel, n_pages=n_pages, lam_init=lam_init),
        grid_spec=pltpu.PrefetchScalarGridSpec(
            num_scalar_prefetch=1, grid=(bs,), in_specs=specs, out_specs=one),
        out_shape=jax.ShapeDtypeStruct((bs, 1, DA_WIDTH), F32),
        compiler_params=_cparams(("arbitrary",)),
        name="paged_diff_attn",
    )(pt, q.reshape(bs, 1, DA_WIDTH), k_self.reshape(bs, 1, DA_WIDTH), v_self.reshape(bs, 1, DA_WIDTH),
      *lam_vecs, norm_w4, *([cache_k4] * n_pages), *([cache_v4] * n_pages))
    return out.reshape(bs, DA_WIDTH)


def _head_expand(cols, width, per):
    rows = cols.shape[0]
    lane = lax.broadcasted_iota(jnp.int32, (rows, width), 1)
    out = jnp.broadcast_to(cols[:, 0:1], (rows, width))
    for h in range(1, width // per):
        out = jnp.where(lane >= h * per, cols[:, h:h + 1], out)
    return out


def _ssd_kernel(xbc_ref, z_ref, dt_ref, cw_ref, cb_ref, dtb_ref, alog_ref, dsk_ref, nw_ref,
                y_ref, st_ref, tail_sc, state_sc):
    c = pl.program_id(1)
    q = SSM_CHUNK

    @pl.when(c == 0)
    def _():
        tail_sc[...] = jnp.zeros_like(tail_sc)
        state_sc[...] = jnp.zeros_like(state_sc)

    u = xbc_ref[...]
    full = jnp.concatenate([tail_sc[...], u], axis=0)
    tail_sc[...] = u[q - 8:, :]
    conv = full * cw_ref[SSM_CONV - 1:SSM_CONV, :]
    for j in range(1, SSM_CONV):
        conv = conv + pltpu.roll(full, j, 0) * cw_ref[SSM_CONV - 1 - j:SSM_CONV - j, :]
    xbc = _silu(conv[8:, :] + cb_ref[...])
    xs = xbc[:, :SSM_WIDTH]
    bm = (xbc[:, SSM_WIDTH:SSM_WIDTH + SSM_STATE], xbc[:, SSM_WIDTH + SSM_STATE:SSM_WIDTH + 2 * SSM_STATE])
    cm = (xbc[:, SSM_WIDTH + 2 * SSM_STATE:SSM_WIDTH + 3 * SSM_STATE], xbc[:, SSM_WIDTH + 3 * SSM_STATE:])

    dt = _softplus(dt_ref[...] + dtb_ref[...])
    dta = dt * (-jnp.exp(alog_ref[...]))
    ti = lax.broadcasted_iota(jnp.int32, (q, q), 0)
    si = lax.broadcasted_iota(jnp.int32, (q, q), 1)
    causal = si <= ti
    tri = jnp.where(causal, 1.0, 0.0)
    a_cs = jnp.dot(tri, dta, precision=lax.Precision.HIGHEST, preferred_element_type=F32)
    a_cs_t = a_cs.T
    last = a_cs[q - 1:q, :]

    dt_x = _head_expand(dt, SSM_WIDTH, SSM_HEADDIM)
    xdt = xs * dt_x
    xdt_end = xdt * _head_expand(jnp.exp(last - a_cs), SSM_WIDTH, SSM_HEADDIM)
    lane = lax.broadcasted_iota(jnp.int32, (q, SSM_WIDTH), 1)

    cb = [_dot_nt(cm[g], bm[g]) for g in range(2)]
    y_diag = jnp.zeros((q, SSM_WIDTH), F32)
    for h in range(SSM_HEADS):
        seg = a_cs[:, h:h + 1] - a_cs_t[h:h + 1, :]
        decay = jnp.exp(jnp.where(causal, seg, NEG_INF))
        xh = jnp.where((lane >= h * SSM_HEADDIM) & (lane < (h + 1) * SSM_HEADDIM), xdt, 0.0)
        y_diag = y_diag + _dot(cb[h // 2] * decay, xh)

    state = state_sc[...]
    sb = state.astype(BF16)
    y_off = jnp.where(lane < 2 * SSM_HEADDIM, _dot_nt(cm[0], sb), _dot_nt(cm[1], sb))
    y_off = y_off * _head_expand(jnp.exp(a_cs), SSM_WIDTH, SSM_HEADDIM)

    xt = xdt_end.T
    rowi = lax.broadcasted_iota(jnp.int32, (SSM_WIDTH, SSM_STATE), 0)
    new = jnp.where(rowi < 2 * SSM_HEADDIM, _dot(xt, bm[0]), _dot(xt, bm[1]))
    e_last = jnp.exp(last)
    dec_rows = jnp.broadcast_to(e_last[:, 0:1], (SSM_WIDTH, SSM_STATE))
    for h in range(1, SSM_HEADS):
        dec_rows = jnp.where(rowi >= h * SSM_HEADDIM, e_last[:, h:h + 1], dec_rows)
    state = state * dec_rows + new
    state_sc[...] = state

    y = y_diag + y_off + dsk_ref[...] * xs
    y = y * _silu(z_ref[...])
    ms = jnp.mean(y * y, axis=1, keepdims=True)
    y_ref[...] = y * lax.rsqrt(ms + RMS_EPS) * nw_ref[...]

    @pl.when(c == pl.num_programs(1) - 1)
    def _():
        st_ref[...] = state


def _ssd_prompt(xbc, z, dt_raw, conv_w, conv_b, dt_bias, a_log, d_skip_x, norm_w, *, batch, seq):
    nc = seq // SSM_CHUNK
    blk = lambda w: pl.BlockSpec((None, SSM_CHUNK, w), lambda b, c: (b, c, 0))
    y, st = pl.pallas_call(
        _ssd_kernel,
        grid=(batch, nc),
        in_specs=[blk(CONV_CH), blk(SSM_WIDTH), blk(DT_PAD),
                  _const_spec((SSM_CONV, CONV_CH)), _const_spec((1, CONV_CH)),
                  _const_spec((1, DT_PAD)), _const_spec((1, DT_PAD)),
                  _const_spec((1, SSM_WIDTH)), _const_spec((1, SSM_WIDTH))],
        out_specs=[blk(SSM_WIDTH), pl.BlockSpec((None, SSM_WIDTH, SSM_STATE), lambda b, c: (b, 0, 0))],
        out_shape=[jax.ShapeDtypeStruct((batch, seq, SSM_WIDTH), F32),
                   jax.ShapeDtypeStruct((batch, SSM_WIDTH, SSM_STATE), F32)],
        scratch_shapes=[pltpu.VMEM((8, CONV_CH), F32), pltpu.VMEM((SSM_WIDTH, SSM_STATE), F32)],
        compiler_params=_cparams(("parallel", "arbitrary")),
        name="ssd_prompt",
    )(xbc.reshape(batch, seq, CONV_CH), z.reshape(batch, seq, SSM_WIDTH), dt_raw.reshape(batch, seq, DT_PAD),
      conv_w, conv_b, dt_bias, a_log, d_skip_x, norm_w)
    return y.reshape(batch * seq, SSM_WIDTH), st.reshape(batch, SSM_HEADS, SSM_HEADDIM, SSM_STATE)


def _pool_select(sums, inv_cnt, lane):
    out = sums[POOL_WINDOWS[0]] * inv_cnt[POOL_WINDOWS[0]]
    for g in range(1, len(POOL_WINDOWS)):
        w = POOL_WINDOWS[g]
        out = jnp.where(lane >= g * POOL_GDIM, sums[w] * inv_cnt[w], out)
    return out


def _pool_kernel(u_ref, w_ref, sc_ref, y_ref, tail_sc, *, tp):
    c = pl.program_id(1)
    halo = 16

    @pl.when(c == 0)
    def _():
        tail_sc[...] = jnp.zeros_like(tail_sc)

    u = u_ref[...]
    full = jnp.concatenate([tail_sc[...], u], axis=0)
    tail_sc[...] = u[tp - halo:, :]
    sums = {}
    run = full
    w = 1
    while w < max(POOL_WINDOWS):
        run = run + pltpu.roll(run, w, 0)
        w *= 2
        sums[w] = run[halo:, :]
    pos = c * tp + lax.broadcasted_iota(jnp.int32, (tp, 1), 0)
    inv_cnt = {w: 1.0 / jnp.minimum(w, pos + 1).astype(F32) for w in POOL_WINDOWS}
    lane = lax.broadcasted_iota(jnp.int32, (tp, POOL_WIDTH), 1)
    d = _pool_select(sums, inv_cnt, lane) - u
    y_ref[...] = _dot(d, w_ref[...]) * sc_ref[...]


def _pool_prompt(pu, pool_wbd, pool_scale, *, batch, seq, tp):
    blk = pl.BlockSpec((None, tp, POOL_WIDTH), lambda b, c: (b, c, 0))
    y = pl.pallas_call(
        functools.partial(_pool_kernel, tp=tp),
        grid=(batch, seq // tp),
        in_specs=[blk, _const_spec((POOL_WIDTH, POOL_WIDTH)), _const_spec((1, POOL_WIDTH))],
        out_specs=blk,
        out_shape=jax.ShapeDtypeStruct((batch, seq, POOL_WIDTH), F32),
        scratch_shapes=[pltpu.VMEM((16, POOL_WIDTH), F32)],
        compiler_params=_cparams(("parallel", "arbitrary")),
        name="pool_prompt",
    )(pu.reshape(batch, seq, POOL_WIDTH), pool_wbd, pool_scale)
    return y.reshape(batch * seq, POOL_WIDTH)


def _sample_prep_kernel(xbc_ref, cbuf_ref, cw_ref, cb_ref, dt_ref, dtb_ref, alog_ref, pu_ref, pbuf_ref,
                        pw_ref, psc_ref, xact_ref, dto_ref, dec_ref, yp_ref, *, pos0):
    u = xbc_ref[...]
    conv = u * cw_ref[SSM_CONV - 1:SSM_CONV, :] + cb_ref[...]
    for j in range(SSM_CONV - 1):
        conv = conv + cbuf_ref[:, j * CONV_CH:(j + 1) * CONV_CH] * cw_ref[j:j + 1, :]
    xact_ref[...] = _silu(conv)
    dt = _softplus(dt_ref[...] + dtb_ref[...])
    dto_ref[...] = dt
    dec_ref[...] = jnp.exp(dt * (-jnp.exp(alog_ref[...])))
    pu = pu_ref[...]
    run = pu
    sums = {}
    for j in range(1, max(POOL_WINDOWS)):
        run = run + pbuf_ref[:, (POOL_BUF - j) * POOL_WIDTH:(POOL_BUF - j + 1) * POOL_WIDTH]
        if j + 1 in POOL_WINDOWS:
            sums[j + 1] = run
    inv_cnt = {w: 1.0 / float(min(w, pos0 + 1)) for w in POOL_WINDOWS}
    lane = lax.broadcasted_iota(jnp.int32, pu.shape, 1)
    d = _pool_select(sums, inv_cnt, lane) - pu
    yp_ref[...] = _dot(d, pw_ref[...]) * psc_ref[...]


def _sample_prep(xbc, conv_buf, conv_w, conv_b, dt_raw, dt_bias, a_log, pu, pool_buf, pool_wbd, pool_scale, *, pos0):
    bs = xbc.shape[0]
    args = (xbc, conv_buf.reshape(bs, -1), conv_w, conv_b, dt_raw, dt_bias, a_log, pu, pool_buf.reshape(bs, -1),
            pool_wbd, pool_scale)
    shapes = [(bs, CONV_CH), (bs, DT_PAD), (bs, DT_PAD), (bs, POOL_WIDTH)]
    return pl.pallas_call(
        functools.partial(_sample_prep_kernel, pos0=pos0),
        grid=(1,),
        in_specs=[_const_spec(a.shape) for a in args],
        out_specs=[_const_spec(s) for s in shapes],
        out_shape=[jax.ShapeDtypeStruct(s, F32) for s in shapes],
        compiler_params=_cparams(("arbitrary",)),
        name="sample_prep",
    )(*args)


def _ssd_step_kernel(s_ref, x_ref, z_ref, dt_ref, dec_ref, b_ref, c_ref, dsk_ref, nw_ref, y_ref, so_ref):
    rowi = lax.broadcasted_iota(jnp.int32, (SSM_WIDTH, SSM_STATE), 0)
    brow, crow = b_ref[...], c_ref[...]
    bsel = jnp.where(rowi < 2 * SSM_HEADDIM, brow[:, :SSM_STATE], brow[:, SSM_STATE:])
    csel = jnp.where(rowi < 2 * SSM_HEADDIM, crow[:, :SSM_STATE], crow[:, SSM_STATE:])
    x = x_ref[...]
    sn = s_ref[...] * dec_ref[...] + (x * dt_ref[...]) * bsel
    so_ref[...] = sn
    y = jnp.sum(sn * csel, axis=1, keepdims=True) + dsk_ref[...] * x
    y = y * _silu(z_ref[...])
    ms = jnp.sum(y * y, axis=0, keepdims=True) * (1.0 / SSM_WIDTH)
    y_ref[...] = y * lax.rsqrt(ms + RMS_EPS) * nw_ref[...]


def _ssd_step(state_all, layer, xact, z, dt, dec, d_skip_x, norm_w):
    bs = xact.shape[0]
    rep = lambda t: jnp.repeat(t[:, :SSM_HEADS], SSM_HEADDIM, axis=1).reshape(bs, SSM_WIDTH, 1)
    colb = pl.BlockSpec((None, SSM_WIDTH, 1), lambda b: (b, 0, 0))
    rowb = pl.BlockSpec((None, 1, 2 * SSM_STATE), lambda b: (b, 0, 0))
    st_in = pl.BlockSpec((None, SSM_WIDTH, SSM_STATE), lambda b: (layer * bs + b, 0, 0))
    st_out = pl.BlockSpec((None, SSM_WIDTH, SSM_STATE), lambda b: (b, 0, 0))
    y, st = pl.pallas_call(
        _ssd_step_kernel,
        grid=(bs,),
        in_specs=[st_in, colb, colb, colb, colb, rowb, rowb,
                  _const_spec((SSM_WIDTH, 1)), _const_spec((SSM_WIDTH, 1))],
        out_specs=[colb, st_out],
        out_shape=[jax.ShapeDtypeStruct((bs, SSM_WIDTH, 1), F32),
                   jax.ShapeDtypeStruct((bs, SSM_WIDTH, SSM_STATE), F32)],
        compiler_params=_cparams(("parallel",)),
        name="ssd_step",
    )(state_all, xact[:, :SSM_WIDTH].reshape(bs, SSM_WIDTH, 1), z.reshape(bs, SSM_WIDTH, 1), rep(dt), rep(dec),
      xact[:, SSM_WIDTH:SSM_WIDTH + 2 * SSM_STATE].reshape(bs, 1, 2 * SSM_STATE),
      xact[:, SSM_WIDTH + 2 * SSM_STATE:].reshape(bs, 1, 2 * SSM_STATE),
      d_skip_x.reshape(SSM_WIDTH, 1), norm_w.reshape(SSM_WIDTH, 1))
    return y.reshape(bs, SSM_WIDTH), st.reshape(bs, SSM_HEADS, SSM_HEADDIM, SSM_STATE)


def _mm_kernel(x_ref, w_ref, *out_refs, scale):
    y = _dot(x_ref[...], w_ref[...])
    if scale != 1.0:
        y = y * scale
    for r in out_refs:
        r[...] = y.astype(r.dtype)


def _matmul(x, w, *, tm, scale=1.0, dtypes=(F32,)):
    m, k = x.shape
    n = w.shape[1]
    res = pl.pallas_call(
        functools.partial(_mm_kernel, scale=scale),
        grid=(m // tm,),
        in_specs=[pl.BlockSpec((tm, k), lambda i: (i, 0)), _const_spec(w.shape)],
        out_specs=[pl.BlockSpec((tm, n), lambda i: (i, 0)) for _ in dtypes],
        out_shape=[jax.ShapeDtypeStruct((m, n), d) for d in dtypes],
        compiler_params=_cparams(("parallel",)),
        name="matmul",
    )(x, w)
    return res


def _mm_res_ln_kernel(*refs, n_in):
    a_refs = refs[:n_in]
    w_refs = refs[n_in:2 * n_in]
    x_ref, g_ref, b_ref, o_ref = refs[2 * n_in:]
    h = _dot(a_refs[0][...], w_refs[0][...])
    for a, w in zip(a_refs[1:], w_refs[1:]):
        h = h + _dot(a[...], w[...])
    o_ref[...] = _layer_norm(ALPHA * x_ref[...] + h, g_ref[...], b_ref[...])


def _mm_res_ln(acts, weights, x, g, b, *, tm):
    m = x.shape[0]
    row = lambda a: pl.BlockSpec((tm, a.shape[1]), lambda i: (i, 0))
    return pl.pallas_call(
        functools.partial(_mm_res_ln_kernel, n_in=len(acts)),
        grid=(m // tm,),
        in_specs=[row(a) for a in acts] + [_const_spec(w.shape) for w in weights]
        + [row(x), _const_spec(g.shape), _const_spec(b.shape)],
        out_specs=row(x),
        out_shape=jax.ShapeDtypeStruct(x.shape, F32),
        compiler_params=_cparams(("parallel",)),
        name="mm_res_ln",
    )(*acts, *weights, x, g, b)


def _mem_prompt_kernel(x_ref, mk_ref, mv_ref, wq_ref, wo_ref, g_ref, b_ref, o_ref):
    x = x_ref[...]
    q = (_dot(x, wq_ref[...]) * (MEM_HD ** -0.5)).astype(BF16)
    outs = []
    for h in range(MEM_HEADS):
        sl = slice(h * MEM_HD, (h + 1) * MEM_HD)
        s = lax.dot_general(q[:, sl], mk_ref[:, sl], (((1,), (1,)), ((), ())), preferred_element_type=F32)
        p = jnp.exp(s - jnp.max(s, axis=1, keepdims=True))
        p = p * (1.0 / jnp.sum(p, axis=1, keepdims=True))
        outs.append(jnp.dot(p.astype(BF16), mv_ref[:, sl], preferred_element_type=F32))
    o = jnp.concatenate(outs, axis=1)
    o_ref[...] = _layer_norm(ALPHA * x + _dot(o, wo_ref[...]), g_ref[...], b_ref[...])


def _mem_attn_prompt(x, mkb, mvb, wq, wo, g, b, *, batch, seq, tm):
    n_mem = mkb.shape[0] // batch
    xblk = pl.BlockSpec((None, tm, D_MODEL), lambda bb, i: (bb, i, 0))
    mblk = pl.BlockSpec((None, n_mem, D_MODEL), lambda bb, i: (bb, 0, 0))
    out = pl.pallas_call(
        _mem_prompt_kernel,
        grid=(batch, seq // tm),
        in_specs=[xblk, mblk, mblk, _const_spec(wq.shape), _const_spec(wo.shape),
                  _const_spec(g.shape), _const_spec(b.shape)],
        out_specs=xblk,
        out_shape=jax.ShapeDtypeStruct((batch, seq, D_MODEL), F32),
        compiler_params=_cparams(("parallel", "parallel")),
        name="mem_attn_prompt",
    )(x.reshape(batch, seq, D_MODEL), mkb.reshape(batch, n_mem, D_MODEL), mvb.reshape(batch, n_mem, D_MODEL),
      wq, wo, g, b)
    return out.reshape(batch * seq, D_MODEL)


def _mem_sample_kernel(q_ref, mk_ref, mv_ref, o_ref, *, group):
    for i in range(group):
        q = q_ref[i]
        outs = []
        for h in range(MEM_HEADS):
            sl = slice(h * MEM_HD, (h + 1) * MEM_HD)
            s = jnp.sum(mk_ref[i, :, sl] * q[:, sl], axis=1, keepdims=True)
            p = jnp.exp(s - jnp.max(s, axis=0, keepdims=True))
            p = p * (1.0 / jnp.sum(p, axis=0, keepdims=True))
            outs.append(jnp.sum(mv_ref[i, :, sl] * p, axis=0, keepdims=True))
        o_ref[i] = jnp.concatenate(outs, axis=1)


def _mem_attn_sample(q, mem_k_all, mem_v_all, *, layer, group):
    bs = q.shape[0]
    n_mem = mem_k_all.shape[1]
    nb = bs // group
    qblk = pl.BlockSpec((group, 1, D_MODEL), lambda i: (i, 0, 0))
    mblk = pl.BlockSpec((group, n_mem, D_MODEL), lambda i: (layer * nb + i, 0, 0))
    out = pl.pallas_call(
        functools.partial(_mem_sample_kernel, group=group),
        grid=(nb,),
        in_specs=[qblk, mblk, mblk],
        out_specs=qblk,
        out_shape=jax.ShapeDtypeStruct((bs, 1, D_MODEL), F32),
        compiler_params=_cparams(("parallel",)),
        name="mem_attn_sample",
    )(q.reshape(bs, 1, D_MODEL), mem_k_all, mem_v_all)
    return out.reshape(bs, D_MODEL)


def _ffn_kernel(x_ref, w1_ref, w3_ref, w2_ref, g_ref, b_ref, o_ref, *, n_chunks):
    x = x_ref[...]
    xb = x.astype(BF16)
    acc = jnp.zeros(x.shape, F32)
    for c in range(n_chunks):
        h = _silu(jnp.dot(xb, w1_ref[c], preferred_element_type=F32)) * jnp.dot(xb, w3_ref[c],
                                                                                 preferred_element_type=F32)
        acc = acc + jnp.dot(h.astype(BF16), w2_ref[c], preferred_element_type=F32)
    o_ref[...] = _layer_norm(ALPHA * x + acc, g_ref[...], b_ref[...])


def _ffn(x, w1c, w3c, w2c, g, b, *, tm):
    m = x.shape[0]
    row = pl.BlockSpec((tm, D_MODEL), lambda i: (i, 0))
    wspec = lambda w: pl.BlockSpec(w.shape, lambda i: (0, 0, 0), pipeline_mode=pl.Buffered(1))
    return pl.pallas_call(
        functools.partial(_ffn_kernel, n_chunks=w1c.shape[0]),
        grid=(m // tm,),
        in_specs=[row, wspec(w1c), wspec(w3c), wspec(w2c), _const_spec(g.shape), _const_spec(b.shape)],
        out_specs=row,
        out_shape=jax.ShapeDtypeStruct(x.shape, F32),
        compiler_params=_cparams(("parallel",), vmem_mb=56),
        name="ffn_dense",
    )(x, w1c, w3c, w2c, g, b)


def _router_kernel(x_ref, w_ref, b_ref, o_ref):
    logits = jnp.dot(x_ref[...], w_ref[...], precision=lax.Precision.HIGHEST, preferred_element_type=F32)
    lane = lax.broadcasted_iota(jnp.int32, logits.shape, 1)
    logits = jnp.where(lane < N_EXPERTS, logits + b_ref[...], NEG_INF)
    m1 = jnp.max(logits, axis=1, keepdims=True)
    i1 = jnp.min(jnp.where(logits == m1, lane, LANES), axis=1, keepdims=True)
    rest = jnp.where(lane == i1, NEG_INF, logits)
    m2 = jnp.max(rest, axis=1, keepdims=True)
    i2 = jnp.min(jnp.where(rest == m2, lane, LANES), axis=1, keepdims=True)
    e = jnp.exp(m2 - m1)
    g1 = 1.0 / (1.0 + e)
    g2 = e * g1
    out = jnp.where(lane == 0, i1.astype(F32), jnp.where(lane == 1, i2.astype(F32),
                    jnp.where(lane == 2, g1, jnp.where(lane == 3, g2, 0.0))))
    o_ref[...] = out


def _router(x, w_pad, b_pad, *, tm):
    m = x.shape[0]
    return pl.pallas_call(
        _router_kernel,
        grid=(m // tm,),
        in_specs=[pl.BlockSpec((tm, D_MODEL), lambda i: (i, 0)), _const_spec(w_pad.shape), _const_spec(b_pad.shape)],
        out_specs=pl.BlockSpec((tm, LANES), lambda i: (i, 0)),
        out_shape=jax.ShapeDtypeStruct((m, LANES), F32),
        compiler_params=_cparams(("parallel",)),
        name="moe_router",
    )(x, w_pad, b_pad)


def _expert_kernel(te_ref, tv_ref, x_ref, w1_ref, w3_ref, w2_ref, o_ref, acc_sc):
    i = pl.program_id(0)
    j = pl.program_id(1)

    @pl.when(tv_ref[i] == 1)
    def _():
        xb = x_ref[...].astype(BF16)
        h = _silu(jnp.dot(xb, w1_ref[...], preferred_element_type=F32)) * jnp.dot(xb, w3_ref[...],
                                                                                  preferred_element_type=F32)
        part = jnp.dot(h.astype(BF16), w2_ref[...], preferred_element_type=F32)

        @pl.when(j == 0)
        def _():
            acc_sc[...] = part

        @pl.when(j > 0)
        def _():
            acc_sc[...] = acc_sc[...] + part

    @pl.when(j == pl.num_programs(1) - 1)
    def _():
        o_ref[...] = acc_sc[...]


def _experts(x_sorted, tile_expert, tile_valid, w1, w3, w2, *, tm, fc):
    n_slots = x_sorted.shape[0]
    nj = w1.shape[2] // fc
    last = nj - 1

    def jj(i, j, tv):
        return jnp.where(tv[i] == 1, j, last)

    xs = pl.BlockSpec((tm, D_MODEL), lambda i, j, te, tv: (i, 0))
    w13 = pl.BlockSpec((None, D_MODEL, fc), lambda i, j, te, tv: (te[i], 0, jj(i, j, tv)))
    w2s = pl.BlockSpec((None, fc, D_MODEL), lambda i, j, te, tv: (te[i], jj(i, j, tv), 0))
    return pl.pallas_call(
        _expert_kernel,
        grid_spec=pltpu.PrefetchScalarGridSpec(
            num_scalar_prefetch=2, grid=(n_slots // tm, nj),
            in_specs=[xs, w13, w13, w2s], out_specs=xs,
            scratch_shapes=[pltpu.VMEM((tm, D_MODEL), F32)]),
        out_shape=jax.ShapeDtypeStruct((n_slots, D_MODEL), F32),
        compiler_params=_cparams(("arbitrary", "arbitrary")),
        name="moe_experts",
    )(tile_expert, tile_valid, x_sorted, w1, w3, w2)


def _combine_ln_kernel(x_ref, y_ref, r_ref, g_ref, b_ref, o_ref):
    r = r_ref[...]
    f = r[:, 2:3] * y_ref[:, :D_MODEL] + r[:, 3:4] * y_ref[:, D_MODEL:]
    o_ref[...] = _layer_norm(ALPHA * x_ref[...] + f, g_ref[...], b_ref[...])


def _combine_ln(x, y2, route, g, b, *, tm):
    m = x.shape[0]
    row = lambda w: pl.BlockSpec((tm, w), lambda i: (i, 0))
    return pl.pallas_call(
        _combine_ln_kernel,
        grid=(m // tm,),
        in_specs=[row(D_MODEL), row(2 * D_MODEL), row(LANES), _const_spec(g.shape), _const_spec(b.shape)],
        out_specs=row(D_MODEL),
        out_shape=jax.ShapeDtypeStruct(x.shape, F32),
        compiler_params=_cparams(("parallel",)),
        name="moe_combine_ln",
    )(x, y2, route, g, b)


def _moe(x_all, router_w, router_b, w1, w3, w2, g, b, *, tm_e, fc):
    t = x_all.shape[0]
    w_pad = jnp.zeros((D_MODEL, LANES), F32).at[:, :N_EXPERTS].set(router_w)
    b_pad = jnp.zeros((1, LANES), F32).at[0, :N_EXPERTS].set(router_b)
    route = _router(x_all, w_pad, b_pad, tm=128)
    top_i = route[:, :2].astype(jnp.int32).reshape(-1)
    onehot = (top_i[:, None] == jnp.arange(N_EXPERTS)[None, :]).astype(jnp.int32)
    csum = jnp.cumsum(onehot, axis=0)
    counts = csum[-1]
    rank = jnp.sum((csum - onehot) * onehot, axis=1)
    padded = ((counts + tm_e - 1) // tm_e) * tm_e
    ends = jnp.cumsum(padded)
    starts = ends - padded
    slot = starts[top_i] + rank
    n_tiles = (2 * t + N_EXPERTS * (tm_e - 1)) // tm_e + 1
    n_slots = n_tiles * tm_e
    src = jnp.zeros((n_slots,), jnp.int32).at[slot].set(jnp.arange(2 * t, dtype=jnp.int32) // 2)
    tile_start = jnp.arange(n_tiles, dtype=jnp.int32) * tm_e
    tile_valid = (tile_start < ends[-1]).astype(jnp.int32)
    tile_expert = jnp.minimum(jnp.searchsorted(ends, tile_start, side="right"), N_EXPERTS - 1).astype(jnp.int32)
    last_e = tile_expert[jnp.maximum(ends[-1] // tm_e - 1, 0)]
    tile_expert = jnp.where(tile_valid == 1, tile_expert, last_e)
    x_sorted = jnp.take(x_all, src, axis=0)
    y_sorted = _experts(x_sorted, tile_expert, tile_valid, w1, w3, w2, tm=tm_e, fc=fc)
    y2 = jnp.take(y_sorted, slot, axis=0).reshape(t, 2 * D_MODEL)
    return _combine_ln(x_all, y2, route, g, b, tm=128)


def _pack_w_in(w_in):
    q, k, v, z, xbc, dt, pu = jnp.split(w_in, np.cumsum([512, 512, 512, 256, 768, 4]).tolist(), axis=1)
    dt_pad = jnp.zeros((D_MODEL, DT_PAD), w_in.dtype).at[:, :SSM_HEADS].set(dt)
    return jnp.concatenate([q, k, v, z, xbc, pu, dt_pad], axis=1).astype(BF16)


def _block_diag(pool_w):
    out = jnp.zeros((POOL_WIDTH, POOL_WIDTH), F32)
    for g in range(len(POOL_WINDOWS)):
        out = out.at[g * POOL_GDIM:(g + 1) * POOL_GDIM, g * POOL_GDIM:(g + 1) * POOL_GDIM].set(pool_w[g])
    return out.astype(BF16)


def _pad_heads(v):
    return jnp.zeros((1, DT_PAD), F32).at[0, :SSM_HEADS].set(v)


def kernel(x_prompt, x_sample, cache_k, cache_v, cache_mem_k, cache_mem_v, state_ssm, state_conv, state_pool, page_table, mem_prompt, w_in, lam_q1, lam_k1, lam_q2, lam_k2, da_norm_w, conv_w, conv_b, dt_bias, a_log, d_skip, ssm_norm_w, pool_w, pool_scale, w_out, ln1_g, ln1_b, wq_mem, wk_mem, wv_mem, wo_mem, ln2_g, ln2_b, ffn_w1, ffn_w3, ffn_w2, moe_router, moe_router_b, moe_w1, moe_w3, moe_w2, ln3_g, ln3_b):
    bp, seq, _ = x_prompt.shape
    bs = x_sample.shape[0]
    n_pages = page_table.shape[1]
    past = n_pages * PAGE_SIZE
    n_mem = mem_prompt.shape[1]
    tp_rows = bp * seq
    TM = 512

    xp = x_prompt.reshape(tp_rows, D_MODEL)
    xs = x_sample.reshape(bs, D_MODEL)
    tab_p = _rope_tables(jnp.arange(seq))
    tab_s = _rope_tables(jnp.full((1,), past))
    cache_k4 = cache_k.reshape(DEPTH, -1, PAGE_SIZE, DA_WIDTH)
    cache_v4 = cache_v.reshape(DEPTH, -1, PAGE_SIZE, DA_WIDTH)
    mem_k_all = cache_mem_k.reshape(DEPTH * bs, n_mem, D_MODEL)
    mem_v_all = cache_mem_v.reshape(DEPTH * bs, n_mem, D_MODEL)
    state_all = state_ssm.reshape(DEPTH * bs, SSM_WIDTH, SSM_STATE)
    mem_flat = mem_prompt.reshape(bp * n_mem, D_MODEL)
    row = lambda v: v.reshape(1, -1)

    outs = {k: [] for k in ("kp", "vp", "ks", "vs", "mk", "mv", "hp", "hs", "cp", "cs", "pp", "ps")}
    for l in range(DEPTH):
        lam_init = 0.8 - 0.6 * math.exp(-0.3 * l)
        w_cat = _pack_w_in(w_in[l])
        lam_vecs = (row(lam_q1[l]), row(lam_k1[l]), row(lam_q2[l]), row(lam_k2[l]))
        nw = row(da_norm_w[l])
        dtb, alog = _pad_heads(dt_bias[l]), _pad_heads(a_log[l])
        dsk = row(jnp.repeat(d_skip[l], SSM_HEADDIM))
        snw = row(ssm_norm_w[l])
        pwbd = _block_diag(pool_w[l])
        psc = row(pool_scale[l])
        wo_b = w_out[l].astype(BF16)
        wo_parts = (wo_b[:DA_WIDTH], wo_b[DA_WIDTH:DA_WIDTH + SSM_WIDTH], wo_b[DA_WIDTH + SSM_WIDTH:])
        g1, b1, g2, b2, g3, b3 = (row(ln1_g[l]), row(ln1_b[l]), row(ln2_g[l]), row(ln2_b[l]),
                                  row(ln3_g[l]), row(ln3_b[l]))

        qb, kp, kb, vp, vb, zp, xbcp, pup, dtp = _proj_in(xp, w_cat, tab_p, prompt=True, tm=TM, seq=seq)
        oa_p = _flash_attn(qb, kb, vb, lam_vecs, nw, batch=bp, seq=seq, tq=512, tk=512, lam_init=lam_init)
        y_p, h_p = _ssd_prompt(xbcp, zp, dtp, conv_w[l], row(conv_b[l]), dtb, alog, dsk, snw, batch=bp, seq=seq)
        yp_p = _pool_prompt(pup, pwbd, psc, batch=bp, seq=seq, tp=512)
        xp = _mm_res_ln((oa_p, y_p, yp_p), wo_parts, xp, g1, b1, tm=TM)

        q_s, k_s, v_s, z_s, xbc_s, pu_s, dt_s = _proj_in(xs, w_cat, tab_s, prompt=False, tm=bs, seq=1)
        oa_s = _paged_attn(q_s, k_s, v_s, cache_k4, cache_v4, page_table, lam_vecs, jnp.tile(nw, (1, DA_HEADS)),
                           layer=l, lam_init=lam_init)
        xact, dt_a, dec, yp_s = _sample_prep(xbc_s, state_conv[l], conv_w[l], row(conv_b[l]), dt_s, dtb, alog,
                                             pu_s, state_pool[l], pwbd, psc, pos0=past)
        y_s, h_s = _ssd_step(state_all, l, xact, z_s, dt_a, dec, dsk, snw)
        xs = _mm_res_ln((oa_s, y_s, yp_s), wo_parts, xs, g1, b1, tm=bs)

        wq_b, wk_b, wv_b, wom_b = (wq_mem[l].astype(BF16), wk_mem[l].astype(BF16), wv_mem[l].astype(BF16),
                                   wo_mem[l].astype(BF16))
        mk, mkb = _matmul(mem_flat, wk_b, tm=bp * n_mem // 2, dtypes=(F32, BF16))
        mv, mvb = _matmul(mem_flat, wv_b, tm=bp * n_mem // 2, dtypes=(F32, BF16))
        xp = _mem_attn_prompt(xp, mkb, mvb, wq_b, wom_b, g2, b2, batch=bp, seq=seq, tm=TM)
        (qm_s,) = _matmul(xs, wq_b, tm=bs, scale=MEM_HD ** -0.5)
        om_s = _mem_attn_sample(qm_s, mem_k_all, mem_v_all, layer=l, group=4)
        xs = _mm_res_ln((om_s,), (wom_b,), xs, g2, b2, tm=bs)

        j = l // 2
        if l % 2 == 0:
            f = ffn_w1.shape[2]
            fc = 256
            w1c = ffn_w1[j].astype(BF16).reshape(D_MODEL, f // fc, fc).transpose(1, 0, 2)
            w3c = ffn_w3[j].astype(BF16).reshape(D_MODEL, f // fc, fc).transpose(1, 0, 2)
            w2c = ffn_w2[j].astype(BF16).reshape(f // fc, fc, D_MODEL)
            xp = _ffn(xp, w1c, w3c, w2c, g3, b3, tm=TM)
            xs = _ffn(xs, w1c, w3c, w2c, g3, b3, tm=bs)
        else:
            x_all = jnp.concatenate([xp, xs], axis=0)
            x_all = _moe(x_all, moe_router[j], moe_router_b[j], moe_w1[j].astype(BF16), moe_w3[j].astype(BF16),
                         moe_w2[j].astype(BF16), g3, b3, tm_e=512, fc=512)
            xp, xs = x_all[:tp_rows], x_all[tp_rows:]

        outs["kp"].append(kp.reshape(bp, seq // PAGE_SIZE, PAGE_SIZE, DA_HEADS, 2, DA_DK))
        outs["vp"].append(vp.reshape(bp, seq // PAGE_SIZE, PAGE_SIZE, DA_HEADS, DA_DV))
        outs["ks"].append(k_s.reshape(bs, 1, DA_HEADS, 2, DA_DK))
        outs["vs"].append(v_s.reshape(bs, 1, DA_HEADS, DA_DV))
        outs["mk"].append(mk.reshape(bp, n_mem, MEM_HEADS, MEM_HD))
        outs["mv"].append(mv.reshape(bp, n_mem, MEM_HEADS, MEM_HD))
        outs["hp"].append(h_p)
        outs["hs"].append(h_s)
        outs["cp"].append(xbcp.reshape(bp, seq, CONV_CH)[:, seq - (SSM_CONV - 1):])
        outs["cs"].append(jnp.concatenate([state_conv[l][:, 1:], xbc_s[:, None, :]], axis=1))
        outs["pp"].append(pup.reshape(bp, seq, POOL_WIDTH)[:, seq - POOL_BUF:])
        outs["ps"].append(jnp.concatenate([state_pool[l][:, 1:], pu_s[:, None, :]], axis=1))

    st = lambda k: jnp.stack(outs[k])
    return (xp.reshape(bp, seq, D_MODEL), xs.reshape(bs, 1, D_MODEL), st("kp"), st("vp"), st("ks"), st("vs"),
            st("mk"), st("mv"), st("hp"), st("hs"), st("cp"), st("cs"), st("pp"), st("ps"))
```

```python
import functools
import math

import jax
import jax.numpy as jnp
import numpy as np
from jax import lax
from jax.experimental import pallas as pl
from jax.experimental.pallas import tpu as pltpu

F32 = jnp.float32
BF16 = jnp.bfloat16

D_MODEL = 1024
DEPTH = 2
PAGE_SIZE = 128
DA_HEADS = 4
DA_WIDTH = 512
DA_DV = 128
DA_DK = 64
ROT_DIM = 16
ROPE_THETA = 500000.0
SSM_WIDTH = 256
SSM_HEADS = 4
SSM_HEADDIM = 64
SSM_STATE = 128
SSM_CONV = 4
SSM_CHUNK = 128
CONV_CH = 768
POOL_WIDTH = 256
POOL_WINDOWS = (2, 4, 8, 16)
POOL_GDIM = 64
POOL_BUF = 15
MEM_HEADS = 4
MEM_HD = 256
N_EXPERTS = 8
ALPHA = (2.0 * DEPTH) ** 0.25
LN_EPS = 1e-5
RMS_EPS = 1e-6

LANES = 128
DT_PAD = LANES
_C_Q, _C_K, _C_V, _C_Z, _C_X, _C_P, _C_DT, _C_END = 0, 512, 1024, 1536, 1792, 2560, 2816, 2944

NEG_INF = float("-inf")
Q_SCALE = DA_DK ** -0.5 * math.log2(math.e)


def _cparams(sem, vmem_mb=48):
    return pltpu.CompilerParams(dimension_semantics=sem, vmem_limit_bytes=vmem_mb * 1024 * 1024)


def _dot(a, b):
    return jnp.dot(a.astype(BF16), b.astype(BF16), preferred_element_type=F32)


def _dot_nt(a, b):
    return lax.dot_general(a.astype(BF16), b.astype(BF16), (((1,), (1,)), ((), ())),
                           preferred_element_type=F32)


def _layer_norm(x, g, b):
    mu = jnp.mean(x, axis=-1, keepdims=True)
    xc = x - mu
    var = jnp.mean(xc * xc, axis=-1, keepdims=True)
    return xc * lax.rsqrt(var + LN_EPS) * g + b


def _silu(x):
    return x * (1.0 / (1.0 + jnp.exp(-x)))


def _softplus(x):
    return jnp.maximum(x, 0.0) + jnp.log(1.0 + jnp.exp(-jnp.abs(x)))


def _const_spec(shape):
    nd = len(shape)
    return pl.BlockSpec(shape, lambda *_: (0,) * nd)


def _rope_tables(pos):
    half = ROT_DIM // 2
    inv = ROPE_THETA ** (-jnp.arange(half, dtype=F32) * 2.0 / ROT_DIM)
    ang = pos.astype(F32)[:, None] * inv[None, :]
    cos, sin = jnp.cos(ang), jnp.sin(ang)
    n = pos.shape[0]
    c64 = jnp.concatenate([cos, cos, jnp.ones((n, DA_DK - ROT_DIM), F32)], axis=1)
    s1 = jnp.concatenate([-sin, jnp.zeros((n, DA_DK - half), F32)], axis=1)
    s2 = jnp.concatenate([jnp.zeros((n, half), F32), sin, jnp.zeros((n, DA_DK - ROT_DIM), F32)], axis=1)
    two = lambda t: jnp.concatenate([t, t], axis=1)
    return two(c64), two(s1), two(s2)


def _proj_in_kernel(x_ref, w_ref, c_ref, s1_ref, s2_ref, *out_refs, prompt):
    xb = x_ref[...].astype(BF16)
    cc, s1, s2 = c_ref[...], s1_ref[...], s2_ref[...]

    def rope(t):
        outs = []
        for h in range(DA_HEADS):
            th = t[:, h * LANES:(h + 1) * LANES]
            outs.append(th * cc + pltpu.roll(th, LANES - ROT_DIM // 2, 1) * s1
                        + pltpu.roll(th, ROT_DIM // 2, 1) * s2)
        return jnp.concatenate(outs, axis=1)

    def seg(a, b):
        return lax.dot_general(xb, w_ref[a:b, :], (((1,), (1,)), ((), ())), preferred_element_type=F32)

    q = rope(seg(_C_Q, _C_K)) * Q_SCALE
    k = rope(seg(_C_K, _C_V))
    v = seg(_C_V, _C_Z)
    if prompt:
        q_ref, k_ref, kb_ref, v_ref, z_ref, xbc_ref, pu_ref, dt_ref = out_refs
        kb_ref[...] = k.astype(BF16)
    else:
        q_ref, k_ref, v_ref, z_ref, xbc_ref, pu_ref, dt_ref = out_refs
    q_ref[...] = q
    k_ref[...] = k
    v_ref[...] = v
    z_ref[...] = seg(_C_Z, _C_X)
    xbc_ref[...] = seg(_C_X, _C_P)
    pu_ref[...] = seg(_C_P, _C_DT)
    dt_ref[...] = seg(_C_DT, _C_END)


def _proj_in(x, w_cat, tables, *, prompt, tm, seq):
    m = x.shape[0]
    grid = (m // tm,)
    row = lambda w: pl.BlockSpec((tm, w), lambda i: (i, 0))
    if prompt:
        npos = seq // tm
        tab = pl.BlockSpec((tm, LANES), lambda i: (i % npos, 0))
        widths = [(DA_WIDTH, F32), (DA_WIDTH, F32), (DA_WIDTH, BF16), (DA_WIDTH, F32)]
    else:
        tab = _const_spec((1, LANES))
        widths = [(DA_WIDTH, F32), (DA_WIDTH, F32), (DA_WIDTH, F32)]
    widths += [(SSM_WIDTH, F32), (CONV_CH, F32), (POOL_WIDTH, F32), (DT_PAD, F32)]
    return pl.pallas_call(
        functools.partial(_proj_in_kernel, prompt=prompt),
        grid=grid,
        in_specs=[row(D_MODEL), _const_spec(w_cat.shape), tab, tab, tab],
        out_specs=[row(w) for w, _ in widths],
        out_shape=[jax.ShapeDtypeStruct((m, w), dt) for w, dt in widths],
        compiler_params=_cparams(("parallel",)),
        name="proj_in",
    )(x, w_cat, *tables)


def _lam_value(lq1, lk1, lq2, lk2, lam_init):
    return (jnp.exp(jnp.sum(lq1 * lk1, axis=1, keepdims=True))
            - jnp.exp(jnp.sum(lq2 * lk2, axis=1, keepdims=True)) + lam_init)


def _flash_kernel(q_ref, k_ref, v_ref, lq1_ref, lk1_ref, lq2_ref, lk2_ref, nw_ref, o_ref,
                  vt_sc, s_sc, p_sc, m_sc, acc_sc, *, tq, tk, lam_init):
    i = pl.program_id(2)
    n_kt, aug, _ = vt_sc.shape

    @pl.when(i == 0)
    def _():
        ones = jnp.ones((aug - DA_DV, tk), BF16)
        for j in range(n_kt):
            vt = v_ref[j * tk:(j + 1) * tk, :].T.astype(BF16)
            vt_sc[j] = jnp.concatenate([vt, ones], axis=0)

    qt = q_ref[...].T
    drow = lax.broadcasted_iota(jnp.int32, qt.shape, 0)
    qts = (jnp.where(drow < DA_DK, qt, 0.0).astype(BF16), jnp.where(drow >= DA_DK, qt, 0.0).astype(BF16))

    def stage(j, cur, mask=None, last=False):
        nxt = 1 - cur
        if not last:
            kt = k_ref[pl.ds(pl.multiple_of((j + 1) * tk, tk), tk), :]
            for c in range(2):
                s_sc[nxt, c] = jnp.dot(kt, qts[c], preferred_element_type=F32)
        vta = vt_sc[jnp.maximum(j - 1, 0)]
        for c in range(2):
            pv = jnp.dot(vta, p_sc[cur, c], preferred_element_type=F32)
            s = s_sc[cur, c]
            if mask is not None:
                s = jnp.where(mask, s, NEG_INF)
            m = m_sc[c]
            m_new = jnp.maximum(m, jnp.max(s, axis=0, keepdims=True))
            p_sc[nxt, c] = jnp.exp2(s - m_new).astype(BF16)
            acc_sc[c] = jnp.exp2(m - m_new) * (acc_sc[c] + pv)
            m_sc[c] = m_new

    kt0 = k_ref[0:tk, :]
    for c in range(2):
        s_sc[0, c] = jnp.dot(kt0, qts[c], preferred_element_type=F32)
        p_sc[0, c] = jnp.zeros((tk, tq), BF16)
        m_sc[c] = jnp.full((1, tq), NEG_INF, F32)
        acc_sc[c] = jnp.zeros((aug, tq), F32)

    def pair(t, carry):
        stage(2 * t, 0)
        stage(2 * t + 1, 1)
        return carry

    lax.fori_loop(0, i // 2, pair, 0)
    odd = i % 2 == 1

    @pl.when(odd)
    def _():
        stage(i - 1, 0)

    krow = lax.broadcasted_iota(jnp.int32, (tk, tq), 0)
    qcol = lax.broadcasted_iota(jnp.int32, (tk, tq), 1)

    def finish(cur):
        stage(i, cur, mask=krow <= qcol, last=True)
        vta = vt_sc[i]
        a0, a1 = (acc_sc[c] + jnp.dot(vta, p_sc[1 - cur, c], preferred_element_type=F32) for c in range(2))
        lam = _lam_value(lq1_ref[...], lk1_ref[...], lq2_ref[...], lk2_ref[...], lam_init)
        o = a0[:DA_DV] * (1.0 / a0[DA_DV:DA_DV + 1]) - lam * (a1[:DA_DV] * (1.0 / a1[DA_DV:DA_DV + 1]))
        ms = jnp.mean(o * o, axis=0, keepdims=True)
        o = o * lax.rsqrt(ms + RMS_EPS) * nw_ref[...] * (1.0 - lam_init)
        o_ref[...] = o.T

    pl.when(odd)(lambda: finish(1))
    pl.when(jnp.logical_not(odd))(lambda: finish(0))


def _flash_attn(q, kb, v, lam_vecs, norm_w_col, *, batch, seq, tq, tk, lam_init):
    assert tq == tk
    q3 = q.reshape(batch, seq, DA_WIDTH)
    k3 = kb.reshape(batch, seq, DA_WIDTH)
    v3 = v.reshape(batch, seq, DA_WIDTH)
    qspec = pl.BlockSpec((None, tq, DA_DV), lambda b, h, i: (b, i, h))
    kvspec = pl.BlockSpec((None, seq, DA_DV), lambda b, h, i: (b, 0, h))
    vec = _const_spec((1, DA_DK))
    aug = DA_DV + 16
    out = pl.pallas_call(
        functools.partial(_flash_kernel, tq=tq, tk=tk, lam_init=lam_init),
        grid=(batch, DA_HEADS, seq // tq),
        in_specs=[qspec, kvspec, kvspec, vec, vec, vec, vec, _const_spec((DA_DV, 1))],
        out_specs=qspec,
        out_shape=jax.ShapeDtypeStruct((batch, seq, DA_WIDTH), F32),
        scratch_shapes=[pltpu.VMEM((seq // tk, aug, tk), BF16), pltpu.VMEM((2, 2, tk, tq), F32),
                        pltpu.VMEM((2, 2, tk, tq), BF16), pltpu.VMEM((2, 1, tq), F32),
                        pltpu.VMEM((2, aug, tq), F32)],
        compiler_params=_cparams(("parallel", "parallel", "arbitrary")),
        name="flash_diff_attn",
    )(q3, k3, v3, *lam_vecs, norm_w_col)
    return out.reshape(batch * seq, DA_WIDTH)


def _paged_kernel(pt_ref, q_ref, ks_ref, vs_ref, lq1_ref, lk1_ref, lq2_ref, lk2_ref, nw_ref, *rest,
                  n_pages, lam_init):
    k_refs = rest[:n_pages]
    v_refs = rest[n_pages:2 * n_pages]
    o_ref = rest[2 * n_pages]
    q = q_ref[...]
    nhc = 2 * DA_HEADS
    jj = lax.broadcasted_iota(jnp.int32, (2 * nhc, DA_WIDTH), 0)
    ll = lax.broadcasted_iota(jnp.int32, (2 * nhc, DA_WIDTH), 1)
    qblk = jnp.where((jj < nhc) & ((ll >> 6) == (jj & (DA_HEADS - 1)) * 2 + (jj >> 2)), q, 0.0)
    qb = qblk.astype(BF16)

    s_all = jnp.concatenate(
        [jnp.dot(qb, k_refs[j][...].astype(BF16), preferred_element_type=F32)[:nhc] for j in range(n_pages)],
        axis=1)
    s_self = jnp.sum(qblk[:nhc] * ks_ref[...], axis=1, keepdims=True)
    m = jnp.maximum(jnp.max(s_all, axis=1, keepdims=True), s_self)
    p = jnp.exp2(s_all - m)
    p_self = jnp.exp2(s_self - m)
    inv_l = 1.0 / (jnp.sum(p, axis=1, keepdims=True) + p_self)
    lam = _lam_value(lq1_ref[...], lk1_ref[...], lq2_ref[...], lk2_ref[...], lam_init)
    pn = p * inv_l
    pn_self = p_self * inv_l
    w = pn - lam * pltpu.roll(pn, DA_HEADS, 0)
    w_self = pn_self - lam * pltpu.roll(pn_self, DA_HEADS, 0)
    w16 = jnp.concatenate([w, jnp.zeros_like(w)], axis=0).astype(BF16)
    accs = [jnp.zeros((2 * nhc, DA_DV), F32) for _ in range(DA_HEADS)]
    for j in range(n_pages):
        wj = w16[:, j * PAGE_SIZE:(j + 1) * PAGE_SIZE]
        for h in range(DA_HEADS):
            vh = v_refs[j][pl.ds(h, PAGE_SIZE, stride=DA_HEADS), :]
            accs[h] = accs[h] + jnp.dot(wj, vh.astype(BF16), preferred_element_type=F32)
    outs = []
    for h in range(DA_HEADS):
        o = accs[h][h:h + 1, :] + w_self[h:h + 1, :] * vs_ref[:, h * DA_DV:(h + 1) * DA_DV]
        ms = jnp.mean(o * o, axis=1, keepdims=True)
        outs.append(o * lax.rsqrt(ms + RMS_EPS) * nw_ref[...] * (1.0 - lam_init))
    o_ref[...] = jnp.concatenate(outs, axis=1)


def _paged_attn(q, k_self, v_self, cache_k4, cache_v4, page_table, lam_vecs, norm_w4, *, layer, lam_init):
    bs, n_pages = page_table.shape
    pt = page_table.reshape(-1)
    one = pl.BlockSpec((None, 1, DA_WIDTH), lambda b, pt: (b, 0, 0))
    vec = pl.BlockSpec((1, DA_DK), lambda b, pt: (0, 0))

    def page_spec(j):
        return pl.BlockSpec((None, None, DA_WIDTH, PAGE_SIZE), lambda b, pt: (layer, pt[b * n_pages + j], 0, 0))

    specs = [one, one, one, vec, vec, vec, vec, pl.BlockSpec((1, DA_DV), lambda b, pt: (0, 0))]
    specs += [page_spec(j) for j in range(n_pages)] * 2
    out = pl.pallas_call(
        functools.partial(_paged_kernel, n_pages=n_pages, lam_init=lam_init),
        grid_spec=pltpu.PrefetchScalarGridSpec(
            num_scalar_prefetch=1, grid=(bs,), in_specs=specs, out_specs=one),
        out_shape=jax.ShapeDtypeStruct((bs, 1, DA_WIDTH), F32),
        compiler_params=_cparams(("arbitrary",)),
        name="paged_diff_attn",
    )(pt, q.reshape(bs, 1, DA_WIDTH), k_self.reshape(bs, 1, DA_WIDTH), v_self.reshape(bs, 1, DA_WIDTH),
      *lam_vecs, norm_w4, *([cache_k4] * n_pages), *([cache_v4] * n_pages))
    return out.reshape(bs, DA_WIDTH)


def _head_expand(cols, width, per):
    rows = cols.shape[0]
    lane = lax.broadcasted_iota(jnp.int32, (rows, width), 1)
    out = jnp.broadcast_to(cols[:, 0:1], (rows, width))
    for h in range(1, width // per):
        out = jnp.where(lane >= h * per, cols[:, h:h + 1], out)
    return out


def _ssd_kernel(xbc_ref, z_ref, dt_ref, cw_ref, cb_ref, dtb_ref, alog_ref, dsk_ref, nw_ref,
                y_ref, st_ref, tail_sc, state_sc):
    c = pl.program_id(1)
    q = SSM_CHUNK

    @pl.when(c == 0)
    def _():
        tail_sc[...] = jnp.zeros_like(tail_sc)
        state_sc[...] = jnp.zeros_like(state_sc)

    u = xbc_ref[...]
    full = jnp.concatenate([tail_sc[...], u], axis=0)
    tail_sc[...] = u[q - 8:, :]
    conv = full * cw_ref[SSM_CONV - 1:SSM_CONV, :]
    for j in range(1, SSM_CONV):
        conv = conv + pltpu.roll(full, j, 0) * cw_ref[SSM_CONV - 1 - j:SSM_CONV - j, :]
    xbc = _silu(conv[8:, :] + cb_ref[...])
    xs = xbc[:, :SSM_WIDTH]
    bm = (xbc[:, SSM_WIDTH:SSM_WIDTH + SSM_STATE], xbc[:, SSM_WIDTH + SSM_STATE:SSM_WIDTH + 2 * SSM_STATE])
    cm = (xbc[:, SSM_WIDTH + 2 * SSM_STATE:SSM_WIDTH + 3 * SSM_STATE], xbc[:, SSM_WIDTH + 3 * SSM_STATE:])

    dt = _softplus(dt_ref[...] + dtb_ref[...])
    dta = dt * (-jnp.exp(alog_ref[...]))
    ti = lax.broadcasted_iota(jnp.int32, (q, q), 0)
    si = lax.broadcasted_iota(jnp.int32, (q, q), 1)
    causal = si <= ti
    tri = jnp.where(causal, 1.0, 0.0)
    a_cs = jnp.dot(tri, dta, precision=lax.Precision.HIGHEST, preferred_element_type=F32)
    a_cs_t = a_cs.T
    last = a_cs[q - 1:q, :]

    dt_x = _head_expand(dt, SSM_WIDTH, SSM_HEADDIM)
    xdt = xs * dt_x
    xdt_end = xdt * _head_expand(jnp.exp(last - a_cs), SSM_WIDTH, SSM_HEADDIM)
    lane = lax.broadcasted_iota(jnp.int32, (q, SSM_WIDTH), 1)

    cb = [_dot_nt(cm[g], bm[g]) for g in range(2)]
    y_diag = jnp.zeros((q, SSM_WIDTH), F32)
    for h in range(SSM_HEADS):
        seg = a_cs[:, h:h + 1] - a_cs_t[h:h + 1, :]
        decay = jnp.exp(jnp.where(causal, seg, NEG_INF))
        xh = jnp.where((lane >= h * SSM_HEADDIM) & (lane < (h + 1) * SSM_HEADDIM), xdt, 0.0)
        y_diag = y_diag + _dot(cb[h // 2] * decay, xh)

    state = state_sc[...]
    sb = state.astype(BF16)
    y_off = jnp.where(lane < 2 * SSM_HEADDIM, _dot_nt(cm[0], sb), _dot_nt(cm[1], sb))
    y_off = y_off * _head_expand(jnp.exp(a_cs), SSM_WIDTH, SSM_HEADDIM)

    xt = xdt_end.T
    rowi = lax.broadcasted_iota(jnp.int32, (SSM_WIDTH, SSM_STATE), 0)
    new = jnp.where(rowi < 2 * SSM_HEADDIM, _dot(xt, bm[0]), _dot(xt, bm[1]))
    e_last = jnp.exp(last)
    dec_rows = jnp.broadcast_to(e_last[:, 0:1], (SSM_WIDTH, SSM_STATE))
    for h in range(1, SSM_HEADS):
        dec_rows = jnp.where(rowi >= h * SSM_HEADDIM, e_last[:, h:h + 1], dec_rows)
    state = state * dec_rows + new
    state_sc[...] = state

    y = y_diag + y_off + dsk_ref[...] * xs
    y = y * _silu(z_ref[...])
    ms = jnp.mean(y * y, axis=1, keepdims=True)
    y_ref[...] = y * lax.rsqrt(ms + RMS_EPS) * nw_ref[...]

    @pl.when(c == pl.num_programs(1) - 1)
    def _():
        st_ref[...] = state


def _ssd_prompt(xbc, z, dt_raw, conv_w, conv_b, dt_bias, a_log, d_skip_x, norm_w, *, batch, seq):
    nc = seq // SSM_CHUNK
    blk = lambda w: pl.BlockSpec((None, SSM_CHUNK, w), lambda b, c: (b, c, 0))
    y, st = pl.pallas_call(
        _ssd_kernel,
        grid=(batch, nc),
        in_specs=[blk(CONV_CH), blk(SSM_WIDTH), blk(DT_PAD),
                  _const_spec((SSM_CONV, CONV_CH)), _const_spec((1, CONV_CH)),
                  _const_spec((1, DT_PAD)), _const_spec((1, DT_PAD)),
                  _const_spec((1, SSM_WIDTH)), _const_spec((1, SSM_WIDTH))],
        out_specs=[blk(SSM_WIDTH), pl.BlockSpec((None, SSM_WIDTH, SSM_STATE), lambda b, c: (b, 0, 0))],
        out_shape=[jax.ShapeDtypeStruct((batch, seq, SSM_WIDTH), F32),
                   jax.ShapeDtypeStruct((batch, SSM_WIDTH, SSM_STATE), F32)],
        scratch_shapes=[pltpu.VMEM((8, CONV_CH), F32), pltpu.VMEM((SSM_WIDTH, SSM_STATE), F32)],
        compiler_params=_cparams(("parallel", "arbitrary")),
        name="ssd_prompt",
    )(xbc.reshape(batch, seq, CONV_CH), z.reshape(batch, seq, SSM_WIDTH), dt_raw.reshape(batch, seq, DT_PAD),
      conv_w, conv_b, dt_bias, a_log, d_skip_x, norm_w)
    return y.reshape(batch * seq, SSM_WIDTH), st.reshape(batch, SSM_HEADS, SSM_HEADDIM, SSM_STATE)


def _pool_select(sums, inv_cnt, lane):
    out = sums[POOL_WINDOWS[0]] * inv_cnt[POOL_WINDOWS[0]]
    for g in range(1, len(POOL_WINDOWS)):
        w = POOL_WINDOWS[g]
        out = jnp.where(lane >= g * POOL_GDIM, sums[w] * inv_cnt[w], out)
    return out


def _pool_kernel(u_ref, w_ref, sc_ref, y_ref, tail_sc, *, tp):
    c = pl.program_id(1)
    halo = 16

    @pl.when(c == 0)
    def _():
        tail_sc[...] = jnp.zeros_like(tail_sc)

    u = u_ref[...]
    full = jnp.concatenate([tail_sc[...], u], axis=0)
    tail_sc[...] = u[tp - halo:, :]
    sums = {}
    run = full
    w = 1
    while w < max(POOL_WINDOWS):
        run = run + pltpu.roll(run, w, 0)
        w *= 2
        sums[w] = run[halo:, :]
    pos = c * tp + lax.broadcasted_iota(jnp.int32, (tp, 1), 0)
    inv_cnt = {w: 1.0 / jnp.minimum(w, pos + 1).astype(F32) for w in POOL_WINDOWS}
    lane = lax.broadcasted_iota(jnp.int32, (tp, POOL_WIDTH), 1)
    d = _pool_select(sums, inv_cnt, lane) - u
    y_ref[...] = _dot(d, w_ref[...]) * sc_ref[...]


def _pool_prompt(pu, pool_wbd, pool_scale, *, batch, seq, tp):
    blk = pl.BlockSpec((None, tp, POOL_WIDTH), lambda b, c: (b, c, 0))
    y = pl.pallas_call(
        functools.partial(_pool_kernel, tp=tp),
        grid=(batch, seq // tp),
        in_specs=[blk, _const_spec((POOL_WIDTH, POOL_WIDTH)), _const_spec((1, POOL_WIDTH))],
        out_specs=blk,
        out_shape=jax.ShapeDtypeStruct((batch, seq, POOL_WIDTH), F32),
        scratch_shapes=[pltpu.VMEM((16, POOL_WIDTH), F32)],
        compiler_params=_cparams(("parallel", "arbitrary")),
        name="pool_prompt",
    )(pu.reshape(batch, seq, POOL_WIDTH), pool_wbd, pool_scale)
    return y.reshape(batch * seq, POOL_WIDTH)


def _sample_prep_kernel(xbc_ref, cbuf_ref, cw_ref, cb_ref, dt_ref, dtb_ref, alog_ref, pu_ref, pbuf_ref,
                        pw_ref, psc_ref, xact_ref, dto_ref, dec_ref, yp_ref, *, pos0):
    u = xbc_ref[...]
    conv = u * cw_ref[SSM_CONV - 1:SSM_CONV, :] + cb_ref[...]
    for j in range(SSM_CONV - 1):
        conv = conv + cbuf_ref[j] * cw_ref[j:j + 1, :]
    xact_ref[...] = _silu(conv)
    dt = _softplus(dt_ref[...] + dtb_ref[...])
    dto_ref[...] = dt
    dec_ref[...] = jnp.exp(dt * (-jnp.exp(alog_ref[...])))
    pu = pu_ref[...]
    run = pu
    sums = {}
    for j in range(1, max(POOL_WINDOWS)):
        run = run + pbuf_ref[POOL_BUF - j]
        if j + 1 in POOL_WINDOWS:
            sums[j + 1] = run
    inv_cnt = {w: 1.0 / float(min(w, pos0 + 1)) for w in POOL_WINDOWS}
    lane = lax.broadcasted_iota(jnp.int32, pu.shape, 1)
    d = _pool_select(sums, inv_cnt, lane) - pu
    yp_ref[...] = _dot(d, pw_ref[...]) * psc_ref[...]


def _sample_prep(xbc, conv_buf, conv_w, conv_b, dt_raw, dt_bias, a_log, pu, pool_buf, pool_wbd, pool_scale, *, pos0):
    bs = xbc.shape[0]
    args = (xbc, conv_buf, conv_w, conv_b, dt_raw, dt_bias, a_log, pu, pool_buf, pool_wbd, pool_scale)
    shapes = [(bs, CONV_CH), (bs, DT_PAD), (bs, DT_PAD), (bs, POOL_WIDTH)]
    return pl.pallas_call(
        functools.partial(_sample_prep_kernel, pos0=pos0),
        grid=(1,),
        in_specs=[_const_spec(a.shape) for a in args],
        out_specs=[_const_spec(s) for s in shapes],
        out_shape=[jax.ShapeDtypeStruct(s, F32) for s in shapes],
        compiler_params=_cparams(("arbitrary",)),
        name="sample_prep",
    )(*args)


def _ssd_step_kernel(s_ref, x_ref, z_ref, dt_ref, dec_ref, b_ref, c_ref, dsk_ref, nw_ref, y_ref, so_ref):
    rowi = lax.broadcasted_iota(jnp.int32, (SSM_WIDTH, SSM_STATE), 0)
    brow, crow = b_ref[...], c_ref[...]
    bsel = jnp.where(rowi < 2 * SSM_HEADDIM, brow[:, :SSM_STATE], brow[:, SSM_STATE:])
    csel = jnp.where(rowi < 2 * SSM_HEADDIM, crow[:, :SSM_STATE], crow[:, SSM_STATE:])
    x = x_ref[...]
    sn = s_ref[...] * dec_ref[...] + (x * dt_ref[...]) * bsel
    so_ref[...] = sn
    y = jnp.sum(sn * csel, axis=1, keepdims=True) + dsk_ref[...] * x
    y = y * _silu(z_ref[...])
    ms = jnp.sum(y * y, axis=0, keepdims=True) * (1.0 / SSM_WIDTH)
    y_ref[...] = y * lax.rsqrt(ms + RMS_EPS) * nw_ref[...]


def _ssd_step(state_all, layer, xact, z, dt, dec, d_skip_x, norm_w):
    bs = xact.shape[0]
    rep = lambda t: jnp.repeat(t[:, :SSM_HEADS], SSM_HEADDIM, axis=1).reshape(bs, SSM_WIDTH, 1)
    colb = pl.BlockSpec((None, SSM_WIDTH, 1), lambda b: (b, 0, 0))
    rowb = pl.BlockSpec((None, 1, 2 * SSM_STATE), lambda b: (b, 0, 0))
    st_in = pl.BlockSpec((None, SSM_WIDTH, SSM_STATE), lambda b: (layer * bs + b, 0, 0))
    st_out = pl.BlockSpec((None, SSM_WIDTH, SSM_STATE), lambda b: (b, 0, 0))
    y, st = pl.pallas_call(
        _ssd_step_kernel,
        grid=(bs,),
        in_specs=[st_in, colb, colb, colb, colb, rowb, rowb,
                  _const_spec((SSM_WIDTH, 1)), _const_spec((SSM_WIDTH, 1))],
        out_specs=[colb, st_out],
        out_shape=[jax.ShapeDtypeStruct((bs, SSM_WIDTH, 1), F32),
                   jax.ShapeDtypeStruct((bs, SSM_WIDTH, SSM_STATE), F32)],
        compiler_params=_cparams(("parallel",)),
        name="ssd_step",
    )(state_all, xact[:, :SSM_WIDTH].reshape(bs, SSM_WIDTH, 1), z.reshape(bs, SSM_WIDTH, 1), rep(dt), rep(dec),
      xact[:, SSM_WIDTH:SSM_WIDTH + 2 * SSM_STATE].reshape(bs, 1, 2 * SSM_STATE),
      xact[:, SSM_WIDTH + 2 * SSM_STATE:].reshape(bs, 1, 2 * SSM_STATE),
      d_skip_x.reshape(SSM_WIDTH, 1), norm_w.reshape(SSM_WIDTH, 1))
    return y.reshape(bs, SSM_WIDTH), st.reshape(bs, SSM_HEADS, SSM_HEADDIM, SSM_STATE)


def _mm_kernel(x_ref, w_ref, *out_refs, scale):
    y = _dot(x_ref[...], w_ref[...])
    if scale != 1.0:
        y = y * scale
    for r in out_refs:
        r[...] = y.astype(r.dtype)


def _matmul(x, w, *, tm, scale=1.0, dtypes=(F32,)):
    m, k = x.shape
    n = w.shape[1]
    res = pl.pallas_call(
        functools.partial(_mm_kernel, scale=scale),
        grid=(m // tm,),
        in_specs=[pl.BlockSpec((tm, k), lambda i: (i, 0)), _const_spec(w.shape)],
        out_specs=[pl.BlockSpec((tm, n), lambda i: (i, 0)) for _ in dtypes],
        out_shape=[jax.ShapeDtypeStruct((m, n), d) for d in dtypes],
        compiler_params=_cparams(("parallel",)),
        name="matmul",
    )(x, w)
    return res


def _mm_res_ln_kernel(*refs, n_in):
    a_refs = refs[:n_in]
    w_refs = refs[n_in:2 * n_in]
    x_ref, g_ref, b_ref, o_ref = refs[2 * n_in:]
    h = _dot(a_refs[0][...], w_refs[0][...])
    for a, w in zip(a_refs[1:], w_refs[1:]):
        h = h + _dot(a[...], w[...])
    o_ref[...] = _layer_norm(ALPHA * x_ref[...] + h, g_ref[...], b_ref[...])


def _mm_res_ln(acts, weights, x, g, b, *, tm):
    m = x.shape[0]
    row = lambda a: pl.BlockSpec((tm, a.shape[1]), lambda i: (i, 0))
    return pl.pallas_call(
        functools.partial(_mm_res_ln_kernel, n_in=len(acts)),
        grid=(m // tm,),
        in_specs=[row(a) for a in acts] + [_const_spec(w.shape) for w in weights]
        + [row(x), _const_spec(g.shape), _const_spec(b.shape)],
        out_specs=row(x),
        out_shape=jax.ShapeDtypeStruct(x.shape, F32),
        compiler_params=_cparams(("parallel",)),
        name="mm_res_ln",
    )(*acts, *weights, x, g, b)


def _mem_prompt_kernel(x_ref, mk_ref, mv_ref, wq_ref, wo_ref, g_ref, b_ref, o_ref):
    x = x_ref[...]
    q = (_dot(x, wq_ref[...]) * (MEM_HD ** -0.5)).astype(BF16)
    outs = []
    for h in range(MEM_HEADS):
        sl = slice(h * MEM_HD, (h + 1) * MEM_HD)
        s = lax.dot_general(q[:, sl], mk_ref[:, sl], (((1,), (1,)), ((), ())), preferred_element_type=F32)
        p = jnp.exp(s - jnp.max(s, axis=1, keepdims=True))
        p = p * (1.0 / jnp.sum(p, axis=1, keepdims=True))
        outs.append(jnp.dot(p.astype(BF16), mv_ref[:, sl], preferred_element_type=F32))
    o = jnp.concatenate(outs, axis=1)
    o_ref[...] = _layer_norm(ALPHA * x + _dot(o, wo_ref[...]), g_ref[...], b_ref[...])


def _mem_attn_prompt(x, mkb, mvb, wq, wo, g, b, *, batch, seq, tm):
    n_mem = mkb.shape[0] // batch
    xblk = pl.BlockSpec((None, tm, D_MODEL), lambda bb, i: (bb, i, 0))
    mblk = pl.BlockSpec((None, n_mem, D_MODEL), lambda bb, i: (bb, 0, 0))
    out = pl.pallas_call(
        _mem_prompt_kernel,
        grid=(batch, seq // tm),
        in_specs=[xblk, mblk, mblk, _const_spec(wq.shape), _const_spec(wo.shape),
                  _const_spec(g.shape), _const_spec(b.shape)],
        out_specs=xblk,
        out_shape=jax.ShapeDtypeStruct((batch, seq, D_MODEL), F32),
        compiler_params=_cparams(("parallel", "parallel")),
        name="mem_attn_prompt",
    )(x.reshape(batch, seq, D_MODEL), mkb.reshape(batch, n_mem, D_MODEL), mvb.reshape(batch, n_mem, D_MODEL),
      wq, wo, g, b)
    return out.reshape(batch * seq, D_MODEL)


def _mem_sample_kernel(q_ref, mk_ref, mv_ref, o_ref, *, group):
    for i in range(group):
        q = q_ref[i]
        s = jnp.sum(mk_ref[i] * q, axis=2, keepdims=True)
        p = jnp.exp(s - jnp.max(s, axis=0, keepdims=True))
        p = p * (1.0 / jnp.sum(p, axis=0, keepdims=True))
        o_ref[i] = jnp.sum(mv_ref[i] * p, axis=0)


def _mem_attn_sample(q, mem_k_all, mem_v_all, *, layer, group):
    bs = q.shape[0]
    n_mem = mem_k_all.shape[1]
    nb = bs // group
    qblk = pl.BlockSpec((group, MEM_HEADS, MEM_HD), lambda i: (i, 0, 0))
    mblk = pl.BlockSpec((group, n_mem, MEM_HEADS, MEM_HD), lambda i: (layer * nb + i, 0, 0, 0))
    out = pl.pallas_call(
        functools.partial(_mem_sample_kernel, group=group),
        grid=(nb,),
        in_specs=[qblk, mblk, mblk],
        out_specs=qblk,
        out_shape=jax.ShapeDtypeStruct((bs, MEM_HEADS, MEM_HD), F32),
        compiler_params=_cparams(("parallel",)),
        name="mem_attn_sample",
    )(q.reshape(bs, MEM_HEADS, MEM_HD), mem_k_all, mem_v_all)
    return out.reshape(bs, D_MODEL)


def _ffn_kernel(x_ref, w1_ref, w3_ref, w2_ref, g_ref, b_ref, o_ref, *, n_chunks):
    x = x_ref[...]
    xb = x.astype(BF16)
    acc = jnp.zeros(x.shape, F32)
    for c in range(n_chunks):
        h = _silu(jnp.dot(xb, w1_ref[c], preferred_element_type=F32)) * jnp.dot(xb, w3_ref[c],
                                                                                 preferred_element_type=F32)
        acc = acc + jnp.dot(h.astype(BF16), w2_ref[c], preferred_element_type=F32)
    o_ref[...] = _layer_norm(ALPHA * x + acc, g_ref[...], b_ref[...])


def _ffn(x, w1c, w3c, w2c, g, b, *, tm):
    m = x.shape[0]
    row = pl.BlockSpec((tm, D_MODEL), lambda i: (i, 0))
    wspec = lambda w: pl.BlockSpec(w.shape, lambda i: (0, 0, 0), pipeline_mode=pl.Buffered(1))
    return pl.pallas_call(
        functools.partial(_ffn_kernel, n_chunks=w1c.shape[0]),
        grid=(m // tm,),
        in_specs=[row, wspec(w1c), wspec(w3c), wspec(w2c), _const_spec(g.shape), _const_spec(b.shape)],
        out_specs=row,
        out_shape=jax.ShapeDtypeStruct(x.shape, F32),
        compiler_params=_cparams(("parallel",), vmem_mb=56),
        name="ffn_dense",
    )(x, w1c, w3c, w2c, g, b)


def _router_kernel(x_ref, w_ref, b_ref, o_ref):
    logits = jnp.dot(x_ref[...], w_ref[...], precision=lax.Precision.HIGHEST, preferred_element_type=F32)
    lane = lax.broadcasted_iota(jnp.int32, logits.shape, 1)
    logits = jnp.where(lane < N_EXPERTS, logits + b_ref[...], NEG_INF)
    m1 = jnp.max(logits, axis=1, keepdims=True)
    i1 = jnp.min(jnp.where(logits == m1, lane, LANES), axis=1, keepdims=True)
    rest = jnp.where(lane == i1, NEG_INF, logits)
    m2 = jnp.max(rest, axis=1, keepdims=True)
    i2 = jnp.min(jnp.where(rest == m2, lane, LANES), axis=1, keepdims=True)
    e = jnp.exp(m2 - m1)
    g1 = 1.0 / (1.0 + e)
    g2 = e * g1
    out = jnp.where(lane == 0, i1.astype(F32), jnp.where(lane == 1, i2.astype(F32),
                    jnp.where(lane == 2, g1, jnp.where(lane == 3, g2, 0.0))))
    o_ref[...] = out


def _router(x, w_pad, b_pad, *, tm):
    m = x.shape[0]
    return pl.pallas_call(
        _router_kernel,
        grid=(m // tm,),
        in_specs=[pl.BlockSpec((tm, D_MODEL), lambda i: (i, 0)), _const_spec(w_pad.shape), _const_spec(b_pad.shape)],
        out_specs=pl.BlockSpec((tm, LANES), lambda i: (i, 0)),
        out_shape=jax.ShapeDtypeStruct((m, LANES), F32),
        compiler_params=_cparams(("parallel",)),
        name="moe_router",
    )(x, w_pad, b_pad)


def _expert_kernel(te_ref, tv_ref, x_ref, w1_ref, w3_ref, w2_ref, o_ref, acc_sc):
    i = pl.program_id(0)
    j = pl.program_id(1)

    @pl.when(tv_ref[i] == 1)
    def _():
        xb = x_ref[...].astype(BF16)
        h = _silu(jnp.dot(xb, w1_ref[...], preferred_element_type=F32)) * jnp.dot(xb, w3_ref[...],
                                                                                  preferred_element_type=F32)
        part = jnp.dot(h.astype(BF16), w2_ref[...], preferred_element_type=F32)

        @pl.when(j == 0)
        def _():
            acc_sc[...] = part

        @pl.when(j > 0)
        def _():
            acc_sc[...] = acc_sc[...] + part

    @pl.when(j == pl.num_programs(1) - 1)
    def _():
        o_ref[...] = acc_sc[...]


def _experts(x_sorted, tile_expert, tile_valid, w1, w3, w2, *, tm, fc):
    n_slots = x_sorted.shape[0]
    nj = w1.shape[2] // fc
    last = nj - 1

    def jj(i, j, tv):
        return jnp.where(tv[i] == 1, j, last)

    xs = pl.BlockSpec((tm, D_MODEL), lambda i, j, te, tv: (i, 0))
    w13 = pl.BlockSpec((None, D_MODEL, fc), lambda i, j, te, tv: (te[i], 0, jj(i, j, tv)))
    w2s = pl.BlockSpec((None, fc, D_MODEL), lambda i, j, te, tv: (te[i], jj(i, j, tv), 0))
    return pl.pallas_call(
        _expert_kernel,
        grid_spec=pltpu.PrefetchScalarGridSpec(
            num_scalar_prefetch=2, grid=(n_slots // tm, nj),
            in_specs=[xs, w13, w13, w2s], out_specs=xs,
            scratch_shapes=[pltpu.VMEM((tm, D_MODEL), F32)]),
        out_shape=jax.ShapeDtypeStruct((n_slots, D_MODEL), F32),
        compiler_params=_cparams(("arbitrary", "arbitrary")),
        name="moe_experts",
    )(tile_expert, tile_valid, x_sorted, w1, w3, w2)


def _combine_ln_kernel(x_ref, y_ref, r_ref, g_ref, b_ref, o_ref):
    r = r_ref[...]
    f = r[:, 2:3] * y_ref[:, :D_MODEL] + r[:, 3:4] * y_ref[:, D_MODEL:]
    o_ref[...] = _layer_norm(ALPHA * x_ref[...] + f, g_ref[...], b_ref[...])


def _combine_ln(x, y2, route, g, b, *, tm):
    m = x.shape[0]
    row = lambda w: pl.BlockSpec((tm, w), lambda i: (i, 0))
    return pl.pallas_call(
        _combine_ln_kernel,
        grid=(m // tm,),
        in_specs=[row(D_MODEL), row(2 * D_MODEL), row(LANES), _const_spec(g.shape), _const_spec(b.shape)],
        out_specs=row(D_MODEL),
        out_shape=jax.ShapeDtypeStruct(x.shape, F32),
        compiler_params=_cparams(("parallel",)),
        name="moe_combine_ln",
    )(x, y2, route, g, b)


def _moe(x_all, router_w, router_b, w1, w3, w2, g, b, *, tm_e, fc):
    t = x_all.shape[0]
    w_pad = jnp.zeros((D_MODEL, LANES), F32).at[:, :N_EXPERTS].set(router_w)
    b_pad = jnp.zeros((1, LANES), F32).at[0, :N_EXPERTS].set(router_b)
    route = _router(x_all, w_pad, b_pad, tm=128)
    top_i = route[:, :2].astype(jnp.int32).reshape(-1)
    onehot = (top_i[:, None] == jnp.arange(N_EXPERTS)[None, :]).astype(jnp.int32)
    csum = jnp.cumsum(onehot, axis=0)
    counts = csum[-1]
    rank = jnp.sum((csum - onehot) * onehot, axis=1)
    padded = ((counts + tm_e - 1) // tm_e) * tm_e
    ends = jnp.cumsum(padded)
    starts = ends - padded
    slot = starts[top_i] + rank
    n_tiles = (2 * t + N_EXPERTS * (tm_e - 1)) // tm_e + 1
    n_slots = n_tiles * tm_e
    src = jnp.zeros((n_slots,), jnp.int32).at[slot].set(jnp.arange(2 * t, dtype=jnp.int32) // 2)
    tile_start = jnp.arange(n_tiles, dtype=jnp.int32) * tm_e
    tile_valid = (tile_start < ends[-1]).astype(jnp.int32)
    tile_expert = jnp.sum((tile_start[:, None] >= ends[None, :]).astype(jnp.int32), axis=1)
    tile_expert = jnp.minimum(tile_expert, N_EXPERTS - 1)
    last_e = tile_expert[jnp.maximum(ends[-1] // tm_e - 1, 0)]
    tile_expert = jnp.where(tile_valid == 1, tile_expert, last_e)
    x_sorted = jnp.take(x_all, src, axis=0, mode="clip")
    y_sorted = _experts(x_sorted, tile_expert, tile_valid, w1, w3, w2, tm=tm_e, fc=fc)
    y2 = jnp.take(y_sorted, slot, axis=0, mode="clip").reshape(t, 2 * D_MODEL)
    return _combine_ln(x_all, y2, route, g, b, tm=128)


def _pack_w_in(w_in_t):
    q, k, v, z, xbc, dt, pu = jnp.split(w_in_t, np.cumsum([512, 512, 512, 256, 768, 4]).tolist(), axis=0)
    pad = jnp.zeros((DT_PAD - SSM_HEADS, D_MODEL), w_in_t.dtype)
    return jnp.concatenate([q, k, v, z, xbc, pu, dt, pad], axis=0).astype(BF16)


def _block_diag(pool_w):
    out = jnp.zeros((POOL_WIDTH, POOL_WIDTH), F32)
    for g in range(len(POOL_WINDOWS)):
        out = out.at[g * POOL_GDIM:(g + 1) * POOL_GDIM, g * POOL_GDIM:(g + 1) * POOL_GDIM].set(pool_w[g])
    return out.astype(BF16)


def _pad_heads(v):
    return jnp.zeros((1, DT_PAD), F32).at[0, :SSM_HEADS].set(v)


def kernel(x_prompt, x_sample, cache_k, cache_v, cache_mem_k, cache_mem_v, state_ssm, state_conv, state_pool, page_table, mem_prompt, w_in, lam_q1, lam_k1, lam_q2, lam_k2, da_norm_w, conv_w, conv_b, dt_bias, a_log, d_skip, ssm_norm_w, pool_w, pool_scale, w_out, ln1_g, ln1_b, wq_mem, wk_mem, wv_mem, wo_mem, ln2_g, ln2_b, ffn_w1, ffn_w3, ffn_w2, moe_router, moe_router_b, moe_w1, moe_w3, moe_w2, ln3_g, ln3_b):
    bp, seq, _ = x_prompt.shape
    bs = x_sample.shape[0]
    n_pages = page_table.shape[1]
    past = n_pages * PAGE_SIZE
    n_mem = mem_prompt.shape[1]
    tp_rows = bp * seq
    TM = 512

    xp = x_prompt.reshape(tp_rows, D_MODEL)
    xs = x_sample.reshape(bs, D_MODEL)
    tab_p = _rope_tables(jnp.arange(seq))
    tab_s = _rope_tables(jnp.full((1,), past))
    cache_k4 = jnp.transpose(cache_k, (0, 1, 3, 4, 5, 2)).reshape(DEPTH, -1, DA_WIDTH, PAGE_SIZE)
    cache_v4 = cache_v.reshape(DEPTH, -1, PAGE_SIZE * DA_HEADS, DA_DV)
    mem_k_all = cache_mem_k.reshape(DEPTH * bs, n_mem, MEM_HEADS, MEM_HD)
    mem_v_all = cache_mem_v.reshape(DEPTH * bs, n_mem, MEM_HEADS, MEM_HD)
    state_all = state_ssm.reshape(DEPTH * bs, SSM_WIDTH, SSM_STATE)
    conv_hist = jnp.transpose(state_conv, (0, 2, 1, 3))
    pool_hist = jnp.transpose(state_pool, (0, 2, 1, 3))
    w_in_t = jnp.transpose(w_in, (2, 0, 1))
    mem_flat = mem_prompt.reshape(bp * n_mem, D_MODEL)
    row = lambda v: v.reshape(1, -1)

    outs = {k: [] for k in ("kp", "vp", "ks", "vs", "mk", "mv", "hp", "hs", "cp", "cs", "pp", "ps")}
    for l in range(DEPTH):
        lam_init = 0.8 - 0.6 * math.exp(-0.3 * l)
        w_cat = _pack_w_in(w_in_t[:, l, :])
        lam_vecs = (row(lam_q1[l]), row(lam_k1[l]), row(lam_q2[l]), row(lam_k2[l]))
        nw = row(da_norm_w[l])
        dtb, alog = _pad_heads(dt_bias[l]), _pad_heads(a_log[l])
        dsk = row(jnp.repeat(d_skip[l], SSM_HEADDIM))
        snw = row(ssm_norm_w[l])
        pwbd = _block_diag(pool_w[l])
        psc = row(pool_scale[l])
        wo_b = w_out[l].astype(BF16)
        wo_parts = (wo_b[:DA_WIDTH], wo_b[DA_WIDTH:DA_WIDTH + SSM_WIDTH], wo_b[DA_WIDTH + SSM_WIDTH:])
        g1, b1, g2, b2, g3, b3 = (row(ln1_g[l]), row(ln1_b[l]), row(ln2_g[l]), row(ln2_b[l]),
                                  row(ln3_g[l]), row(ln3_b[l]))

        qp, kp, kb, vp, zp, xbcp, pup, dtp = _proj_in(xp, w_cat, tab_p, prompt=True, tm=TM, seq=seq)
        oa_p = _flash_attn(qp, kb, vp, lam_vecs, nw.reshape(DA_DV, 1), batch=bp, seq=seq, tq=512, tk=512,
                           lam_init=lam_init)
        y_p, h_p = _ssd_prompt(xbcp, zp, dtp, conv_w[l], row(conv_b[l]), dtb, alog, dsk, snw, batch=bp, seq=seq)
        yp_p = _pool_prompt(pup, pwbd, psc, batch=bp, seq=seq, tp=512)
        xp = _mm_res_ln((oa_p, y_p, yp_p), wo_parts, xp, g1, b1, tm=TM)

        q_s, k_s, v_s, z_s, xbc_s, pu_s, dt_s = _proj_in(xs, w_cat, tab_s, prompt=False, tm=bs, seq=1)
        oa_s = _paged_attn(q_s, k_s, v_s, cache_k4, cache_v4, page_table, lam_vecs, nw, layer=l, lam_init=lam_init)
        xact, dt_a, dec, yp_s = _sample_prep(xbc_s, conv_hist[l], conv_w[l], row(conv_b[l]), dt_s, dtb, alog,
                                             pu_s, pool_hist[l], pwbd, psc, pos0=past)
        y_s, h_s = _ssd_step(state_all, l, xact, z_s, dt_a, dec, dsk, snw)
        xs = _mm_res_ln((oa_s, y_s, yp_s), wo_parts, xs, g1, b1, tm=bs)

        wq_b, wk_b, wv_b, wom_b = (wq_mem[l].astype(BF16), wk_mem[l].astype(BF16), wv_mem[l].astype(BF16),
                                   wo_mem[l].astype(BF16))
        mk, mkb = _matmul(mem_flat, wk_b, tm=bp * n_mem // 2, dtypes=(F32, BF16))
        mv, mvb = _matmul(mem_flat, wv_b, tm=bp * n_mem // 2, dtypes=(F32, BF16))
        xp = _mem_attn_prompt(xp, mkb, mvb, wq_b, wom_b, g2, b2, batch=bp, seq=seq, tm=TM)
        (qm_s,) = _matmul(xs, wq_b, tm=bs, scale=MEM_HD ** -0.5)
        om_s = _mem_attn_sample(qm_s, mem_k_all, mem_v_all, layer=l, group=4)
        xs = _mm_res_ln((om_s,), (wom_b,), xs, g2, b2, tm=bs)

        j = l // 2
        if l % 2 == 0:
            f = ffn_w1.shape[2]
            fc = 256
            w1c = ffn_w1[j].astype(BF16).reshape(D_MODEL, f // fc, fc).transpose(1, 0, 2)
            w3c = ffn_w3[j].astype(BF16).reshape(D_MODEL, f // fc, fc).transpose(1, 0, 2)
            w2c = ffn_w2[j].astype(BF16).reshape(f // fc, fc, D_MODEL)
            xp = _ffn(xp, w1c, w3c, w2c, g3, b3, tm=TM)
            xs = _ffn(xs, w1c, w3c, w2c, g3, b3, tm=bs)
        else:
            x_all = jnp.concatenate([xp, xs], axis=0)
            x_all = _moe(x_all, moe_router[j], moe_router_b[j], moe_w1[j].astype(BF16), moe_w3[j].astype(BF16),
                         moe_w2[j].astype(BF16), g3, b3, tm_e=512, fc=512)
            xp, xs = x_all[:tp_rows], x_all[tp_rows:]

        outs["kp"].append(kp.reshape(bp, seq // PAGE_SIZE, PAGE_SIZE, DA_HEADS, 2, DA_DK))
        outs["vp"].append(vp.reshape(bp, seq // PAGE_SIZE, PAGE_SIZE, DA_HEADS, DA_DV))
        outs["ks"].append(k_s.reshape(bs, 1, DA_HEADS, 2, DA_DK))
        outs["vs"].append(v_s.reshape(bs, 1, DA_HEADS, DA_DV))
        outs["mk"].append(mk.reshape(bp, n_mem, MEM_HEADS, MEM_HD))
        outs["mv"].append(mv.reshape(bp, n_mem, MEM_HEADS, MEM_HD))
        outs["hp"].append(h_p)
        outs["hs"].append(h_s)
        outs["cp"].append(xbcp.reshape(bp, seq, CONV_CH)[:, seq - (SSM_CONV - 1):])
        outs["cs"].append(jnp.transpose(jnp.concatenate([conv_hist[l, 1:], xbc_s[None]], axis=0), (1, 0, 2)))
        outs["pp"].append(pup.reshape(bp, seq, POOL_WIDTH)[:, seq - POOL_BUF:])
        outs["ps"].append(jnp.transpose(jnp.concatenate([pool_hist[l, 1:], pu_s[None]], axis=0), (1, 0, 2)))

    st = lambda k: jnp.stack(outs[k])
    return (xp.reshape(bp, seq, D_MODEL), xs.reshape(bs, 1, D_MODEL), st("kp"), st("vp"), st("ks"), st("vs"),
            st("mk"), st("mv"), st("hp"), st("hs"), st("cp"), st("cs"), st("pp"), st("ps"))
```

```python
import functools
import math

import jax
import jax.numpy as jnp
import numpy as np
from jax import lax
from jax.experimental import pallas as pl
from jax.experimental.pallas import tpu as pltpu

F32 = jnp.float32
BF16 = jnp.bfloat16

D_MODEL = 1024
DEPTH = 2
PAGE_SIZE = 128
DA_HEADS = 4
DA_WIDTH = 512
DA_DV = 128
DA_DK = 64
ROT_DIM = 16
ROPE_THETA = 500000.0
SSM_WIDTH = 256
SSM_HEADS = 4
SSM_HEADDIM = 64
SSM_STATE = 128
SSM_CONV = 4
SSM_CHUNK = 128
CONV_CH = 768
POOL_WIDTH = 256
POOL_WINDOWS = (2, 4, 8, 16)
POOL_GDIM = 64
POOL_BUF = 15
MEM_HEADS = 4
MEM_HD = 256
N_EXPERTS = 8
ALPHA = (2.0 * DEPTH) ** 0.25
LN_EPS = 1e-5
RMS_EPS = 1e-6

LANES = 128
DT_PAD = LANES
_C_Q, _C_K, _C_V, _C_Z, _C_X, _C_P, _C_DT, _C_END = 0, 512, 1024, 1536, 1792, 2560, 2816, 2944

NEG_INF = float("-inf")
Q_SCALE = DA_DK ** -0.5 * math.log2(math.e)


def _cparams(sem, vmem_mb=48):
    return pltpu.CompilerParams(dimension_semantics=sem, vmem_limit_bytes=vmem_mb * 1024 * 1024)


def _dot(a, b):
    return jnp.dot(a.astype(BF16), b.astype(BF16), preferred_element_type=F32)


def _dot_nt(a, b):
    return lax.dot_general(a.astype(BF16), b.astype(BF16), (((1,), (1,)), ((), ())),
                           preferred_element_type=F32)


def _layer_norm(x, g, b):
    mu = jnp.mean(x, axis=-1, keepdims=True)
    xc = x - mu
    var = jnp.mean(xc * xc, axis=-1, keepdims=True)
    return xc * lax.rsqrt(var + LN_EPS) * g + b


def _silu(x):
    return x * (1.0 / (1.0 + jnp.exp(-x)))


def _softplus(x):
    return jnp.maximum(x, 0.0) + jnp.log(1.0 + jnp.exp(-jnp.abs(x)))


def _const_spec(shape):
    nd = len(shape)
    return pl.BlockSpec(shape, lambda *_: (0,) * nd)


def _rope_tables(pos):
    half = ROT_DIM // 2
    inv = ROPE_THETA ** (-jnp.arange(half, dtype=F32) * 2.0 / ROT_DIM)
    ang = pos.astype(F32)[:, None] * inv[None, :]
    cos, sin = jnp.cos(ang), jnp.sin(ang)
    n = pos.shape[0]
    c64 = jnp.concatenate([cos, cos, jnp.ones((n, DA_DK - ROT_DIM), F32)], axis=1)
    s1 = jnp.concatenate([-sin, jnp.zeros((n, DA_DK - half), F32)], axis=1)
    s2 = jnp.concatenate([jnp.zeros((n, half), F32), sin, jnp.zeros((n, DA_DK - ROT_DIM), F32)], axis=1)
    two = lambda t: jnp.concatenate([t, t], axis=1)
    return two(c64), two(s1), two(s2)


def _proj_in_kernel(x_ref, w_ref, c_ref, s1_ref, s2_ref, *out_refs, prompt):
    xb = x_ref[...].astype(BF16)
    cc, s1, s2 = c_ref[...], s1_ref[...], s2_ref[...]

    def rope(t):
        outs = []
        for h in range(DA_HEADS):
            th = t[:, h * LANES:(h + 1) * LANES]
            outs.append(th * cc + pltpu.roll(th, LANES - ROT_DIM // 2, 1) * s1
                        + pltpu.roll(th, ROT_DIM // 2, 1) * s2)
        return jnp.concatenate(outs, axis=1)

    def seg(a, b):
        return lax.dot_general(xb, w_ref[a:b, :], (((1,), (1,)), ((), ())), preferred_element_type=F32)

    q = rope(seg(_C_Q, _C_K)) * Q_SCALE
    k = rope(seg(_C_K, _C_V))
    v = seg(_C_V, _C_Z)
    if prompt:
        q_ref, kb_ref, vb_ref, kpg_ref, vrow_ref, z_ref, xbc_ref, pu_ref, dt_ref = out_refs
        kb_ref[...] = k.astype(BF16)
        vb_ref[...] = v.astype(BF16)
        for pg in range(kpg_ref.shape[0]):
            kpg_ref[pg] = k[pg * PAGE_SIZE:(pg + 1) * PAGE_SIZE, :].T
        for h in range(DA_HEADS):
            vrow_ref[pl.ds(h, v.shape[0], stride=DA_HEADS), :] = v[:, h * DA_DV:(h + 1) * DA_DV]
    else:
        q_ref, k_ref, v_ref, z_ref, xbc_ref, pu_ref, dt_ref = out_refs
        k_ref[...] = k
        v_ref[...] = v
    q_ref[...] = q
    z_ref[...] = seg(_C_Z, _C_X)
    xbc_ref[...] = seg(_C_X, _C_P)
    pu_ref[...] = seg(_C_P, _C_DT)
    dt_ref[...] = seg(_C_DT, _C_END)


def _proj_in(x, w_cat, tables, *, prompt, tm, seq, rows, row0=0):
    m = rows
    grid = (m // tm,)
    xrow = pl.BlockSpec((tm, D_MODEL), lambda i: (row0 + i, 0))

    def rows_out(w, dt):
        return pl.BlockSpec((tm, w), lambda i: (i, 0)), jax.ShapeDtypeStruct((m, w), dt)

    if prompt:
        npos = seq // tm
        ppt = tm // PAGE_SIZE
        tab = pl.BlockSpec((tm, LANES), lambda i: (i % npos, 0))
        outs = [rows_out(DA_WIDTH, F32), rows_out(DA_WIDTH, BF16), rows_out(DA_WIDTH, BF16),
                (pl.BlockSpec((ppt, DA_WIDTH, PAGE_SIZE), lambda i: (i, 0, 0)),
                 jax.ShapeDtypeStruct((m // PAGE_SIZE, DA_WIDTH, PAGE_SIZE), F32)),
                (pl.BlockSpec((tm * DA_HEADS, DA_DV), lambda i: (i, 0)),
                 jax.ShapeDtypeStruct((m * DA_HEADS, DA_DV), F32))]
    else:
        tab = _const_spec((1, LANES))
        outs = [rows_out(DA_WIDTH, F32), rows_out(DA_WIDTH, F32), rows_out(DA_WIDTH, F32)]
    outs += [rows_out(SSM_WIDTH, F32), rows_out(CONV_CH, F32), rows_out(POOL_WIDTH, F32), rows_out(DT_PAD, F32)]
    return pl.pallas_call(
        functools.partial(_proj_in_kernel, prompt=prompt),
        grid=grid,
        in_specs=[xrow, _const_spec(w_cat.shape), tab, tab, tab],
        out_specs=[o[0] for o in outs],
        out_shape=[o[1] for o in outs],
        compiler_params=_cparams(("parallel",)),
        name="proj_in",
    )(x, w_cat, *tables)


def _lam_value(lq1, lk1, lq2, lk2, lam_init):
    return (jnp.exp(jnp.sum(lq1 * lk1, axis=1, keepdims=True))
            - jnp.exp(jnp.sum(lq2 * lk2, axis=1, keepdims=True)) + lam_init)


def _flash_kernel(q_ref, k_ref, v_ref, lq1_ref, lk1_ref, lq2_ref, lk2_ref, nw_ref, o_ref,
                  vt_sc, s_sc, p_sc, m_sc, acc_sc, *, tq, tk, lam_init):
    i = pl.program_id(2)
    n_kt, aug, _ = vt_sc.shape

    @pl.when(i == 0)
    def _():
        ones = jnp.ones((aug - DA_DV, tk), BF16)
        for j in range(n_kt):
            vt = v_ref[j * tk:(j + 1) * tk, :].astype(F32).T.astype(BF16)
            vt_sc[j] = jnp.concatenate([vt, ones], axis=0)

    qt = q_ref[...].T
    drow = lax.broadcasted_iota(jnp.int32, qt.shape, 0)
    qts = (jnp.where(drow < DA_DK, qt, 0.0).astype(BF16), jnp.where(drow >= DA_DK, qt, 0.0).astype(BF16))

    def stage(j, cur, mask=None, last=False):
        nxt = 1 - cur
        if not last:
            kt = k_ref[pl.ds(pl.multiple_of((j + 1) * tk, tk), tk), :]
            for c in range(2):
                s_sc[nxt, c] = jnp.dot(kt, qts[c], preferred_element_type=F32)
        vta = vt_sc[jnp.maximum(j - 1, 0)]
        for c in range(2):
            pv = jnp.dot(vta, p_sc[cur, c], preferred_element_type=F32)
            s = s_sc[cur, c]
            if mask is not None:
                s = jnp.where(mask, s, NEG_INF)
            m = m_sc[c]
            m_new = jnp.maximum(m, jnp.max(s, axis=0, keepdims=True))
            p_sc[nxt, c] = jnp.exp2(s - m_new).astype(BF16)
            acc_sc[c] = jnp.exp2(m - m_new) * (acc_sc[c] + pv)
            m_sc[c] = m_new

    kt0 = k_ref[0:tk, :]
    for c in range(2):
        s_sc[0, c] = jnp.dot(kt0, qts[c], preferred_element_type=F32)
        p_sc[0, c] = jnp.zeros((tk, tq), BF16)
        m_sc[c] = jnp.full((1, tq), NEG_INF, F32)
        acc_sc[c] = jnp.zeros((aug, tq), F32)

    def pair(t, carry):
        stage(2 * t, 0)
        stage(2 * t + 1, 1)
        return carry

    lax.fori_loop(0, i // 2, pair, 0)
    odd = i % 2 == 1

    @pl.when(odd)
    def _():
        stage(i - 1, 0)

    krow = lax.broadcasted_iota(jnp.int32, (tk, tq), 0)
    qcol = lax.broadcasted_iota(jnp.int32, (tk, tq), 1)

    def finish(cur):
        stage(i, cur, mask=krow <= qcol, last=True)
        vta = vt_sc[i]
        a0, a1 = (acc_sc[c] + jnp.dot(vta, p_sc[1 - cur, c], preferred_element_type=F32) for c in range(2))
        lam = _lam_value(lq1_ref[...], lk1_ref[...], lq2_ref[...], lk2_ref[...], lam_init)
        o = a0[:DA_DV] * (1.0 / a0[DA_DV:DA_DV + 1]) - lam * (a1[:DA_DV] * (1.0 / a1[DA_DV:DA_DV + 1]))
        ms = jnp.mean(o * o, axis=0, keepdims=True)
        o = o * lax.rsqrt(ms + RMS_EPS) * nw_ref[...] * (1.0 - lam_init)
        o_ref[...] = o.T

    pl.when(odd)(lambda: finish(1))
    pl.when(jnp.logical_not(odd))(lambda: finish(0))


def _flash_attn(q, kb, v, lam_vecs, norm_w_col, *, batch, seq, tq, tk, lam_init):
    assert tq == tk
    q3 = q.reshape(batch, seq, DA_WIDTH)
    k3 = kb.reshape(batch, seq, DA_WIDTH)
    v3 = v.reshape(batch, seq, DA_WIDTH)
    qspec = pl.BlockSpec((None, tq, DA_DV), lambda b, h, i: (b, i, h))
    kvspec = pl.BlockSpec((None, seq, DA_DV), lambda b, h, i: (b, 0, h))
    vec = _const_spec((1, DA_DK))
    aug = DA_DV + 16
    out = pl.pallas_call(
        functools.partial(_flash_kernel, tq=tq, tk=tk, lam_init=lam_init),
        grid=(batch, DA_HEADS, seq // tq),
        in_specs=[qspec, kvspec, kvspec, vec, vec, vec, vec, _const_spec((DA_DV, 1))],
        out_specs=qspec,
        out_shape=jax.ShapeDtypeStruct((batch, seq, DA_WIDTH), F32),
        scratch_shapes=[pltpu.VMEM((seq // tk, aug, tk), BF16), pltpu.VMEM((2, 2, tk, tq), F32),
                        pltpu.VMEM((2, 2, tk, tq), BF16), pltpu.VMEM((2, 1, tq), F32),
                        pltpu.VMEM((2, aug, tq), F32)],
        compiler_params=_cparams(("parallel", "parallel", "arbitrary")),
        name="flash_diff_attn",
    )(q3, k3, v3, *lam_vecs, norm_w_col)
    return out.reshape(batch * seq, DA_WIDTH)


def _paged_kernel(pt_ref, q_ref, ks_ref, vs_ref, lq1_ref, lk1_ref, lq2_ref, lk2_ref, nw_ref, *rest,
                  n_pages, lam_init):
    k_refs = rest[:n_pages]
    v_refs = rest[n_pages:2 * n_pages]
    o_ref = rest[2 * n_pages]
    q = q_ref[...]
    nhc = 2 * DA_HEADS
    jj = lax.broadcasted_iota(jnp.int32, (2 * nhc, DA_WIDTH), 0)
    ll = lax.broadcasted_iota(jnp.int32, (2 * nhc, DA_WIDTH), 1)
    qblk = jnp.where((jj < nhc) & ((ll >> 6) == (jj & (DA_HEADS - 1)) * 2 + (jj >> 2)), q, 0.0)
    qb = qblk.astype(BF16)

    s_all = jnp.concatenate(
        [jnp.dot(qb, k_refs[j][...].astype(BF16), preferred_element_type=F32)[:nhc] for j in range(n_pages)],
        axis=1)
    s_self = jnp.sum(qblk[:nhc] * ks_ref[...], axis=1, keepdims=True)
    m = jnp.maximum(jnp.max(s_all, axis=1, keepdims=True), s_self)
    p = jnp.exp2(s_all - m)
    p_self = jnp.exp2(s_self - m)
    inv_l = 1.0 / (jnp.sum(p, axis=1, keepdims=True) + p_self)
    lam = _lam_value(lq1_ref[...], lk1_ref[...], lq2_ref[...], lk2_ref[...], lam_init)
    pn = p * inv_l
    pn_self = p_self * inv_l
    w = pn - lam * pltpu.roll(pn, DA_HEADS, 0)
    w_self = pn_self - lam * pltpu.roll(pn_self, DA_HEADS, 0)
    w16 = jnp.concatenate([w, jnp.zeros_like(w)], axis=0).astype(BF16)
    accs = [jnp.zeros((2 * nhc, DA_DV), F32) for _ in range(DA_HEADS)]
    for j in range(n_pages):
        wj = w16[:, j * PAGE_SIZE:(j + 1) * PAGE_SIZE]
        for h in range(DA_HEADS):
            vh = v_refs[j][pl.ds(h, PAGE_SIZE, stride=DA_HEADS), :]
            accs[h] = accs[h] + jnp.dot(wj, vh.astype(BF16), preferred_element_type=F32)
    outs = []
    for h in range(DA_HEADS):
        o = accs[h][h:h + 1, :] + w_self[h:h + 1, :] * vs_ref[:, h * DA_DV:(h + 1) * DA_DV]
        ms = jnp.mean(o * o, axis=1, keepdims=True)
        outs.append(o * lax.rsqrt(ms + RMS_EPS) * nw_ref[...] * (1.0 - lam_init))
    o_ref[...] = jnp.concatenate(outs, axis=1)


def _paged_attn(q, k_self, v_self, cache_k4, cache_v4, page_table, lam_vecs, norm_w4, *, layer, lam_init):
    bs, n_pages = page_table.shape
    pt = page_table.reshape(-1)
    one = pl.BlockSpec((None, 1, DA_WIDTH), lambda b, pt: (b, 0, 0))
    vec = pl.BlockSpec((1, DA_DK), lambda b, pt: (0, 0))

    def page_spec(j):
        return pl.BlockSpec((None, None, DA_WIDTH, PAGE_SIZE), lambda b, pt: (layer, pt[b * n_pages + j], 0, 0))

    specs = [one, one, one, vec, vec, vec, vec, pl.BlockSpec((1, DA_DV), lambda b, pt: (0, 0))]
    specs += [page_spec(j) for j in range(n_pages)] * 2
    out = pl.pallas_call(
        functools.partial(_paged_kernel, n_pages=n_pages, lam_init=lam_init),
        grid_spec=pltpu.PrefetchScalarGridSpec(
            num_scalar_prefetch=1, grid=(bs,), in_specs=specs, out_specs=one),
        out_shape=jax.ShapeDtypeStruct((bs, 1, DA_WIDTH), F32),
        compiler_params=_cparams(("arbitrary",)),
        name="paged_diff_attn",
    )(pt, q.reshape(bs, 1, DA_WIDTH), k_self.reshape(bs, 1, DA_WIDTH), v_self.reshape(bs, 1, DA_WIDTH),
      *lam_vecs, norm_w4, *([cache_k4] * n_pages), *([cache_v4] * n_pages))
    return out.reshape(bs, DA_WIDTH)


def _head_expand(cols, width, per):
    rows = cols.shape[0]
    lane = lax.broadcasted_iota(jnp.int32, (rows, width), 1)
    out = jnp.broadcast_to(cols[:, 0:1], (rows, width))
    for h in range(1, width // per):
        out = jnp.where(lane >= h * per, cols[:, h:h + 1], out)
    return out


def _ssd_kernel(xbc_ref, z_ref, dt_ref, cw_ref, cb_ref, dtb_ref, alog_ref, dsk_ref, nw_ref,
                y_ref, st_ref, tail_sc, state_sc):
    c = pl.program_id(1)
    q = SSM_CHUNK

    @pl.when(c == 0)
    def _():
        tail_sc[...] = jnp.zeros_like(tail_sc)
        state_sc[...] = jnp.zeros_like(state_sc)

    u = xbc_ref[...]
    full = jnp.concatenate([tail_sc[...], u], axis=0)
    tail_sc[...] = u[q - 8:, :]
    conv = full * cw_ref[SSM_CONV - 1:SSM_CONV, :]
    for j in range(1, SSM_CONV):
        conv = conv + pltpu.roll(full, j, 0) * cw_ref[SSM_CONV - 1 - j:SSM_CONV - j, :]
    xbc = _silu(conv[8:, :] + cb_ref[...])
    xs = xbc[:, :SSM_WIDTH]
    bm = (xbc[:, SSM_WIDTH:SSM_WIDTH + SSM_STATE], xbc[:, SSM_WIDTH + SSM_STATE:SSM_WIDTH + 2 * SSM_STATE])
    cm = (xbc[:, SSM_WIDTH + 2 * SSM_STATE:SSM_WIDTH + 3 * SSM_STATE], xbc[:, SSM_WIDTH + 3 * SSM_STATE:])

    dt = _softplus(dt_ref[...] + dtb_ref[...])
    dta = dt * (-jnp.exp(alog_ref[...]))
    ti = lax.broadcasted_iota(jnp.int32, (q, q), 0)
    si = lax.broadcasted_iota(jnp.int32, (q, q), 1)
    causal = si <= ti
    tri = jnp.where(causal, 1.0, 0.0)
    a_cs = jnp.dot(tri, dta, precision=lax.Precision.HIGHEST, preferred_element_type=F32)
    a_cs_t = a_cs.T
    last = a_cs[q - 1:q, :]

    dt_x = _head_expand(dt, SSM_WIDTH, SSM_HEADDIM)
    xdt = xs * dt_x
    xdt_end = xdt * _head_expand(jnp.exp(last - a_cs), SSM_WIDTH, SSM_HEADDIM)
    lane = lax.broadcasted_iota(jnp.int32, (q, SSM_WIDTH), 1)

    cb = [_dot_nt(cm[g], bm[g]) for g in range(2)]
    y_diag = jnp.zeros((q, SSM_WIDTH), F32)
    for h in range(SSM_HEADS):
        seg = a_cs[:, h:h + 1] - a_cs_t[h:h + 1, :]
        decay = jnp.exp(jnp.where(causal, seg, NEG_INF))
        xh = jnp.where((lane >= h * SSM_HEADDIM) & (lane < (h + 1) * SSM_HEADDIM), xdt, 0.0)
        y_diag = y_diag + _dot(cb[h // 2] * decay, xh)

    state = state_sc[...]
    sb = state.astype(BF16)
    y_off = jnp.where(lane < 2 * SSM_HEADDIM, _dot_nt(cm[0], sb), _dot_nt(cm[1], sb))
    y_off = y_off * _head_expand(jnp.exp(a_cs), SSM_WIDTH, SSM_HEADDIM)

    xt = xdt_end.T
    rowi = lax.broadcasted_iota(jnp.int32, (SSM_WIDTH, SSM_STATE), 0)
    new = jnp.where(rowi < 2 * SSM_HEADDIM, _dot(xt, bm[0]), _dot(xt, bm[1]))
    e_last = jnp.exp(last)
    dec_rows = jnp.broadcast_to(e_last[:, 0:1], (SSM_WIDTH, SSM_STATE))
    for h in range(1, SSM_HEADS):
        dec_rows = jnp.where(rowi >= h * SSM_HEADDIM, e_last[:, h:h + 1], dec_rows)
    state = state * dec_rows + new
    state_sc[...] = state

    y = y_diag + y_off + dsk_ref[...] * xs
    y = y * _silu(z_ref[...])
    ms = jnp.mean(y * y, axis=1, keepdims=True)
    y_ref[...] = y * lax.rsqrt(ms + RMS_EPS) * nw_ref[...]

    @pl.when(c == pl.num_programs(1) - 1)
    def _():
        st_ref[...] = state


def _ssd_prompt(xbc, z, dt_raw, conv_w, conv_b, dt_bias, a_log, d_skip_x, norm_w, *, batch, seq):
    nc = seq // SSM_CHUNK
    blk = lambda w: pl.BlockSpec((None, SSM_CHUNK, w), lambda b, c: (b, c, 0))
    y, st = pl.pallas_call(
        _ssd_kernel,
        grid=(batch, nc),
        in_specs=[blk(CONV_CH), blk(SSM_WIDTH), blk(DT_PAD),
                  _const_spec((SSM_CONV, CONV_CH)), _const_spec((1, CONV_CH)),
                  _const_spec((1, DT_PAD)), _const_spec((1, DT_PAD)),
                  _const_spec((1, SSM_WIDTH)), _const_spec((1, SSM_WIDTH))],
        out_specs=[blk(SSM_WIDTH), pl.BlockSpec((None, SSM_WIDTH, SSM_STATE), lambda b, c: (b, 0, 0))],
        out_shape=[jax.ShapeDtypeStruct((batch, seq, SSM_WIDTH), F32),
                   jax.ShapeDtypeStruct((batch, SSM_WIDTH, SSM_STATE), F32)],
        scratch_shapes=[pltpu.VMEM((8, CONV_CH), F32), pltpu.VMEM((SSM_WIDTH, SSM_STATE), F32)],
        compiler_params=_cparams(("parallel", "arbitrary")),
        name="ssd_prompt",
    )(xbc.reshape(batch, seq, CONV_CH), z.reshape(batch, seq, SSM_WIDTH), dt_raw.reshape(batch, seq, DT_PAD),
      conv_w, conv_b, dt_bias, a_log, d_skip_x, norm_w)
    return y.reshape(batch * seq, SSM_WIDTH), st.reshape(batch, SSM_HEADS, SSM_HEADDIM, SSM_STATE)


def _pool_select(sums, inv_cnt, lane):
    out = sums[POOL_WINDOWS[0]] * inv_cnt[POOL_WINDOWS[0]]
    for g in range(1, len(POOL_WINDOWS)):
        w = POOL_WINDOWS[g]
        out = jnp.where(lane >= g * POOL_GDIM, sums[w] * inv_cnt[w], out)
    return out


def _pool_kernel(u_ref, w_ref, sc_ref, y_ref, tail_sc, *, tp):
    c = pl.program_id(1)
    halo = 16

    @pl.when(c == 0)
    def _():
        tail_sc[...] = jnp.zeros_like(tail_sc)

    u = u_ref[...]
    full = jnp.concatenate([tail_sc[...], u], axis=0)
    tail_sc[...] = u[tp - halo:, :]
    sums = {}
    run = full
    w = 1
    while w < max(POOL_WINDOWS):
        run = run + pltpu.roll(run, w, 0)
        w *= 2
        sums[w] = run[halo:, :]
    pos = c * tp + lax.broadcasted_iota(jnp.int32, (tp, 1), 0)
    inv_cnt = {w: 1.0 / jnp.minimum(w, pos + 1).astype(F32) for w in POOL_WINDOWS}
    lane = lax.broadcasted_iota(jnp.int32, (tp, POOL_WIDTH), 1)
    d = _pool_select(sums, inv_cnt, lane) - u
    y_ref[...] = _dot(d, w_ref[...]) * sc_ref[...]


def _pool_prompt(pu, pool_wbd, pool_scale, *, batch, seq, tp):
    blk = pl.BlockSpec((None, tp, POOL_WIDTH), lambda b, c: (b, c, 0))
    y = pl.pallas_call(
        functools.partial(_pool_kernel, tp=tp),
        grid=(batch, seq // tp),
        in_specs=[blk, _const_spec((POOL_WIDTH, POOL_WIDTH)), _const_spec((1, POOL_WIDTH))],
        out_specs=blk,
        out_shape=jax.ShapeDtypeStruct((batch, seq, POOL_WIDTH), F32),
        scratch_shapes=[pltpu.VMEM((16, POOL_WIDTH), F32)],
        compiler_params=_cparams(("parallel", "arbitrary")),
        name="pool_prompt",
    )(pu.reshape(batch, seq, POOL_WIDTH), pool_wbd, pool_scale)
    return y.reshape(batch * seq, POOL_WIDTH)


def _sample_prep_kernel(xbc_ref, cbuf_ref, cw_ref, cb_ref, dt_ref, dtb_ref, alog_ref, pu_ref, pbuf_ref,
                        pw_ref, psc_ref, xact_ref, dto_ref, dec_ref, yp_ref, *, pos0):
    u = xbc_ref[...]
    conv = u * cw_ref[SSM_CONV - 1:SSM_CONV, :] + cb_ref[...]
    for j in range(SSM_CONV - 1):
        conv = conv + cbuf_ref[j] * cw_ref[j:j + 1, :]
    xact_ref[...] = _silu(conv)
    dt = _softplus(dt_ref[...] + dtb_ref[...])
    dto_ref[...] = dt
    dec_ref[...] = jnp.exp(dt * (-jnp.exp(alog_ref[...])))
    pu = pu_ref[...]
    run = pu
    sums = {}
    for j in range(1, max(POOL_WINDOWS)):
        run = run + pbuf_ref[POOL_BUF - j]
        if j + 1 in POOL_WINDOWS:
            sums[j + 1] = run
    inv_cnt = {w: 1.0 / float(min(w, pos0 + 1)) for w in POOL_WINDOWS}
    lane = lax.broadcasted_iota(jnp.int32, pu.shape, 1)
    d = _pool_select(sums, inv_cnt, lane) - pu
    yp_ref[...] = _dot(d, pw_ref[...]) * psc_ref[...]


def _sample_prep(xbc, conv_buf, conv_w, conv_b, dt_raw, dt_bias, a_log, pu, pool_buf, pool_wbd, pool_scale, *, pos0):
    bs = xbc.shape[0]
    args = (xbc, conv_buf, conv_w, conv_b, dt_raw, dt_bias, a_log, pu, pool_buf, pool_wbd, pool_scale)
    shapes = [(bs, CONV_CH), (bs, DT_PAD), (bs, DT_PAD), (bs, POOL_WIDTH)]
    return pl.pallas_call(
        functools.partial(_sample_prep_kernel, pos0=pos0),
        grid=(1,),
        in_specs=[_const_spec(a.shape) for a in args],
        out_specs=[_const_spec(s) for s in shapes],
        out_shape=[jax.ShapeDtypeStruct(s, F32) for s in shapes],
        compiler_params=_cparams(("arbitrary",)),
        name="sample_prep",
    )(*args)


def _ssd_step_kernel(s_ref, x_ref, z_ref, dt_ref, dec_ref, b_ref, c_ref, dsk_ref, nw_ref, y_ref, so_ref, *, group):
    rowi = lax.broadcasted_iota(jnp.int32, (SSM_WIDTH, SSM_STATE), 0)
    for i in range(group):
        brow, crow = b_ref[i], c_ref[i]
        bsel = jnp.where(rowi < 2 * SSM_HEADDIM, brow[:, :SSM_STATE], brow[:, SSM_STATE:])
        csel = jnp.where(rowi < 2 * SSM_HEADDIM, crow[:, :SSM_STATE], crow[:, SSM_STATE:])
        x = x_ref[i]
        sn = s_ref[i] * dec_ref[i] + (x * dt_ref[i]) * bsel
        so_ref[i] = sn
        y = jnp.sum(sn * csel, axis=1, keepdims=True) + dsk_ref[...] * x
        y = y * _silu(z_ref[i])
        ms = jnp.sum(y * y, axis=0, keepdims=True) * (1.0 / SSM_WIDTH)
        y_ref[i] = y * lax.rsqrt(ms + RMS_EPS) * nw_ref[...]


def _ssd_step(state_all, layer, xact, z, dt, dec, d_skip_x, norm_w, *, group=8):
    bs = xact.shape[0]
    nb = bs // group
    rep = lambda t: jnp.repeat(t[:, :SSM_HEADS], SSM_HEADDIM, axis=1).reshape(bs, SSM_WIDTH, 1)
    colb = pl.BlockSpec((group, SSM_WIDTH, 1), lambda b: (b, 0, 0))
    rowb = pl.BlockSpec((group, 1, 2 * SSM_STATE), lambda b: (b, 0, 0))
    st_in = pl.BlockSpec((group, SSM_WIDTH, SSM_STATE), lambda b: (layer * nb + b, 0, 0))
    st_out = pl.BlockSpec((group, SSM_WIDTH, SSM_STATE), lambda b: (b, 0, 0))
    y, st = pl.pallas_call(
        functools.partial(_ssd_step_kernel, group=group),
        grid=(nb,),
        in_specs=[st_in, colb, colb, colb, colb, rowb, rowb,
                  _const_spec((SSM_WIDTH, 1)), _const_spec((SSM_WIDTH, 1))],
        out_specs=[colb, st_out],
        out_shape=[jax.ShapeDtypeStruct((bs, SSM_WIDTH, 1), F32),
                   jax.ShapeDtypeStruct((bs, SSM_WIDTH, SSM_STATE), F32)],
        compiler_params=_cparams(("parallel",)),
        name="ssd_step",
    )(state_all, xact[:, :SSM_WIDTH].reshape(bs, SSM_WIDTH, 1), z.reshape(bs, SSM_WIDTH, 1), rep(dt), rep(dec),
      xact[:, SSM_WIDTH:SSM_WIDTH + 2 * SSM_STATE].reshape(bs, 1, 2 * SSM_STATE),
      xact[:, SSM_WIDTH + 2 * SSM_STATE:].reshape(bs, 1, 2 * SSM_STATE),
      d_skip_x.reshape(SSM_WIDTH, 1), norm_w.reshape(SSM_WIDTH, 1))
    return y.reshape(bs, SSM_WIDTH), st.reshape(bs, SSM_HEADS, SSM_HEADDIM, SSM_STATE)


def _mm_kernel(x_ref, w_ref, *out_refs, scale):
    y = _dot(x_ref[...], w_ref[...])
    if scale != 1.0:
        y = y * scale
    for r in out_refs:
        r[...] = y.astype(r.dtype)


def _matmul(x, w, *, tm, scale=1.0, dtypes=(F32,)):
    m, k = x.shape
    n = w.shape[1]
    res = pl.pallas_call(
        functools.partial(_mm_kernel, scale=scale),
        grid=(m // tm,),
        in_specs=[pl.BlockSpec((tm, k), lambda i: (i, 0)), _const_spec(w.shape)],
        out_specs=[pl.BlockSpec((tm, n), lambda i: (i, 0)) for _ in dtypes],
        out_shape=[jax.ShapeDtypeStruct((m, n), d) for d in dtypes],
        compiler_params=_cparams(("parallel",)),
        name="matmul",
    )(x, w)
    return res


def _mm_res_ln_kernel(*refs, n_in):
    a_refs = refs[:n_in]
    w_refs = refs[n_in:2 * n_in]
    x_ref, g_ref, b_ref, o_ref = refs[2 * n_in:]
    h = _dot(a_refs[0][...], w_refs[0][...])
    for a, w in zip(a_refs[1:], w_refs[1:]):
        h = h + _dot(a[...], w[...])
    o_ref[...] = _layer_norm(ALPHA * x_ref[...] + h, g_ref[...], b_ref[...])


def _mm_res_ln(acts, weights, x, g, b, *, tm, x_row0=0):
    m = acts[0].shape[0]
    row = lambda a: pl.BlockSpec((tm, a.shape[1]), lambda i: (i, 0))
    return pl.pallas_call(
        functools.partial(_mm_res_ln_kernel, n_in=len(acts)),
        grid=(m // tm,),
        in_specs=[row(a) for a in acts] + [_const_spec(w.shape) for w in weights]
        + [pl.BlockSpec((tm, D_MODEL), lambda i: (x_row0 + i, 0)), _const_spec(g.shape), _const_spec(b.shape)],
        out_specs=pl.BlockSpec((tm, D_MODEL), lambda i: (i, 0)),
        out_shape=jax.ShapeDtypeStruct((m, D_MODEL), F32),
        compiler_params=_cparams(("parallel",)),
        name="mm_res_ln",
    )(*acts, *weights, x, g, b)


def _mem_prompt_kernel(x_ref, mk_ref, mv_ref, wq_ref, wo_ref, g_ref, b_ref, o_ref):
    x = x_ref[...]
    q = (_dot(x, wq_ref[...]) * (MEM_HD ** -0.5)).astype(BF16)
    outs = []
    for h in range(MEM_HEADS):
        sl = slice(h * MEM_HD, (h + 1) * MEM_HD)
        s = lax.dot_general(q[:, sl], mk_ref[:, sl], (((1,), (1,)), ((), ())), preferred_element_type=F32)
        p = jnp.exp(s - jnp.max(s, axis=1, keepdims=True))
        p = p * (1.0 / jnp.sum(p, axis=1, keepdims=True))
        outs.append(jnp.dot(p.astype(BF16), mv_ref[:, sl], preferred_element_type=F32))
    o = jnp.concatenate(outs, axis=1)
    o_ref[...] = _layer_norm(ALPHA * x + _dot(o, wo_ref[...]), g_ref[...], b_ref[...])


def _mem_attn_prompt(x, mkb, mvb, wq, wo, g, b, *, batch, seq, tm):
    n_mem = mkb.shape[0] // batch
    nt = seq // tm
    xblk = pl.BlockSpec((tm, D_MODEL), lambda bb, i: (bb * nt + i, 0))
    mblk = pl.BlockSpec((None, n_mem, D_MODEL), lambda bb, i: (bb, 0, 0))
    return pl.pallas_call(
        _mem_prompt_kernel,
        grid=(batch, nt),
        in_specs=[xblk, mblk, mblk, _const_spec(wq.shape), _const_spec(wo.shape),
                  _const_spec(g.shape), _const_spec(b.shape)],
        out_specs=xblk,
        out_shape=jax.ShapeDtypeStruct((batch * seq, D_MODEL), F32),
        compiler_params=_cparams(("parallel", "parallel")),
        name="mem_attn_prompt",
    )(x, mkb.reshape(batch, n_mem, D_MODEL), mvb.reshape(batch, n_mem, D_MODEL), wq, wo, g, b)


def _mem_sample_kernel(q_ref, mk_ref, mv_ref, o_ref, *, group):
    for i in range(group):
        q = q_ref[i]
        s = jnp.sum(mk_ref[i] * q, axis=2, keepdims=True)
        p = jnp.exp(s - jnp.max(s, axis=0, keepdims=True))
        p = p * (1.0 / jnp.sum(p, axis=0, keepdims=True))
        o_ref[i] = jnp.sum(mv_ref[i] * p, axis=0)


def _mem_attn_sample(q, mem_k_all, mem_v_all, *, layer, group):
    bs = q.shape[0]
    n_mem = mem_k_all.shape[1]
    nb = bs // group
    qblk = pl.BlockSpec((group, MEM_HEADS, MEM_HD), lambda i: (i, 0, 0))
    mblk = pl.BlockSpec((group, n_mem, MEM_HEADS, MEM_HD), lambda i: (layer * nb + i, 0, 0, 0))
    out = pl.pallas_call(
        functools.partial(_mem_sample_kernel, group=group),
        grid=(nb,),
        in_specs=[qblk, mblk, mblk],
        out_specs=qblk,
        out_shape=jax.ShapeDtypeStruct((bs, MEM_HEADS, MEM_HD), F32),
        compiler_params=_cparams(("parallel",)),
        name="mem_attn_sample",
    )(q.reshape(bs, MEM_HEADS, MEM_HD), mem_k_all, mem_v_all)
    return out.reshape(bs, D_MODEL)


def _ffn_kernel(x_ref, w1_ref, w3_ref, w2_ref, g_ref, b_ref, o_ref, *, n_chunks):
    x = x_ref[...]
    xb = x.astype(BF16)
    acc = jnp.zeros(x.shape, F32)
    for c in range(n_chunks):
        h = _silu(jnp.dot(xb, w1_ref[c], preferred_element_type=F32)) * jnp.dot(xb, w3_ref[c],
                                                                                 preferred_element_type=F32)
        acc = acc + jnp.dot(h.astype(BF16), w2_ref[c], preferred_element_type=F32)
    o_ref[...] = _layer_norm(ALPHA * x + acc, g_ref[...], b_ref[...])


def _ffn(x, w1c, w3c, w2c, g, b, *, tm):
    m = x.shape[0]
    row = pl.BlockSpec((tm, D_MODEL), lambda i: (i, 0))
    wspec = lambda w: pl.BlockSpec(w.shape, lambda i: (0, 0, 0), pipeline_mode=pl.Buffered(1))
    return pl.pallas_call(
        functools.partial(_ffn_kernel, n_chunks=w1c.shape[0]),
        grid=(m // tm,),
        in_specs=[row, wspec(w1c), wspec(w3c), wspec(w2c), _const_spec(g.shape), _const_spec(b.shape)],
        out_specs=row,
        out_shape=jax.ShapeDtypeStruct(x.shape, F32),
        compiler_params=_cparams(("parallel",), vmem_mb=56),
        name="ffn_dense",
    )(x, w1c, w3c, w2c, g, b)


def _router_kernel(x_ref, w_ref, b_ref, o_ref):
    logits = jnp.dot(x_ref[...], w_ref[...], precision=lax.Precision.HIGHEST, preferred_element_type=F32)
    lane = lax.broadcasted_iota(jnp.int32, logits.shape, 1)
    logits = jnp.where(lane < N_EXPERTS, logits + b_ref[...], NEG_INF)
    m1 = jnp.max(logits, axis=1, keepdims=True)
    i1 = jnp.min(jnp.where(logits == m1, lane, LANES), axis=1, keepdims=True)
    rest = jnp.where(lane == i1, NEG_INF, logits)
    m2 = jnp.max(rest, axis=1, keepdims=True)
    i2 = jnp.min(jnp.where(rest == m2, lane, LANES), axis=1, keepdims=True)
    e = jnp.exp(m2 - m1)
    g1 = 1.0 / (1.0 + e)
    g2 = e * g1
    out = jnp.where(lane == 0, i1.astype(F32), jnp.where(lane == 1, i2.astype(F32),
                    jnp.where(lane == 2, g1, jnp.where(lane == 3, g2, 0.0))))
    o_ref[...] = out


def _router(x, w_pad, b_pad, *, tm):
    m = x.shape[0]
    return pl.pallas_call(
        _router_kernel,
        grid=(m // tm,),
        in_specs=[pl.BlockSpec((tm, D_MODEL), lambda i: (i, 0)), _const_spec(w_pad.shape), _const_spec(b_pad.shape)],
        out_specs=pl.BlockSpec((tm, LANES), lambda i: (i, 0)),
        out_shape=jax.ShapeDtypeStruct((m, LANES), F32),
        compiler_params=_cparams(("parallel",)),
        name="moe_router",
    )(x, w_pad, b_pad)


def _expert_kernel(te_ref, tr_ref, x_ref, w1_ref, w3_ref, w2_ref, o_ref, acc_sc, *, sub):
    i = pl.program_id(0)
    j = pl.program_id(1)
    rows = tr_ref[i]
    tm = x_ref.shape[0]

    @pl.when(j == 0)
    def _():
        acc_sc[...] = jnp.zeros_like(acc_sc)

    def swiglu_into_acc(sl):
        xb = x_ref[sl, :].astype(BF16)
        w1 = w1_ref[...].astype(BF16)
        w3 = w3_ref[...].astype(BF16)
        h = _silu(jnp.dot(xb, w1, preferred_element_type=F32)) * jnp.dot(xb, w3, preferred_element_type=F32)
        acc_sc[sl, :] += jnp.dot(h.astype(BF16), w2_ref[...].astype(BF16), preferred_element_type=F32)

    @pl.when(rows == tm)
    def _():
        swiglu_into_acc(slice(None))

    @pl.when((rows > 0) & (rows < tm))
    def _():
        for s in range(tm // sub):
            pl.when(rows > s * sub)(functools.partial(swiglu_into_acc, slice(s * sub, (s + 1) * sub)))

    @pl.when(j == pl.num_programs(1) - 1)
    def _():
        o_ref[...] = acc_sc[...]


def _experts(x_sorted, tile_expert, tile_rows, w1, w3, w2, *, tm, sub, fc):
    n_slots = x_sorted.shape[0]
    nj = w1.shape[2] // fc
    last = nj - 1

    def jj(i, j, tr):
        return jnp.where(tr[i] > 0, j, last)

    xs = pl.BlockSpec((tm, D_MODEL), lambda i, j, te, tr: (i, 0))
    w13 = pl.BlockSpec((None, D_MODEL, fc), lambda i, j, te, tr: (te[i], 0, jj(i, j, tr)))
    w2s = pl.BlockSpec((None, fc, D_MODEL), lambda i, j, te, tr: (te[i], jj(i, j, tr), 0))
    return pl.pallas_call(
        functools.partial(_expert_kernel, sub=sub),
        grid_spec=pltpu.PrefetchScalarGridSpec(
            num_scalar_prefetch=2, grid=(n_slots // tm, nj),
            in_specs=[xs, w13, w13, w2s], out_specs=xs,
            scratch_shapes=[pltpu.VMEM((tm, D_MODEL), F32)]),
        out_shape=jax.ShapeDtypeStruct((n_slots, D_MODEL), F32),
        compiler_params=_cparams(("arbitrary", "arbitrary"), vmem_mb=56),
        name="moe_experts",
    )(tile_expert, tile_rows, x_sorted, w1, w3, w2)


def _combine_ln_kernel(x_ref, y_ref, r_ref, g_ref, b_ref, o_ref):
    r = r_ref[...]
    f = r[:, 2:3] * y_ref[:, :D_MODEL] + r[:, 3:4] * y_ref[:, D_MODEL:]
    o_ref[...] = _layer_norm(ALPHA * x_ref[...] + f, g_ref[...], b_ref[...])


def _combine_ln(x, y2, route, g, b, *, tm, rows, row0=0):
    row = lambda w: pl.BlockSpec((tm, w), lambda i: (row0 + i, 0))
    return pl.pallas_call(
        _combine_ln_kernel,
        grid=(rows // tm,),
        in_specs=[row(D_MODEL), row(2 * D_MODEL), row(LANES), _const_spec(g.shape), _const_spec(b.shape)],
        out_specs=pl.BlockSpec((tm, D_MODEL), lambda i: (i, 0)),
        out_shape=jax.ShapeDtypeStruct((rows, D_MODEL), F32),
        compiler_params=_cparams(("parallel",)),
        name="moe_combine_ln",
    )(x, y2, route, g, b)


def _moe(x_all, router_w, router_b, w1, w3, w2, g, b, *, tm_e, sub, fc, split):
    t = x_all.shape[0]
    w_pad = jnp.zeros((D_MODEL, LANES), F32).at[:, :N_EXPERTS].set(router_w)
    b_pad = jnp.zeros((1, LANES), F32).at[0, :N_EXPERTS].set(router_b)
    route = _router(x_all, w_pad, b_pad, tm=128)
    top_i = route[:, :2].astype(jnp.int32).reshape(-1)
    onehot = (top_i[:, None] == jnp.arange(N_EXPERTS)[None, :]).astype(jnp.int32)
    csum = jnp.cumsum(onehot, axis=0)
    counts = csum[-1]
    rank = jnp.sum((csum - onehot) * onehot, axis=1)
    padded = ((counts + tm_e - 1) // tm_e) * tm_e
    ends = jnp.cumsum(padded)
    starts = ends - padded
    slot = starts[top_i] + rank
    n_tiles = (2 * t + N_EXPERTS * (tm_e - 1)) // tm_e + 1
    n_slots = n_tiles * tm_e
    src = jnp.zeros((n_slots,), jnp.int32).at[slot].set(jnp.arange(2 * t, dtype=jnp.int32) // 2)
    tile_start = jnp.arange(n_tiles, dtype=jnp.int32) * tm_e
    tile_expert = jnp.sum((tile_start[:, None] >= ends[None, :]).astype(jnp.int32), axis=1)
    tile_expert = jnp.minimum(tile_expert, N_EXPERTS - 1)
    tile_rows = jnp.clip((starts + counts)[tile_expert] - tile_start, 0, tm_e)
    tile_rows = jnp.where(tile_start < ends[-1], tile_rows, 0).astype(jnp.int32)
    last_e = tile_expert[jnp.maximum(ends[-1] // tm_e - 1, 0)]
    tile_expert = jnp.where(tile_rows > 0, tile_expert, last_e).astype(jnp.int32)
    x_sorted = jnp.take(x_all, src, axis=0, mode="clip")
    y_sorted = _experts(x_sorted, tile_expert, tile_rows, w1, w3, w2, tm=tm_e, sub=sub, fc=fc)
    y2 = jnp.take(y_sorted, slot, axis=0, mode="clip").reshape(t, 2 * D_MODEL)
    tm = 128
    return (_combine_ln(x_all, y2, route, g, b, tm=tm, rows=split),
            _combine_ln(x_all, y2, route, g, b, tm=tm, rows=t - split, row0=split // tm))


def _pack_w_in(w_in_t):
    q, k, v, z, xbc, dt, pu = jnp.split(w_in_t, np.cumsum([512, 512, 512, 256, 768, 4]).tolist(), axis=0)
    pad = jnp.zeros((DT_PAD - SSM_HEADS, D_MODEL), w_in_t.dtype)
    return jnp.concatenate([q, k, v, z, xbc, pu, dt, pad], axis=0).astype(BF16)


def _block_diag(pool_w):
    out = jnp.zeros((POOL_WIDTH, POOL_WIDTH), F32)
    for g in range(len(POOL_WINDOWS)):
        out = out.at[g * POOL_GDIM:(g + 1) * POOL_GDIM, g * POOL_GDIM:(g + 1) * POOL_GDIM].set(pool_w[g])
    return out.astype(BF16)


def _pad_heads(v):
    return jnp.zeros((1, DT_PAD), F32).at[0, :SSM_HEADS].set(v)


def kernel(x_prompt, x_sample, cache_k, cache_v, cache_mem_k, cache_mem_v, state_ssm, state_conv, state_pool, page_table, mem_prompt, w_in, lam_q1, lam_k1, lam_q2, lam_k2, da_norm_w, conv_w, conv_b, dt_bias, a_log, d_skip, ssm_norm_w, pool_w, pool_scale, w_out, ln1_g, ln1_b, wq_mem, wk_mem, wv_mem, wo_mem, ln2_g, ln2_b, ffn_w1, ffn_w3, ffn_w2, moe_router, moe_router_b, moe_w1, moe_w3, moe_w2, ln3_g, ln3_b):
    bp, seq, _ = x_prompt.shape
    bs = x_sample.shape[0]
    n_pages = page_table.shape[1]
    past = n_pages * PAGE_SIZE
    n_mem = mem_prompt.shape[1]
    tp_rows = bp * seq
    TM = 512

    xp = x_prompt.reshape(tp_rows, D_MODEL)
    xs = x_sample.reshape(bs, D_MODEL)
    tab_p = _rope_tables(jnp.arange(seq))
    tab_s = _rope_tables(jnp.full((1,), past))
    cache_k4 = jnp.transpose(cache_k, (0, 1, 3, 4, 5, 2)).reshape(DEPTH, -1, DA_WIDTH, PAGE_SIZE)
    cache_v4 = cache_v.reshape(DEPTH, -1, PAGE_SIZE * DA_HEADS, DA_DV)
    mem_k_all = cache_mem_k.reshape(DEPTH * bs, n_mem, MEM_HEADS, MEM_HD)
    mem_v_all = cache_mem_v.reshape(DEPTH * bs, n_mem, MEM_HEADS, MEM_HD)
    state_all = state_ssm.reshape(DEPTH * bs, SSM_WIDTH, SSM_STATE)
    conv_hist = jnp.transpose(state_conv, (0, 2, 1, 3))
    pool_hist = jnp.transpose(state_pool, (0, 2, 1, 3))
    w_in_t = jnp.transpose(w_in, (2, 0, 1))
    mem_flat = mem_prompt.reshape(bp * n_mem, D_MODEL)
    row = lambda v: v.reshape(1, -1)

    outs = {k: [] for k in ("kp", "vp", "ks", "vs", "mk", "mv", "hp", "hs", "cp", "cs", "pp", "ps")}
    for l in range(DEPTH):
        lam_init = 0.8 - 0.6 * math.exp(-0.3 * l)
        w_cat = _pack_w_in(w_in_t[:, l, :])
        lam_vecs = (row(lam_q1[l]), row(lam_k1[l]), row(lam_q2[l]), row(lam_k2[l]))
        nw = row(da_norm_w[l])
        dtb, alog = _pad_heads(dt_bias[l]), _pad_heads(a_log[l])
        dsk = row(jnp.repeat(d_skip[l], SSM_HEADDIM))
        snw = row(ssm_norm_w[l])
        pwbd = _block_diag(pool_w[l])
        psc = row(pool_scale[l])
        wo_b = w_out[l].astype(BF16)
        wo_parts = (wo_b[:DA_WIDTH], wo_b[DA_WIDTH:DA_WIDTH + SSM_WIDTH], wo_b[DA_WIDTH + SSM_WIDTH:])
        g1, b1, g2, b2, g3, b3 = (row(ln1_g[l]), row(ln1_b[l]), row(ln2_g[l]), row(ln2_b[l]),
                                  row(ln3_g[l]), row(ln3_b[l]))

        qp, kb, vb, kpg, vrow, zp, xbcp, pup, dtp = _proj_in(xp, w_cat, tab_p, prompt=True, tm=TM, seq=seq,
                                                             rows=tp_rows)
        oa_p = _flash_attn(qp, kb, vb, lam_vecs, nw.reshape(DA_DV, 1), batch=bp, seq=seq, tq=512, tk=512,
                           lam_init=lam_init)
        y_p, h_p = _ssd_prompt(xbcp, zp, dtp, conv_w[l], row(conv_b[l]), dtb, alog, dsk, snw, batch=bp, seq=seq)
        yp_p = _pool_prompt(pup, pwbd, psc, batch=bp, seq=seq, tp=512)
        xp = _mm_res_ln((oa_p, y_p, yp_p), wo_parts, xp, g1, b1, tm=TM)

        q_s, k_s, v_s, z_s, xbc_s, pu_s, dt_s = _proj_in(xs, w_cat, tab_s, prompt=False, tm=bs, seq=1, rows=bs)
        oa_s = _paged_attn(q_s, k_s, v_s, cache_k4, cache_v4, page_table, lam_vecs, nw, layer=l, lam_init=lam_init)
        xact, dt_a, dec, yp_s = _sample_prep(xbc_s, conv_hist[l], conv_w[l], row(conv_b[l]), dt_s, dtb, alog,
                                             pu_s, pool_hist[l], pwbd, psc, pos0=past)
        y_s, h_s = _ssd_step(state_all, l, xact, z_s, dt_a, dec, dsk, snw)
        xs = _mm_res_ln((oa_s, y_s, yp_s), wo_parts, xs, g1, b1, tm=bs)

        wq_b, wk_b, wv_b, wom_b = (wq_mem[l].astype(BF16), wk_mem[l].astype(BF16), wv_mem[l].astype(BF16),
                                   wo_mem[l].astype(BF16))
        mk, mkb = _matmul(mem_flat, wk_b, tm=bp * n_mem // 2, dtypes=(F32, BF16))
        mv, mvb = _matmul(mem_flat, wv_b, tm=bp * n_mem // 2, dtypes=(F32, BF16))
        xp = _mem_attn_prompt(xp, mkb, mvb, wq_b, wom_b, g2, b2, batch=bp, seq=seq, tm=TM)
        (qm_s,) = _matmul(xs, wq_b, tm=bs, scale=MEM_HD ** -0.5)
        om_s = _mem_attn_sample(qm_s, mem_k_all, mem_v_all, layer=l, group=4)
        xs = _mm_res_ln((om_s,), (wom_b,), xs, g2, b2, tm=bs)

        j = l // 2
        if l % 2 == 0:
            f = ffn_w1.shape[2]
            fc = 256
            w1c = ffn_w1[j].astype(BF16).reshape(D_MODEL, f // fc, fc).transpose(1, 0, 2)
            w3c = ffn_w3[j].astype(BF16).reshape(D_MODEL, f // fc, fc).transpose(1, 0, 2)
            w2c = ffn_w2[j].astype(BF16).reshape(f // fc, fc, D_MODEL)
            xp = _ffn(xp, w1c, w3c, w2c, g3, b3, tm=TM)
            xs = _ffn(xs, w1c, w3c, w2c, g3, b3, tm=bs)
        else:
            x_all = jnp.concatenate([xp, xs], axis=0)
            xp, xs = _moe(x_all, moe_router[j], moe_router_b[j], moe_w1[j], moe_w3[j], moe_w2[j], g3, b3,
                          tm_e=1024, sub=256, fc=512, split=tp_rows)

        outs["kp"].append(jnp.transpose(kpg.reshape(bp, seq // PAGE_SIZE, DA_HEADS, 2, DA_DK, PAGE_SIZE),
                                        (0, 1, 5, 2, 3, 4)))
        outs["vp"].append(vrow.reshape(bp, seq // PAGE_SIZE, PAGE_SIZE, DA_HEADS, DA_DV))
        outs["ks"].append(k_s.reshape(bs, 1, DA_HEADS, 2, DA_DK))
        outs["vs"].append(v_s.reshape(bs, 1, DA_HEADS, DA_DV))
        outs["mk"].append(mk.reshape(bp, n_mem, MEM_HEADS, MEM_HD))
        outs["mv"].append(mv.reshape(bp, n_mem, MEM_HEADS, MEM_HD))
        outs["hp"].append(h_p)
        outs["hs"].append(h_s)
        outs["cp"].append(xbcp.reshape(bp, seq, CONV_CH)[:, seq - (SSM_CONV - 1):])
        outs["cs"].append(jnp.transpose(jnp.concatenate([conv_hist[l, 1:], xbc_s[None]], axis=0), (1, 0, 2)))
        outs["pp"].append(pup.reshape(bp, seq, POOL_WIDTH)[:, seq - POOL_BUF:])
        outs["ps"].append(jnp.transpose(jnp.concatenate([pool_hist[l, 1:], pu_s[None]], axis=0), (1, 0, 2)))

    st = lambda k: jnp.stack(outs[k])
    return (xp.reshape(bp, seq, D_MODEL), xs.reshape(bs, 1, D_MODEL), st("kp"), st("vp"), st("ks"), st("vs"),
            st("mk"), st("mv"), st("hp"), st("hs"), st("cp"), st("cs"), st("pp"), st("ps"))
```

```python
import functools
import math

import jax
import jax.numpy as jnp
import numpy as np
from jax import lax
from jax.experimental import pallas as pl
from jax.experimental.pallas import tpu as pltpu

F32 = jnp.float32
BF16 = jnp.bfloat16

D_MODEL = 1024
DEPTH = 2
PAGE_SIZE = 128
DA_HEADS = 4
DA_WIDTH = 512
DA_DV = 128
DA_DK = 64
ROT_DIM = 16
ROPE_THETA = 500000.0
SSM_WIDTH = 256
SSM_HEADS = 4
SSM_HEADDIM = 64
SSM_STATE = 128
SSM_CONV = 4
SSM_CHUNK = 128
CONV_CH = 768
POOL_WIDTH = 256
POOL_WINDOWS = (2, 4, 8, 16)
POOL_GDIM = 64
POOL_BUF = 15
MEM_HEADS = 4
MEM_HD = 256
N_EXPERTS = 8
ALPHA = (2.0 * DEPTH) ** 0.25
LN_EPS = 1e-5
RMS_EPS = 1e-6

LANES = 128
DT_PAD = LANES
_C_Q, _C_K, _C_V, _C_Z, _C_X, _C_P, _C_DT, _C_END = 0, 512, 1024, 1536, 1792, 2560, 2816, 2944

NEG_INF = float("-inf")
Q_SCALE = DA_DK ** -0.5 * math.log2(math.e)


def _cparams(sem, vmem_mb=48):
    return pltpu.CompilerParams(dimension_semantics=sem, vmem_limit_bytes=vmem_mb * 1024 * 1024)


def _dot(a, b):
    return jnp.dot(a.astype(BF16), b.astype(BF16), preferred_element_type=F32)


def _dot_nt(a, b):
    return lax.dot_general(a.astype(BF16), b.astype(BF16), (((1,), (1,)), ((), ())),
                           preferred_element_type=F32)


def _layer_norm(x, g, b):
    mu = jnp.mean(x, axis=-1, keepdims=True)
    xc = x - mu
    var = jnp.mean(xc * xc, axis=-1, keepdims=True)
    return xc * lax.rsqrt(var + LN_EPS) * g + b


def _silu(x):
    return x * (1.0 / (1.0 + jnp.exp(-x)))


def _softplus(x):
    return jnp.maximum(x, 0.0) + jnp.log(1.0 + jnp.exp(-jnp.abs(x)))


def _const_spec(shape):
    nd = len(shape)
    return pl.BlockSpec(shape, lambda *_: (0,) * nd)


def _rope_tables(pos):
    half = ROT_DIM // 2
    inv = ROPE_THETA ** (-jnp.arange(half, dtype=F32) * 2.0 / ROT_DIM)
    ang = pos.astype(F32)[:, None] * inv[None, :]
    cos, sin = jnp.cos(ang), jnp.sin(ang)
    n = pos.shape[0]
    c64 = jnp.concatenate([cos, cos, jnp.ones((n, DA_DK - ROT_DIM), F32)], axis=1)
    s1 = jnp.concatenate([-sin, jnp.zeros((n, DA_DK - half), F32)], axis=1)
    s2 = jnp.concatenate([jnp.zeros((n, half), F32), sin, jnp.zeros((n, DA_DK - ROT_DIM), F32)], axis=1)
    two = lambda t: jnp.concatenate([t, t], axis=1)
    return two(c64), two(s1), two(s2)


def _proj_in_kernel(x_ref, w_ref, c_ref, s1_ref, s2_ref, *out_refs, prompt):
    xb = x_ref[...].astype(BF16)
    cc, s1, s2 = c_ref[...], s1_ref[...], s2_ref[...]

    def rope(t):
        outs = []
        for h in range(DA_HEADS):
            th = t[:, h * LANES:(h + 1) * LANES]
            outs.append(th * cc + pltpu.roll(th, LANES - ROT_DIM // 2, 1) * s1
                        + pltpu.roll(th, ROT_DIM // 2, 1) * s2)
        return jnp.concatenate(outs, axis=1)

    def seg(a, b):
        return lax.dot_general(xb, w_ref[a:b, :], (((1,), (1,)), ((), ())), preferred_element_type=F32)

    q = rope(seg(_C_Q, _C_K)) * Q_SCALE
    k = rope(seg(_C_K, _C_V))
    v = seg(_C_V, _C_Z)
    if prompt:
        q_ref, kb_ref, vb_ref, kpg_ref, vrow_ref, z_ref, xbc_ref, pu_ref, dt_ref = out_refs
        kb_ref[...] = k.astype(BF16)
        vb_ref[...] = v.astype(BF16)
        for pg in range(kpg_ref.shape[0]):
            kpg_ref[pg] = k[pg * PAGE_SIZE:(pg + 1) * PAGE_SIZE, :].T
        for h in range(DA_HEADS):
            vrow_ref[pl.ds(h, v.shape[0], stride=DA_HEADS), :] = v[:, h * DA_DV:(h + 1) * DA_DV]
    else:
        q_ref, k_ref, v_ref, z_ref, xbc_ref, pu_ref, dt_ref = out_refs
        k_ref[...] = k
        v_ref[...] = v
    q_ref[...] = q
    z_ref[...] = seg(_C_Z, _C_X)
    xbc_ref[...] = seg(_C_X, _C_P)
    pu_ref[...] = seg(_C_P, _C_DT)
    dt_ref[...] = seg(_C_DT, _C_END)


def _proj_in(x, w_cat, tables, *, prompt, tm, seq, rows, row0=0):
    m = rows
    grid = (m // tm,)
    xrow = pl.BlockSpec((tm, D_MODEL), lambda i: (row0 + i, 0))

    def rows_out(w, dt):
        return pl.BlockSpec((tm, w), lambda i: (i, 0)), jax.ShapeDtypeStruct((m, w), dt)

    if prompt:
        npos = seq // tm
        ppt = tm // PAGE_SIZE
        tab = pl.BlockSpec((tm, LANES), lambda i: (i % npos, 0))
        outs = [rows_out(DA_WIDTH, F32), rows_out(DA_WIDTH, BF16), rows_out(DA_WIDTH, BF16),
                (pl.BlockSpec((ppt, DA_WIDTH, PAGE_SIZE), lambda i: (i, 0, 0)),
                 jax.ShapeDtypeStruct((m // PAGE_SIZE, DA_WIDTH, PAGE_SIZE), F32)),
                (pl.BlockSpec((tm * DA_HEADS, DA_DV), lambda i: (i, 0)),
                 jax.ShapeDtypeStruct((m * DA_HEADS, DA_DV), F32))]
    else:
        tab = _const_spec((1, LANES))
        outs = [rows_out(DA_WIDTH, F32), rows_out(DA_WIDTH, F32), rows_out(DA_WIDTH, F32)]
    outs += [rows_out(SSM_WIDTH, F32), rows_out(CONV_CH, F32), rows_out(POOL_WIDTH, F32), rows_out(DT_PAD, F32)]
    return pl.pallas_call(
        functools.partial(_proj_in_kernel, prompt=prompt),
        grid=grid,
        in_specs=[xrow, _const_spec(w_cat.shape), tab, tab, tab],
        out_specs=[o[0] for o in outs],
        out_shape=[o[1] for o in outs],
        compiler_params=_cparams(("parallel",)),
        name="proj_in",
    )(x, w_cat, *tables)


def _lam_value(lq1, lk1, lq2, lk2, lam_init):
    return (jnp.exp(jnp.sum(lq1 * lk1, axis=1, keepdims=True))
            - jnp.exp(jnp.sum(lq2 * lk2, axis=1, keepdims=True)) + lam_init)


def _flash_kernel(q_ref, k_ref, v_ref, lq1_ref, lk1_ref, lq2_ref, lk2_ref, nw_ref, o_ref,
                  vt_sc, s_sc, p_sc, m_sc, acc_sc, *, tq, tk, lam_init):
    i = pl.program_id(2)
    n_kt, aug, _ = vt_sc.shape

    @pl.when(i == 0)
    def _():
        ones = jnp.ones((aug - DA_DV, tk), BF16)
        for j in range(n_kt):
            vt = v_ref[j * tk:(j + 1) * tk, :].astype(F32).T.astype(BF16)
            vt_sc[j] = jnp.concatenate([vt, ones], axis=0)

    qt = q_ref[...].T
    drow = lax.broadcasted_iota(jnp.int32, qt.shape, 0)
    qts = (jnp.where(drow < DA_DK, qt, 0.0).astype(BF16), jnp.where(drow >= DA_DK, qt, 0.0).astype(BF16))

    def stage(j, cur, mask=None, last=False):
        nxt = 1 - cur
        if not last:
            kt = k_ref[pl.ds(pl.multiple_of((j + 1) * tk, tk), tk), :]
            for c in range(2):
                s_sc[nxt, c] = jnp.dot(kt, qts[c], preferred_element_type=F32)
        vta = vt_sc[jnp.maximum(j - 1, 0)]
        for c in range(2):
            pv = jnp.dot(vta, p_sc[cur, c], preferred_element_type=F32)
            s = s_sc[cur, c]
            if mask is not None:
                s = jnp.where(mask, s, NEG_INF)
            m = m_sc[c]
            m_new = jnp.maximum(m, jnp.max(s, axis=0, keepdims=True))
            p_sc[nxt, c] = jnp.exp2(s - m_new).astype(BF16)
            acc_sc[c] = jnp.exp2(m - m_new) * (acc_sc[c] + pv)
            m_sc[c] = m_new

    kt0 = k_ref[0:tk, :]
    for c in range(2):
        s_sc[0, c] = jnp.dot(kt0, qts[c], preferred_element_type=F32)
        p_sc[0, c] = jnp.zeros((tk, tq), BF16)
        m_sc[c] = jnp.full((1, tq), NEG_INF, F32)
        acc_sc[c] = jnp.zeros((aug, tq), F32)

    def pair(t, carry):
        stage(2 * t, 0)
        stage(2 * t + 1, 1)
        return carry

    lax.fori_loop(0, i // 2, pair, 0)
    odd = i % 2 == 1

    @pl.when(odd)
    def _():
        stage(i - 1, 0)

    krow = lax.broadcasted_iota(jnp.int32, (tk, tq), 0)
    qcol = lax.broadcasted_iota(jnp.int32, (tk, tq), 1)

    def finish(cur):
        stage(i, cur, mask=krow <= qcol, last=True)
        vta = vt_sc[i]
        a0, a1 = (acc_sc[c] + jnp.dot(vta, p_sc[1 - cur, c], preferred_element_type=F32) for c in range(2))
        lam = _lam_value(lq1_ref[...], lk1_ref[...], lq2_ref[...], lk2_ref[...], lam_init)
        o = a0[:DA_DV] * (1.0 / a0[DA_DV:DA_DV + 1]) - lam * (a1[:DA_DV] * (1.0 / a1[DA_DV:DA_DV + 1]))
        ms = jnp.mean(o * o, axis=0, keepdims=True)
        o = o * lax.rsqrt(ms + RMS_EPS) * nw_ref[...] * (1.0 - lam_init)
        o_ref[...] = o.T

    pl.when(odd)(lambda: finish(1))
    pl.when(jnp.logical_not(odd))(lambda: finish(0))


def _flash_attn(q, kb, v, lam_vecs, norm_w_col, *, batch, seq, tq, tk, lam_init):
    assert tq == tk
    q3 = q.reshape(batch, seq, DA_WIDTH)
    k3 = kb.reshape(batch, seq, DA_WIDTH)
    v3 = v.reshape(batch, seq, DA_WIDTH)
    qspec = pl.BlockSpec((None, tq, DA_DV), lambda b, h, i: (b, i, h))
    kvspec = pl.BlockSpec((None, seq, DA_DV), lambda b, h, i: (b, 0, h))
    vec = _const_spec((1, DA_DK))
    aug = DA_DV + 16
    out = pl.pallas_call(
        functools.partial(_flash_kernel, tq=tq, tk=tk, lam_init=lam_init),
        grid=(batch, DA_HEADS, seq // tq),
        in_specs=[qspec, kvspec, kvspec, vec, vec, vec, vec, _const_spec((DA_DV, 1))],
        out_specs=qspec,
        out_shape=jax.ShapeDtypeStruct((batch, seq, DA_WIDTH), F32),
        scratch_shapes=[pltpu.VMEM((seq // tk, aug, tk), BF16), pltpu.VMEM((2, 2, tk, tq), F32),
                        pltpu.VMEM((2, 2, tk, tq), BF16), pltpu.VMEM((2, 1, tq), F32),
                        pltpu.VMEM((2, aug, tq), F32)],
        compiler_params=_cparams(("parallel", "parallel", "arbitrary")),
        name="flash_diff_attn",
    )(q3, k3, v3, *lam_vecs, norm_w_col)
    return out.reshape(batch * seq, DA_WIDTH)


def _paged_kernel(pt_ref, q_ref, ks_ref, vs_ref, lq1_ref, lk1_ref, lq2_ref, lk2_ref, nw_ref, *rest,
                  n_pages, lam_init):
    k_refs = rest[:n_pages]
    v_refs = rest[n_pages:2 * n_pages]
    o_ref = rest[2 * n_pages]
    q = q_ref[...]
    nhc = 2 * DA_HEADS
    jj = lax.broadcasted_iota(jnp.int32, (2 * nhc, DA_WIDTH), 0)
    ll = lax.broadcasted_iota(jnp.int32, (2 * nhc, DA_WIDTH), 1)
    qblk = jnp.where((jj < nhc) & ((ll >> 6) == (jj & (DA_HEADS - 1)) * 2 + (jj >> 2)), q, 0.0)
    qb = qblk.astype(BF16)

    s_all = jnp.concatenate(
        [jnp.dot(qb, k_refs[j][...].astype(BF16), preferred_element_type=F32)[:nhc] for j in range(n_pages)],
        axis=1)
    s_self = jnp.sum(qblk[:nhc] * ks_ref[...], axis=1, keepdims=True)
    m = jnp.maximum(jnp.max(s_all, axis=1, keepdims=True), s_self)
    p = jnp.exp2(s_all - m)
    p_self = jnp.exp2(s_self - m)
    inv_l = 1.0 / (jnp.sum(p, axis=1, keepdims=True) + p_self)
    lam = _lam_value(lq1_ref[...], lk1_ref[...], lq2_ref[...], lk2_ref[...], lam_init)
    pn = p * inv_l
    pn_self = p_self * inv_l
    w = pn - lam * pltpu.roll(pn, DA_HEADS, 0)
    w_self = pn_self - lam * pltpu.roll(pn_self, DA_HEADS, 0)
    w16 = jnp.concatenate([w, jnp.zeros_like(w)], axis=0).astype(BF16)
    accs = [jnp.zeros((2 * nhc, DA_DV), F32) for _ in range(DA_HEADS)]
    for j in range(n_pages):
        wj = w16[:, j * PAGE_SIZE:(j + 1) * PAGE_SIZE]
        for h in range(DA_HEADS):
            vh = v_refs[j][pl.ds(h, PAGE_SIZE, stride=DA_HEADS), :]
            accs[h] = accs[h] + jnp.dot(wj, vh.astype(BF16), preferred_element_type=F32)
    outs = []
    for h in range(DA_HEADS):
        o = accs[h][h:h + 1, :] + w_self[h:h + 1, :] * vs_ref[:, h * DA_DV:(h + 1) * DA_DV]
        ms = jnp.mean(o * o, axis=1, keepdims=True)
        outs.append(o * lax.rsqrt(ms + RMS_EPS) * nw_ref[...] * (1.0 - lam_init))
    o_ref[...] = jnp.concatenate(outs, axis=1)


def _paged_attn(q, k_self, v_self, cache_k4, cache_v4, page_table, lam_vecs, norm_w4, *, layer, lam_init):
    bs, n_pages = page_table.shape
    pt = page_table.reshape(-1)
    one = pl.BlockSpec((None, 1, DA_WIDTH), lambda b, pt: (b, 0, 0))
    vec = pl.BlockSpec((1, DA_DK), lambda b, pt: (0, 0))

    def page_spec(j):
        return pl.BlockSpec((None, None, DA_WIDTH, PAGE_SIZE), lambda b, pt: (layer, pt[b * n_pages + j], 0, 0))

    specs = [one, one, one, vec, vec, vec, vec, pl.BlockSpec((1, DA_DV), lambda b, pt: (0, 0))]
    specs += [page_spec(j) for j in range(n_pages)] * 2
    out = pl.pallas_call(
        functools.partial(_paged_kernel, n_pages=n_pages, lam_init=lam_init),
        grid_spec=pltpu.PrefetchScalarGridSpec(
            num_scalar_prefetch=1, grid=(bs,), in_specs=specs, out_specs=one),
        out_shape=jax.ShapeDtypeStruct((bs, 1, DA_WIDTH), F32),
        compiler_params=_cparams(("arbitrary",)),
        name="paged_diff_attn",
    )(pt, q.reshape(bs, 1, DA_WIDTH), k_self.reshape(bs, 1, DA_WIDTH), v_self.reshape(bs, 1, DA_WIDTH),
      *lam_vecs, norm_w4, *([cache_k4] * n_pages), *([cache_v4] * n_pages))
    return out.reshape(bs, DA_WIDTH)


def _head_expand(cols, width, per):
    rows = cols.shape[0]
    lane = lax.broadcasted_iota(jnp.int32, (rows, width), 1)
    out = jnp.broadcast_to(cols[:, 0:1], (rows, width))
    for h in range(1, width // per):
        out = jnp.where(lane >= h * per, cols[:, h:h + 1], out)
    return out


def _ssd_kernel(xbc_ref, z_ref, dt_ref, cw_ref, cb_ref, dtb_ref, alog_ref, dsk_ref, nw_ref,
                y_ref, st_ref, tail_sc, state_sc):
    c = pl.program_id(0)

    @pl.when(c == 0)
    def _():
        tail_sc[...] = jnp.zeros_like(tail_sc)
        state_sc[...] = jnp.zeros_like(state_sc)

    for b in range(xbc_ref.shape[0]):
        state = _ssd_chunk(xbc_ref.at[b], z_ref.at[b], dt_ref.at[b], cw_ref, cb_ref, dtb_ref, alog_ref, dsk_ref,
                           nw_ref, y_ref.at[b], tail_sc.at[b], state_sc.at[b])

        @pl.when(c == pl.num_programs(0) - 1)
        def _():
            st_ref[b] = state


def _ssd_chunk(xbc_ref, z_ref, dt_ref, cw_ref, cb_ref, dtb_ref, alog_ref, dsk_ref, nw_ref, y_ref, tail_sc,
               state_sc):
    q = SSM_CHUNK
    u = xbc_ref[...]
    full = jnp.concatenate([tail_sc[...], u], axis=0)
    tail_sc[...] = u[q - 8:, :]
    conv = full * cw_ref[SSM_CONV - 1:SSM_CONV, :]
    for j in range(1, SSM_CONV):
        conv = conv + pltpu.roll(full, j, 0) * cw_ref[SSM_CONV - 1 - j:SSM_CONV - j, :]
    xbc = _silu(conv[8:, :] + cb_ref[...])
    xs = xbc[:, :SSM_WIDTH]
    bm = (xbc[:, SSM_WIDTH:SSM_WIDTH + SSM_STATE], xbc[:, SSM_WIDTH + SSM_STATE:SSM_WIDTH + 2 * SSM_STATE])
    cm = (xbc[:, SSM_WIDTH + 2 * SSM_STATE:SSM_WIDTH + 3 * SSM_STATE], xbc[:, SSM_WIDTH + 3 * SSM_STATE:])

    dt = _softplus(dt_ref[...] + dtb_ref[...])
    dta = dt * (-jnp.exp(alog_ref[...]))
    ti = lax.broadcasted_iota(jnp.int32, (q, q), 0)
    si = lax.broadcasted_iota(jnp.int32, (q, q), 1)
    causal = si <= ti
    tri = jnp.where(causal, 1.0, 0.0)
    a_cs = jnp.dot(tri, dta, precision=lax.Precision.HIGHEST, preferred_element_type=F32)
    a_cs_t = a_cs.T
    last = a_cs[q - 1:q, :]

    dt_x = _head_expand(dt, SSM_WIDTH, SSM_HEADDIM)
    xdt = xs * dt_x
    xdt_end = xdt * _head_expand(jnp.exp(last - a_cs), SSM_WIDTH, SSM_HEADDIM)
    lane = lax.broadcasted_iota(jnp.int32, (q, SSM_WIDTH), 1)

    cb = [_dot_nt(cm[g], bm[g]) for g in range(2)]
    y_diag = jnp.zeros((q, SSM_WIDTH), F32)
    for h in range(SSM_HEADS):
        seg = a_cs[:, h:h + 1] - a_cs_t[h:h + 1, :]
        decay = jnp.exp(jnp.where(causal, seg, NEG_INF))
        xh = jnp.where((lane >= h * SSM_HEADDIM) & (lane < (h + 1) * SSM_HEADDIM), xdt, 0.0)
        y_diag = y_diag + _dot(cb[h // 2] * decay, xh)

    state = state_sc[...]
    sb = state.astype(BF16)
    y_off = jnp.where(lane < 2 * SSM_HEADDIM, _dot_nt(cm[0], sb), _dot_nt(cm[1], sb))
    y_off = y_off * _head_expand(jnp.exp(a_cs), SSM_WIDTH, SSM_HEADDIM)

    xt = xdt_end.T
    rowi = lax.broadcasted_iota(jnp.int32, (SSM_WIDTH, SSM_STATE), 0)
    new = jnp.where(rowi < 2 * SSM_HEADDIM, _dot(xt, bm[0]), _dot(xt, bm[1]))
    e_last = jnp.exp(last)
    dec_rows = jnp.broadcast_to(e_last[:, 0:1], (SSM_WIDTH, SSM_STATE))
    for h in range(1, SSM_HEADS):
        dec_rows = jnp.where(rowi >= h * SSM_HEADDIM, e_last[:, h:h + 1], dec_rows)
    state = state * dec_rows + new
    state_sc[...] = state

    y = y_diag + y_off + dsk_ref[...] * xs
    y = y * _silu(z_ref[...])
    ms = jnp.mean(y * y, axis=1, keepdims=True)
    y_ref[...] = y * lax.rsqrt(ms + RMS_EPS) * nw_ref[...]
    return state


def _ssd_prompt(xbc, z, dt_raw, conv_w, conv_b, dt_bias, a_log, d_skip_x, norm_w, *, batch, seq):
    nc = seq // SSM_CHUNK
    blk = lambda w: pl.BlockSpec((batch, SSM_CHUNK, w), lambda c: (0, c, 0))
    y, st = pl.pallas_call(
        _ssd_kernel,
        grid=(nc,),
        in_specs=[blk(CONV_CH), blk(SSM_WIDTH), blk(DT_PAD),
                  _const_spec((SSM_CONV, CONV_CH)), _const_spec((1, CONV_CH)),
                  _const_spec((1, DT_PAD)), _const_spec((1, DT_PAD)),
                  _const_spec((1, SSM_WIDTH)), _const_spec((1, SSM_WIDTH))],
        out_specs=[blk(SSM_WIDTH), pl.BlockSpec((batch, SSM_WIDTH, SSM_STATE), lambda c: (0, 0, 0))],
        out_shape=[jax.ShapeDtypeStruct((batch, seq, SSM_WIDTH), F32),
                   jax.ShapeDtypeStruct((batch, SSM_WIDTH, SSM_STATE), F32)],
        scratch_shapes=[pltpu.VMEM((batch, 8, CONV_CH), F32), pltpu.VMEM((batch, SSM_WIDTH, SSM_STATE), F32)],
        compiler_params=_cparams(("arbitrary",)),
        name="ssd_prompt",
    )(xbc.reshape(batch, seq, CONV_CH), z.reshape(batch, seq, SSM_WIDTH), dt_raw.reshape(batch, seq, DT_PAD),
      conv_w, conv_b, dt_bias, a_log, d_skip_x, norm_w)
    return y.reshape(batch * seq, SSM_WIDTH), st.reshape(batch, SSM_HEADS, SSM_HEADDIM, SSM_STATE)


def _pool_select(sums, inv_cnt, lane):
    out = sums[POOL_WINDOWS[0]] * inv_cnt[POOL_WINDOWS[0]]
    for g in range(1, len(POOL_WINDOWS)):
        w = POOL_WINDOWS[g]
        out = jnp.where(lane >= g * POOL_GDIM, sums[w] * inv_cnt[w], out)
    return out


def _pool_kernel(u_ref, w_ref, sc_ref, y_ref, tail_sc, *, tp):
    c = pl.program_id(1)
    halo = 16

    @pl.when(c == 0)
    def _():
        tail_sc[...] = jnp.zeros_like(tail_sc)

    u = u_ref[...]
    full = jnp.concatenate([tail_sc[...], u], axis=0)
    tail_sc[...] = u[tp - halo:, :]
    sums = {}
    run = full
    w = 1
    while w < max(POOL_WINDOWS):
        run = run + pltpu.roll(run, w, 0)
        w *= 2
        sums[w] = run[halo:, :]
    pos = c * tp + lax.broadcasted_iota(jnp.int32, (tp, 1), 0)
    inv_cnt = {w: 1.0 / jnp.minimum(w, pos + 1).astype(F32) for w in POOL_WINDOWS}
    lane = lax.broadcasted_iota(jnp.int32, (tp, POOL_WIDTH), 1)
    d = _pool_select(sums, inv_cnt, lane) - u
    y_ref[...] = _dot(d, w_ref[...]) * sc_ref[...]


def _pool_prompt(pu, pool_wbd, pool_scale, *, batch, seq, tp):
    blk = pl.BlockSpec((None, tp, POOL_WIDTH), lambda b, c: (b, c, 0))
    y = pl.pallas_call(
        functools.partial(_pool_kernel, tp=tp),
        grid=(batch, seq // tp),
        in_specs=[blk, _const_spec((POOL_WIDTH, POOL_WIDTH)), _const_spec((1, POOL_WIDTH))],
        out_specs=blk,
        out_shape=jax.ShapeDtypeStruct((batch, seq, POOL_WIDTH), F32),
        scratch_shapes=[pltpu.VMEM((16, POOL_WIDTH), F32)],
        compiler_params=_cparams(("parallel", "arbitrary")),
        name="pool_prompt",
    )(pu.reshape(batch, seq, POOL_WIDTH), pool_wbd, pool_scale)
    return y.reshape(batch * seq, POOL_WIDTH)


def _sample_prep_kernel(xbc_ref, cbuf_ref, cw_ref, cb_ref, dt_ref, dtb_ref, alog_ref, pu_ref, pbuf_ref,
                        pw_ref, psc_ref, xact_ref, dto_ref, dec_ref, yp_ref, *, pos0):
    u = xbc_ref[...]
    conv = u * cw_ref[SSM_CONV - 1:SSM_CONV, :] + cb_ref[...]
    for j in range(SSM_CONV - 1):
        conv = conv + cbuf_ref[j] * cw_ref[j:j + 1, :]
    xact_ref[...] = _silu(conv)
    dt = _softplus(dt_ref[...] + dtb_ref[...])
    dto_ref[...] = dt
    dec_ref[...] = jnp.exp(dt * (-jnp.exp(alog_ref[...])))
    pu = pu_ref[...]
    run = pu
    sums = {}
    for j in range(1, max(POOL_WINDOWS)):
        run = run + pbuf_ref[POOL_BUF - j]
        if j + 1 in POOL_WINDOWS:
            sums[j + 1] = run
    inv_cnt = {w: 1.0 / float(min(w, pos0 + 1)) for w in POOL_WINDOWS}
    lane = lax.broadcasted_iota(jnp.int32, pu.shape, 1)
    d = _pool_select(sums, inv_cnt, lane) - pu
    yp_ref[...] = _dot(d, pw_ref[...]) * psc_ref[...]


def _sample_prep(xbc, conv_buf, conv_w, conv_b, dt_raw, dt_bias, a_log, pu, pool_buf, pool_wbd, pool_scale, *, pos0):
    bs = xbc.shape[0]
    args = (xbc, conv_buf, conv_w, conv_b, dt_raw, dt_bias, a_log, pu, pool_buf, pool_wbd, pool_scale)
    shapes = [(bs, CONV_CH), (bs, DT_PAD), (bs, DT_PAD), (bs, POOL_WIDTH)]
    return pl.pallas_call(
        functools.partial(_sample_prep_kernel, pos0=pos0),
        grid=(1,),
        in_specs=[_const_spec(a.shape) for a in args],
        out_specs=[_const_spec(s) for s in shapes],
        out_shape=[jax.ShapeDtypeStruct(s, F32) for s in shapes],
        compiler_params=_cparams(("arbitrary",)),
        name="sample_prep",
    )(*args)


def _ssd_step_kernel(s_ref, x_ref, z_ref, dt_ref, dec_ref, b_ref, c_ref, dsk_ref, nw_ref, y_ref, so_ref, *, group):
    rowi = lax.broadcasted_iota(jnp.int32, (SSM_WIDTH, SSM_STATE), 0)
    for i in range(group):
        brow, crow = b_ref[i], c_ref[i]
        bsel = jnp.where(rowi < 2 * SSM_HEADDIM, brow[:, :SSM_STATE], brow[:, SSM_STATE:])
        csel = jnp.where(rowi < 2 * SSM_HEADDIM, crow[:, :SSM_STATE], crow[:, SSM_STATE:])
        x = x_ref[i]
        sn = s_ref[i] * dec_ref[i] + (x * dt_ref[i]) * bsel
        so_ref[i] = sn
        y = jnp.sum(sn * csel, axis=1, keepdims=True) + dsk_ref[...] * x
        y = y * _silu(z_ref[i])
        ms = jnp.sum(y * y, axis=0, keepdims=True) * (1.0 / SSM_WIDTH)
        y_ref[i] = y * lax.rsqrt(ms + RMS_EPS) * nw_ref[...]


def _ssd_step(state_all, layer, xact, z, dt, dec, d_skip_x, norm_w, *, group=8):
    bs = xact.shape[0]
    nb = bs // group
    rep = lambda t: jnp.repeat(t[:, :SSM_HEADS], SSM_HEADDIM, axis=1).reshape(bs, SSM_WIDTH, 1)
    colb = pl.BlockSpec((group, SSM_WIDTH, 1), lambda b: (b, 0, 0))
    rowb = pl.BlockSpec((group, 1, 2 * SSM_STATE), lambda b: (b, 0, 0))
    st_in = pl.BlockSpec((group, SSM_WIDTH, SSM_STATE), lambda b: (layer * nb + b, 0, 0))
    st_out = pl.BlockSpec((group, SSM_WIDTH, SSM_STATE), lambda b: (b, 0, 0))
    y, st = pl.pallas_call(
        functools.partial(_ssd_step_kernel, group=group),
        grid=(nb,),
        in_specs=[st_in, colb, colb, colb, colb, rowb, rowb,
                  _const_spec((SSM_WIDTH, 1)), _const_spec((SSM_WIDTH, 1))],
        out_specs=[colb, st_out],
        out_shape=[jax.ShapeDtypeStruct((bs, SSM_WIDTH, 1), F32),
                   jax.ShapeDtypeStruct((bs, SSM_WIDTH, SSM_STATE), F32)],
        compiler_params=_cparams(("parallel",)),
        name="ssd_step",
    )(state_all, xact[:, :SSM_WIDTH].reshape(bs, SSM_WIDTH, 1), z.reshape(bs, SSM_WIDTH, 1), rep(dt), rep(dec),
      xact[:, SSM_WIDTH:SSM_WIDTH + 2 * SSM_STATE].reshape(bs, 1, 2 * SSM_STATE),
      xact[:, SSM_WIDTH + 2 * SSM_STATE:].reshape(bs, 1, 2 * SSM_STATE),
      d_skip_x.reshape(SSM_WIDTH, 1), norm_w.reshape(SSM_WIDTH, 1))
    return y.reshape(bs, SSM_WIDTH), st.reshape(bs, SSM_HEADS, SSM_HEADDIM, SSM_STATE)


def _mm_kernel(x_ref, w_ref, *out_refs, scale):
    y = _dot(x_ref[...], w_ref[...])
    if scale != 1.0:
        y = y * scale
    for r in out_refs:
        r[...] = y.astype(r.dtype)


def _matmul(x, w, *, tm, scale=1.0, dtypes=(F32,)):
    m, k = x.shape
    n = w.shape[1]
    res = pl.pallas_call(
        functools.partial(_mm_kernel, scale=scale),
        grid=(m // tm,),
        in_specs=[pl.BlockSpec((tm, k), lambda i: (i, 0)), _const_spec(w.shape)],
        out_specs=[pl.BlockSpec((tm, n), lambda i: (i, 0)) for _ in dtypes],
        out_shape=[jax.ShapeDtypeStruct((m, n), d) for d in dtypes],
        compiler_params=_cparams(("parallel",)),
        name="matmul",
    )(x, w)
    return res


def _mm_res_ln_kernel(*refs, n_in):
    a_refs = refs[:n_in]
    w_refs = refs[n_in:2 * n_in]
    x_ref, g_ref, b_ref, o_ref = refs[2 * n_in:]
    h = _dot(a_refs[0][...], w_refs[0][...])
    for a, w in zip(a_refs[1:], w_refs[1:]):
        h = h + _dot(a[...], w[...])
    o_ref[...] = _layer_norm(ALPHA * x_ref[...] + h, g_ref[...], b_ref[...])


def _mm_res_ln(acts, weights, x, g, b, *, tm, x_row0=0):
    m = acts[0].shape[0]
    row = lambda a: pl.BlockSpec((tm, a.shape[1]), lambda i: (i, 0))
    return pl.pallas_call(
        functools.partial(_mm_res_ln_kernel, n_in=len(acts)),
        grid=(m // tm,),
        in_specs=[row(a) for a in acts] + [_const_spec(w.shape) for w in weights]
        + [pl.BlockSpec((tm, D_MODEL), lambda i: (x_row0 + i, 0)), _const_spec(g.shape), _const_spec(b.shape)],
        out_specs=pl.BlockSpec((tm, D_MODEL), lambda i: (i, 0)),
        out_shape=jax.ShapeDtypeStruct((m, D_MODEL), F32),
        compiler_params=_cparams(("parallel",)),
        name="mm_res_ln",
    )(*acts, *weights, x, g, b)


def _mem_prompt_kernel(x_ref, mk_ref, mv_ref, wq_ref, wo_ref, g_ref, b_ref, o_ref):
    x = x_ref[...]
    q = (_dot(x, wq_ref[...]) * (MEM_HD ** -0.5)).astype(BF16)
    outs = []
    for h in range(MEM_HEADS):
        sl = slice(h * MEM_HD, (h + 1) * MEM_HD)
        s = lax.dot_general(q[:, sl], mk_ref[:, sl], (((1,), (1,)), ((), ())), preferred_element_type=F32)
        p = jnp.exp(s - jnp.max(s, axis=1, keepdims=True))
        p = p * (1.0 / jnp.sum(p, axis=1, keepdims=True))
        outs.append(jnp.dot(p.astype(BF16), mv_ref[:, sl], preferred_element_type=F32))
    o = jnp.concatenate(outs, axis=1)
    o_ref[...] = _layer_norm(ALPHA * x + _dot(o, wo_ref[...]), g_ref[...], b_ref[...])


def _mem_attn_prompt(x, mkb, mvb, wq, wo, g, b, *, batch, seq, tm):
    n_mem = mkb.shape[0] // batch
    nt = seq // tm
    xblk = pl.BlockSpec((tm, D_MODEL), lambda bb, i: (bb * nt + i, 0))
    mblk = pl.BlockSpec((None, n_mem, D_MODEL), lambda bb, i: (bb, 0, 0))
    return pl.pallas_call(
        _mem_prompt_kernel,
        grid=(batch, nt),
        in_specs=[xblk, mblk, mblk, _const_spec(wq.shape), _const_spec(wo.shape),
                  _const_spec(g.shape), _const_spec(b.shape)],
        out_specs=xblk,
        out_shape=jax.ShapeDtypeStruct((batch * seq, D_MODEL), F32),
        compiler_params=_cparams(("parallel", "parallel")),
        name="mem_attn_prompt",
    )(x, mkb.reshape(batch, n_mem, D_MODEL), mvb.reshape(batch, n_mem, D_MODEL), wq, wo, g, b)


def _mem_sample_kernel(q_ref, mk_ref, mv_ref, o_ref, *, group):
    for i in range(group):
        q = q_ref[i]
        s = jnp.sum(mk_ref[i] * q, axis=2, keepdims=True)
        p = jnp.exp(s - jnp.max(s, axis=0, keepdims=True))
        p = p * (1.0 / jnp.sum(p, axis=0, keepdims=True))
        o_ref[i] = jnp.sum(mv_ref[i] * p, axis=0)


def _mem_attn_sample(q, mem_k_all, mem_v_all, *, layer, group):
    bs = q.shape[0]
    n_mem = mem_k_all.shape[1]
    nb = bs // group
    qblk = pl.BlockSpec((group, MEM_HEADS, MEM_HD), lambda i: (i, 0, 0))
    mblk = pl.BlockSpec((group, n_mem, MEM_HEADS, MEM_HD), lambda i: (layer * nb + i, 0, 0, 0))
    out = pl.pallas_call(
        functools.partial(_mem_sample_kernel, group=group),
        grid=(nb,),
        in_specs=[qblk, mblk, mblk],
        out_specs=qblk,
        out_shape=jax.ShapeDtypeStruct((bs, MEM_HEADS, MEM_HD), F32),
        compiler_params=_cparams(("parallel",)),
        name="mem_attn_sample",
    )(q.reshape(bs, MEM_HEADS, MEM_HD), mem_k_all, mem_v_all)
    return out.reshape(bs, D_MODEL)


def _ffn_kernel(x_ref, w1_ref, w3_ref, w2_ref, g_ref, b_ref, o_ref, *, n_chunks):
    x = x_ref[...]
    xb = x.astype(BF16)
    acc = jnp.zeros(x.shape, F32)
    for c in range(n_chunks):
        h = _silu(jnp.dot(xb, w1_ref[c], preferred_element_type=F32)) * jnp.dot(xb, w3_ref[c],
                                                                                 preferred_element_type=F32)
        acc = acc + jnp.dot(h.astype(BF16), w2_ref[c], preferred_element_type=F32)
    o_ref[...] = _layer_norm(ALPHA * x + acc, g_ref[...], b_ref[...])


def _ffn(x, w1c, w3c, w2c, g, b, *, tm):
    m = x.shape[0]
    row = pl.BlockSpec((tm, D_MODEL), lambda i: (i, 0))
    wspec = lambda w: pl.BlockSpec(w.shape, lambda i: (0, 0, 0), pipeline_mode=pl.Buffered(1))
    return pl.pallas_call(
        functools.partial(_ffn_kernel, n_chunks=w1c.shape[0]),
        grid=(m // tm,),
        in_specs=[row, wspec(w1c), wspec(w3c), wspec(w2c), _const_spec(g.shape), _const_spec(b.shape)],
        out_specs=row,
        out_shape=jax.ShapeDtypeStruct(x.shape, F32),
        compiler_params=_cparams(("parallel",), vmem_mb=56),
        name="ffn_dense",
    )(x, w1c, w3c, w2c, g, b)


def _router_kernel(x_ref, wh_ref, wl_ref, b_ref, o_ref):
    x = x_ref[...]
    xh = x.astype(BF16)
    xl = (x - xh.astype(F32)).astype(BF16)
    wh = wh_ref[...]
    logits = (jnp.dot(xh, wh, preferred_element_type=F32) + jnp.dot(xl, wh, preferred_element_type=F32)
              + jnp.dot(xh, wl_ref[...], preferred_element_type=F32))
    lane = lax.broadcasted_iota(jnp.int32, logits.shape, 1)
    logits = jnp.where(lane < N_EXPERTS, logits + b_ref[...], NEG_INF)
    m1 = jnp.max(logits, axis=1, keepdims=True)
    i1 = jnp.min(jnp.where(logits == m1, lane, LANES), axis=1, keepdims=True)
    rest = jnp.where(lane == i1, NEG_INF, logits)
    m2 = jnp.max(rest, axis=1, keepdims=True)
    i2 = jnp.min(jnp.where(rest == m2, lane, LANES), axis=1, keepdims=True)
    e = jnp.exp(m2 - m1)
    g1 = 1.0 / (1.0 + e)
    g2 = e * g1
    out = jnp.where(lane == 0, i1.astype(F32), jnp.where(lane == 1, i2.astype(F32),
                    jnp.where(lane == 2, g1, jnp.where(lane == 3, g2, 0.0))))
    o_ref[...] = out


def _router(x, w_hi, w_lo, b_pad, *, tm):
    m = x.shape[0]
    return pl.pallas_call(
        _router_kernel,
        grid=(m // tm,),
        in_specs=[pl.BlockSpec((tm, D_MODEL), lambda i: (i, 0)), _const_spec(w_hi.shape), _const_spec(w_lo.shape),
                  _const_spec(b_pad.shape)],
        out_specs=pl.BlockSpec((tm, LANES), lambda i: (i, 0)),
        out_shape=jax.ShapeDtypeStruct((m, LANES), F32),
        compiler_params=_cparams(("parallel",)),
        name="moe_router",
    )(x, w_hi, w_lo, b_pad)


def _expert_kernel(te_ref, tr_ref, x_ref, w1_ref, w3_ref, w2_ref, o_ref, acc_sc, *, sub):
    i = pl.program_id(0)
    j = pl.program_id(1)
    rows = tr_ref[i]
    tm = x_ref.shape[0]

    @pl.when(j == 0)
    def _():
        acc_sc[...] = jnp.zeros_like(acc_sc)

    def swiglu_into_acc(sl):
        xb = x_ref[sl, :].astype(BF16)
        w1 = w1_ref[...].astype(BF16)
        w3 = w3_ref[...].astype(BF16)
        h = _silu(jnp.dot(xb, w1, preferred_element_type=F32)) * jnp.dot(xb, w3, preferred_element_type=F32)
        acc_sc[sl, :] += jnp.dot(h.astype(BF16), w2_ref[...].astype(BF16), preferred_element_type=F32)

    @pl.when(rows == tm)
    def _():
        swiglu_into_acc(slice(None))

    @pl.when((rows > 0) & (rows < tm))
    def _():
        for s in range(tm // sub):
            pl.when(rows > s * sub)(functools.partial(swiglu_into_acc, slice(s * sub, (s + 1) * sub)))

    @pl.when(j == pl.num_programs(1) - 1)
    def _():
        o_ref[...] = acc_sc[...]


def _experts(x_sorted, tile_expert, tile_rows, w1, w3, w2, *, tm, sub, fc):
    n_slots = x_sorted.shape[0]
    nj = w1.shape[2] // fc
    last = nj - 1

    def jj(i, j, tr):
        return jnp.where(tr[i] > 0, j, last)

    xs = pl.BlockSpec((tm, D_MODEL), lambda i, j, te, tr: (i, 0))
    w13 = pl.BlockSpec((None, D_MODEL, fc), lambda i, j, te, tr: (te[i], 0, jj(i, j, tr)))
    w2s = pl.BlockSpec((None, fc, D_MODEL), lambda i, j, te, tr: (te[i], jj(i, j, tr), 0))
    return pl.pallas_call(
        functools.partial(_expert_kernel, sub=sub),
        grid_spec=pltpu.PrefetchScalarGridSpec(
            num_scalar_prefetch=2, grid=(n_slots // tm, nj),
            in_specs=[xs, w13, w13, w2s], out_specs=xs,
            scratch_shapes=[pltpu.VMEM((tm, D_MODEL), F32)]),
        out_shape=jax.ShapeDtypeStruct((n_slots, D_MODEL), F32),
        compiler_params=_cparams(("arbitrary", "arbitrary"), vmem_mb=56),
        name="moe_experts",
    )(tile_expert, tile_rows, x_sorted, w1, w3, w2)


def _dispatch_kernel(slot_ref, x_ref, init_ref, xs_ref, sem, *, tm, tok0):
    del init_ref
    base = (tok0 + pl.program_id(0) * tm) * 2

    def issue(r, carry):
        for k in range(2):
            dst = slot_ref[base + 2 * r + k]
            pltpu.make_async_copy(x_ref.at[pl.ds(r, 1), :], xs_ref.at[pl.ds(dst, 1), :], sem).start()
        return carry

    lax.fori_loop(0, tm, issue, 0, unroll=8)
    for _ in range(2):
        pltpu.make_async_copy(x_ref, x_ref, sem).wait()


def _dispatch(x, slot, buf, *, tm, tok0):
    t = x.shape[0]
    return pl.pallas_call(
        functools.partial(_dispatch_kernel, tm=tm, tok0=tok0),
        grid_spec=pltpu.PrefetchScalarGridSpec(
            num_scalar_prefetch=1, grid=(t // tm,),
            in_specs=[pl.BlockSpec((tm, D_MODEL), lambda i, s: (i, 0)), pl.BlockSpec(memory_space=pl.ANY)],
            out_specs=pl.BlockSpec(memory_space=pl.ANY),
            scratch_shapes=[pltpu.SemaphoreType.DMA(())]),
        out_shape=jax.ShapeDtypeStruct(buf.shape, F32),
        input_output_aliases={2: 0},
        compiler_params=_cparams(("arbitrary",)),
        name="moe_dispatch",
    )(slot, x, buf)


def _combine_ln_kernel(slot_ref, x_ref, r_ref, y_ref, g_ref, b_ref, o_ref, ybuf, sem, *, tm, tok0):
    base = (tok0 + pl.program_id(0) * tm) * 2

    def issue(r, carry):
        for k in range(2):
            src = slot_ref[base + 2 * r + k]
            pltpu.make_async_copy(y_ref.at[pl.ds(src, 1), :], ybuf.at[k, pl.ds(r, 1), :], sem).start()
        return carry

    lax.fori_loop(0, tm, issue, 0, unroll=8)
    pltpu.make_async_copy(ybuf, ybuf, sem).wait()
    r = r_ref[...]
    f = r[:, 2:3] * ybuf[0] + r[:, 3:4] * ybuf[1]
    o_ref[...] = _layer_norm(ALPHA * x_ref[...] + f, g_ref[...], b_ref[...])


def _combine_ln(x, y_sorted, slot, route, g, b, *, tm, tok0):
    rows = x.shape[0]
    row = lambda w: pl.BlockSpec((tm, w), lambda i, s: (i, 0))
    const = lambda a: pl.BlockSpec(a.shape, lambda i, s: (0,) * a.ndim)
    return pl.pallas_call(
        functools.partial(_combine_ln_kernel, tm=tm, tok0=tok0),
        grid_spec=pltpu.PrefetchScalarGridSpec(
            num_scalar_prefetch=1, grid=(rows // tm,),
            in_specs=[row(D_MODEL), row(LANES), pl.BlockSpec(memory_space=pl.ANY), const(g), const(b)],
            out_specs=row(D_MODEL),
            scratch_shapes=[pltpu.VMEM((2, tm, D_MODEL), F32), pltpu.SemaphoreType.DMA(())]),
        out_shape=jax.ShapeDtypeStruct((rows, D_MODEL), F32),
        compiler_params=_cparams(("arbitrary",)),
        name="moe_combine_ln",
    )(slot, x, route, y_sorted, g, b)


def _moe(xp, xs, router_w, router_b, w1, w3, w2, g, b, *, tm_e, sub, fc, tm_p):
    n_p, n_s = xp.shape[0], xs.shape[0]
    t = n_p + n_s
    w_pad = jnp.zeros((D_MODEL, LANES), F32).at[:, :N_EXPERTS].set(router_w)
    w_hi = w_pad.astype(BF16)
    w_lo = (w_pad - w_hi.astype(F32)).astype(BF16)
    b_pad = jnp.zeros((1, LANES), F32).at[0, :N_EXPERTS].set(router_b)
    route_p = _router(xp, w_hi, w_lo, b_pad, tm=tm_p)
    route_s = _router(xs, w_hi, w_lo, b_pad, tm=n_s)
    top_i = jnp.concatenate([route_p[:, :2], route_s[:, :2]], axis=0).astype(jnp.int32).reshape(-1)
    onehot = (top_i[:, None] == jnp.arange(N_EXPERTS)[None, :]).astype(jnp.int32)
    csum = jnp.cumsum(onehot, axis=0)
    counts = csum[-1]
    rank = jnp.sum((csum - onehot) * onehot, axis=1)
    padded = ((counts + tm_e - 1) // tm_e) * tm_e
    ends = jnp.cumsum(padded)
    starts = ends - padded
    slot = (starts[top_i] + rank).astype(jnp.int32)
    n_tiles = (2 * t + N_EXPERTS * (tm_e - 1)) // tm_e + 1
    tile_start = jnp.arange(n_tiles, dtype=jnp.int32) * tm_e
    tile_expert = jnp.sum((tile_start[:, None] >= ends[None, :]).astype(jnp.int32), axis=1)
    tile_expert = jnp.minimum(tile_expert, N_EXPERTS - 1)
    tile_rows = jnp.clip((starts + counts)[tile_expert] - tile_start, 0, tm_e)
    tile_rows = jnp.where(tile_start < ends[-1], tile_rows, 0).astype(jnp.int32)
    last_e = tile_expert[jnp.maximum(ends[-1] // tm_e - 1, 0)]
    tile_expert = jnp.where(tile_rows > 0, tile_expert, last_e).astype(jnp.int32)
    x_sorted = jnp.zeros((n_tiles * tm_e, D_MODEL), F32)
    x_sorted = _dispatch(xp, slot, x_sorted, tm=tm_p, tok0=0)
    x_sorted = _dispatch(xs, slot, x_sorted, tm=n_s, tok0=n_p)
    y_sorted = _experts(x_sorted, tile_expert, tile_rows, w1, w3, w2, tm=tm_e, sub=sub, fc=fc)
    return (_combine_ln(xp, y_sorted, slot, route_p, g, b, tm=tm_p // 2, tok0=0),
            _combine_ln(xs, y_sorted, slot, route_s, g, b, tm=n_s, tok0=n_p))


def _pack_w_in(w_in_t):
    q, k, v, z, xbc, dt, pu = jnp.split(w_in_t, np.cumsum([512, 512, 512, 256, 768, 4]).tolist(), axis=0)
    pad = jnp.zeros((DT_PAD - SSM_HEADS, D_MODEL), w_in_t.dtype)
    return jnp.concatenate([q, k, v, z, xbc, pu, dt, pad], axis=0).astype(BF16)


def _block_diag(pool_w):
    out = jnp.zeros((POOL_WIDTH, POOL_WIDTH), F32)
    for g in range(len(POOL_WINDOWS)):
        out = out.at[g * POOL_GDIM:(g + 1) * POOL_GDIM, g * POOL_GDIM:(g + 1) * POOL_GDIM].set(pool_w[g])
    return out.astype(BF16)


def _pad_heads(v):
    return jnp.zeros((1, DT_PAD), F32).at[0, :SSM_HEADS].set(v)


def kernel(x_prompt, x_sample, cache_k, cache_v, cache_mem_k, cache_mem_v, state_ssm, state_conv, state_pool, page_table, mem_prompt, w_in, lam_q1, lam_k1, lam_q2, lam_k2, da_norm_w, conv_w, conv_b, dt_bias, a_log, d_skip, ssm_norm_w, pool_w, pool_scale, w_out, ln1_g, ln1_b, wq_mem, wk_mem, wv_mem, wo_mem, ln2_g, ln2_b, ffn_w1, ffn_w3, ffn_w2, moe_router, moe_router_b, moe_w1, moe_w3, moe_w2, ln3_g, ln3_b):
    bp, seq, _ = x_prompt.shape
    bs = x_sample.shape[0]
    n_pages = page_table.shape[1]
    past = n_pages * PAGE_SIZE
    n_mem = mem_prompt.shape[1]
    tp_rows = bp * seq
    TM = 512

    xp = x_prompt.reshape(tp_rows, D_MODEL)
    xs = x_sample.reshape(bs, D_MODEL)
    tab_p = _rope_tables(jnp.arange(seq))
    tab_s = _rope_tables(jnp.full((1,), past))
    cache_k4 = jnp.transpose(cache_k, (0, 1, 3, 4, 5, 2)).reshape(DEPTH, -1, DA_WIDTH, PAGE_SIZE)
    cache_v4 = cache_v.reshape(DEPTH, -1, PAGE_SIZE * DA_HEADS, DA_DV)
    mem_k_all = cache_mem_k.reshape(DEPTH * bs, n_mem, MEM_HEADS, MEM_HD)
    mem_v_all = cache_mem_v.reshape(DEPTH * bs, n_mem, MEM_HEADS, MEM_HD)
    state_all = state_ssm.reshape(DEPTH * bs, SSM_WIDTH, SSM_STATE)
    conv_hist = jnp.transpose(state_conv, (0, 2, 1, 3))
    pool_hist = jnp.transpose(state_pool, (0, 2, 1, 3))
    w_in_t = jnp.transpose(w_in, (2, 0, 1))
    mem_flat = mem_prompt.reshape(bp * n_mem, D_MODEL)
    row = lambda v: v.reshape(1, -1)

    outs = {k: [] for k in ("kp", "vp", "ks", "vs", "mk", "mv", "hp", "hs", "cp", "cs", "pp", "ps")}
    for l in range(DEPTH):
        lam_init = 0.8 - 0.6 * math.exp(-0.3 * l)
        w_cat = _pack_w_in(w_in_t[:, l, :])
        lam_vecs = (row(lam_q1[l]), row(lam_k1[l]), row(lam_q2[l]), row(lam_k2[l]))
        nw = row(da_norm_w[l])
        dtb, alog = _pad_heads(dt_bias[l]), _pad_heads(a_log[l])
        dsk = row(jnp.repeat(d_skip[l], SSM_HEADDIM))
        snw = row(ssm_norm_w[l])
        pwbd = _block_diag(pool_w[l])
        psc = row(pool_scale[l])
        wo_b = w_out[l].astype(BF16)
        wo_parts = (wo_b[:DA_WIDTH], wo_b[DA_WIDTH:DA_WIDTH + SSM_WIDTH], wo_b[DA_WIDTH + SSM_WIDTH:])
        g1, b1, g2, b2, g3, b3 = (row(ln1_g[l]), row(ln1_b[l]), row(ln2_g[l]), row(ln2_b[l]),
                                  row(ln3_g[l]), row(ln3_b[l]))

        qp, kb, vb, kpg, vrow, zp, xbcp, pup, dtp = _proj_in(xp, w_cat, tab_p, prompt=True, tm=TM, seq=seq,
                                                             rows=tp_rows)
        oa_p = _flash_attn(qp, kb, vb, lam_vecs, nw.reshape(DA_DV, 1), batch=bp, seq=seq, tq=512, tk=512,
                           lam_init=lam_init)
        y_p, h_p = _ssd_prompt(xbcp, zp, dtp, conv_w[l], row(conv_b[l]), dtb, alog, dsk, snw, batch=bp, seq=seq)
        yp_p = _pool_prompt(pup, pwbd, psc, batch=bp, seq=seq, tp=512)
        xp = _mm_res_ln((oa_p, y_p, yp_p), wo_parts, xp, g1, b1, tm=TM)

        q_s, k_s, v_s, z_s, xbc_s, pu_s, dt_s = _proj_in(xs, w_cat, tab_s, prompt=False, tm=bs, seq=1, rows=bs)
        oa_s = _paged_attn(q_s, k_s, v_s, cache_k4, cache_v4, page_table, lam_vecs, nw, layer=l, lam_init=lam_init)
        xact, dt_a, dec, yp_s = _sample_prep(xbc_s, conv_hist[l], conv_w[l], row(conv_b[l]), dt_s, dtb, alog,
                                             pu_s, pool_hist[l], pwbd, psc, pos0=past)
        y_s, h_s = _ssd_step(state_all, l, xact, z_s, dt_a, dec, dsk, snw)
        xs = _mm_res_ln((oa_s, y_s, yp_s), wo_parts, xs, g1, b1, tm=bs)

        wq_b, wk_b, wv_b, wom_b = (wq_mem[l].astype(BF16), wk_mem[l].astype(BF16), wv_mem[l].astype(BF16),
                                   wo_mem[l].astype(BF16))
        mk, mkb = _matmul(mem_flat, wk_b, tm=bp * n_mem // 2, dtypes=(F32, BF16))
        mv, mvb = _matmul(mem_flat, wv_b, tm=bp * n_mem // 2, dtypes=(F32, BF16))
        xp = _mem_attn_prompt(xp, mkb, mvb, wq_b, wom_b, g2, b2, batch=bp, seq=seq, tm=TM)
        (qm_s,) = _matmul(xs, wq_b, tm=bs, scale=MEM_HD ** -0.5)
        om_s = _mem_attn_sample(qm_s, mem_k_all, mem_v_all, layer=l, group=4)
        xs = _mm_res_ln((om_s,), (wom_b,), xs, g2, b2, tm=bs)

        j = l // 2
        if l % 2 == 0:
            f = ffn_w1.shape[2]
            fc = 256
            w1c = ffn_w1[j].astype(BF16).reshape(D_MODEL, f // fc, fc).transpose(1, 0, 2)
            w3c = ffn_w3[j].astype(BF16).reshape(D_MODEL, f // fc, fc).transpose(1, 0, 2)
            w2c = ffn_w2[j].astype(BF16).reshape(f // fc, fc, D_MODEL)
            xp = _ffn(xp, w1c, w3c, w2c, g3, b3, tm=TM)
            xs = _ffn(xs, w1c, w3c, w2c, g3, b3, tm=bs)
        else:
            xp, xs = _moe(xp, xs, moe_router[j], moe_router_b[j], moe_w1[j], moe_w3[j], moe_w2[j], g3, b3,
                          tm_e=1024, sub=256, fc=512, tm_p=512)

        outs["kp"].append(jnp.transpose(kpg.reshape(bp, seq // PAGE_SIZE, DA_HEADS, 2, DA_DK, PAGE_SIZE),
                                        (0, 1, 5, 2, 3, 4)))
        outs["vp"].append(vrow.reshape(bp, seq // PAGE_SIZE, PAGE_SIZE, DA_HEADS, DA_DV))
        outs["ks"].append(k_s.reshape(bs, 1, DA_HEADS, 2, DA_DK))
        outs["vs"].append(v_s.reshape(bs, 1, DA_HEADS, DA_DV))
        outs["mk"].append(mk.reshape(bp, n_mem, MEM_HEADS, MEM_HD))
        outs["mv"].append(mv.reshape(bp, n_mem, MEM_HEADS, MEM_HD))
        outs["hp"].append(h_p)
        outs["hs"].append(h_s)
        outs["cp"].append(xbcp.reshape(bp, seq, CONV_CH)[:, seq - (SSM_CONV - 1):])
        outs["cs"].append(jnp.transpose(jnp.concatenate([conv_hist[l, 1:], xbc_s[None]], axis=0), (1, 0, 2)))
        outs["pp"].append(pup.reshape(bp, seq, POOL_WIDTH)[:, seq - POOL_BUF:])
        outs["ps"].append(jnp.transpose(jnp.concatenate([pool_hist[l, 1:], pu_s[None]], axis=0), (1, 0, 2)))

    st = lambda k: jnp.stack(outs[k])
    return (xp.reshape(bp, seq, D_MODEL), xs.reshape(bs, 1, D_MODEL), st("kp"), st("vp"), st("ks"), st("vs"),
            st("mk"), st("mv"), st("hp"), st("hs"), st("cp"), st("cs"), st("pp"), st("ps"))
```

```python
import functools
import math

import jax
import jax.numpy as jnp
import numpy as np
from jax import lax
from jax.experimental import pallas as pl
from jax.experimental.pallas import tpu as pltpu

F32 = jnp.float32
BF16 = jnp.bfloat16

D_MODEL = 1024
DEPTH = 2
PAGE_SIZE = 128
DA_HEADS = 4
DA_WIDTH = 512
DA_DV = 128
DA_DK = 64
ROT_DIM = 16
ROPE_THETA = 500000.0
SSM_WIDTH = 256
SSM_HEADS = 4
SSM_HEADDIM = 64
SSM_STATE = 128
SSM_CONV = 4
SSM_CHUNK = 128
CONV_CH = 768
POOL_WIDTH = 256
POOL_WINDOWS = (2, 4, 8, 16)
POOL_GDIM = 64
POOL_BUF = 15
MEM_HEADS = 4
MEM_HD = 256
N_EXPERTS = 8
ALPHA = (2.0 * DEPTH) ** 0.25
LN_EPS = 1e-5
RMS_EPS = 1e-6

LANES = 128
DT_PAD = LANES
_C_Q, _C_K, _C_V, _C_Z, _C_X, _C_P, _C_DT, _C_END = 0, 512, 1024, 1536, 1792, 2560, 2816, 2944

NEG_INF = float("-inf")
Q_SCALE = DA_DK ** -0.5 * math.log2(math.e)


def _cparams(sem, vmem_mb=48):
    return pltpu.CompilerParams(dimension_semantics=sem, vmem_limit_bytes=vmem_mb * 1024 * 1024)


def _dot(a, b):
    return jnp.dot(a.astype(BF16), b.astype(BF16), preferred_element_type=F32)


def _dot_nt(a, b):
    return lax.dot_general(a.astype(BF16), b.astype(BF16), (((1,), (1,)), ((), ())),
                           preferred_element_type=F32)


def _layer_norm(x, g, b):
    mu = jnp.mean(x, axis=-1, keepdims=True)
    xc = x - mu
    var = jnp.mean(xc * xc, axis=-1, keepdims=True)
    return xc * lax.rsqrt(var + LN_EPS) * g + b


def _silu(x):
    return x * (1.0 / (1.0 + jnp.exp(-x)))


def _softplus(x):
    return jnp.maximum(x, 0.0) + jnp.log(1.0 + jnp.exp(-jnp.abs(x)))


def _const_spec(shape):
    nd = len(shape)
    return pl.BlockSpec(shape, lambda *_: (0,) * nd)


def _rope_tables(pos):
    half = ROT_DIM // 2
    inv = ROPE_THETA ** (-jnp.arange(half, dtype=F32) * 2.0 / ROT_DIM)
    ang = pos.astype(F32)[:, None] * inv[None, :]
    cos, sin = jnp.cos(ang), jnp.sin(ang)
    n = pos.shape[0]
    c64 = jnp.concatenate([cos, cos, jnp.ones((n, DA_DK - ROT_DIM), F32)], axis=1)
    s1 = jnp.concatenate([-sin, jnp.zeros((n, DA_DK - half), F32)], axis=1)
    s2 = jnp.concatenate([jnp.zeros((n, half), F32), sin, jnp.zeros((n, DA_DK - ROT_DIM), F32)], axis=1)
    two = lambda t: jnp.concatenate([t, t], axis=1)
    return two(c64), two(s1), two(s2)


def _proj_in_kernel(x_ref, w_ref, c_ref, s1_ref, s2_ref, *out_refs, prompt):
    xb = x_ref[...].astype(BF16)
    cc, s1, s2 = c_ref[...], s1_ref[...], s2_ref[...]

    def rope(t):
        outs = []
        for h in range(DA_HEADS):
            th = t[:, h * LANES:(h + 1) * LANES]
            outs.append(th * cc + pltpu.roll(th, LANES - ROT_DIM // 2, 1) * s1
                        + pltpu.roll(th, ROT_DIM // 2, 1) * s2)
        return jnp.concatenate(outs, axis=1)

    def seg(a, b):
        return lax.dot_general(xb, w_ref[a:b, :], (((1,), (1,)), ((), ())), preferred_element_type=F32)

    q = rope(seg(_C_Q, _C_K)) * Q_SCALE
    k = rope(seg(_C_K, _C_V))
    v = seg(_C_V, _C_Z)
    if prompt:
        q_ref, kb_ref, vb_ref, kpg_ref, vrow_ref, z_ref, xbc_ref, pu_ref, dt_ref = out_refs
        kb_ref[...] = k.astype(BF16)
        vb_ref[...] = v.astype(BF16)
        for pg in range(kpg_ref.shape[0]):
            kpg_ref[pg] = k[pg * PAGE_SIZE:(pg + 1) * PAGE_SIZE, :].T
        for h in range(DA_HEADS):
            vrow_ref[pl.ds(h, v.shape[0], stride=DA_HEADS), :] = v[:, h * DA_DV:(h + 1) * DA_DV]
    else:
        q_ref, k_ref, v_ref, z_ref, xbc_ref, pu_ref, dt_ref = out_refs
        k_ref[...] = k
        v_ref[...] = v
    q_ref[...] = q
    z_ref[...] = seg(_C_Z, _C_X)
    xbc_ref[...] = seg(_C_X, _C_P)
    pu_ref[...] = seg(_C_P, _C_DT)
    dt_ref[...] = seg(_C_DT, _C_END)


def _proj_in(x, w_cat, tables, *, prompt, tm, seq, rows, row0=0):
    m = rows
    grid = (m // tm,)
    xrow = pl.BlockSpec((tm, D_MODEL), lambda i: (row0 + i, 0))

    def rows_out(w, dt):
        return pl.BlockSpec((tm, w), lambda i: (i, 0)), jax.ShapeDtypeStruct((m, w), dt)

    if prompt:
        npos = seq // tm
        ppt = tm // PAGE_SIZE
        tab = pl.BlockSpec((tm, LANES), lambda i: (i % npos, 0))
        outs = [rows_out(DA_WIDTH, F32), rows_out(DA_WIDTH, BF16), rows_out(DA_WIDTH, BF16),
                (pl.BlockSpec((ppt, DA_WIDTH, PAGE_SIZE), lambda i: (i, 0, 0)),
                 jax.ShapeDtypeStruct((m // PAGE_SIZE, DA_WIDTH, PAGE_SIZE), F32)),
                (pl.BlockSpec((tm * DA_HEADS, DA_DV), lambda i: (i, 0)),
                 jax.ShapeDtypeStruct((m * DA_HEADS, DA_DV), F32))]
    else:
        tab = _const_spec((1, LANES))
        outs = [rows_out(DA_WIDTH, F32), rows_out(DA_WIDTH, F32), rows_out(DA_WIDTH, F32)]
    outs += [rows_out(SSM_WIDTH, F32), rows_out(CONV_CH, F32), rows_out(POOL_WIDTH, F32), rows_out(DT_PAD, F32)]
    return pl.pallas_call(
        functools.partial(_proj_in_kernel, prompt=prompt),
        grid=grid,
        in_specs=[xrow, _const_spec(w_cat.shape), tab, tab, tab],
        out_specs=[o[0] for o in outs],
        out_shape=[o[1] for o in outs],
        compiler_params=_cparams(("parallel",)),
        name="proj_in",
    )(x, w_cat, *tables)


def _lam_value(lq1, lk1, lq2, lk2, lam_init):
    return (jnp.exp(jnp.sum(lq1 * lk1, axis=1, keepdims=True))
            - jnp.exp(jnp.sum(lq2 * lk2, axis=1, keepdims=True)) + lam_init)


def _flash_kernel(q_ref, k_ref, v_ref, lq1_ref, lk1_ref, lq2_ref, lk2_ref, nw_ref, o_ref,
                  vt_sc, s_sc, p_sc, m_sc, acc_sc, *, tq, tk, lam_init):
    i = pl.program_id(2)
    n_kt, aug, _ = vt_sc.shape

    @pl.when(i == 0)
    def _():
        ones = jnp.ones((aug - DA_DV, tk), BF16)
        for j in range(n_kt):
            vt = v_ref[j * tk:(j + 1) * tk, :].astype(F32).T.astype(BF16)
            vt_sc[j] = jnp.concatenate([vt, ones], axis=0)

    qt = q_ref[...].T
    drow = lax.broadcasted_iota(jnp.int32, qt.shape, 0)
    qts = (jnp.where(drow < DA_DK, qt, 0.0).astype(BF16), jnp.where(drow >= DA_DK, qt, 0.0).astype(BF16))

    def stage(j, cur, mask=None, last=False):
        nxt = 1 - cur
        if not last:
            kt = k_ref[pl.ds(pl.multiple_of((j + 1) * tk, tk), tk), :]
            for c in range(2):
                s_sc[nxt, c] = jnp.dot(kt, qts[c], preferred_element_type=F32)
        vta = vt_sc[jnp.maximum(j - 1, 0)]
        for c in range(2):
            pv = jnp.dot(vta, p_sc[cur, c], preferred_element_type=F32)
            s = s_sc[cur, c]
            if mask is not None:
                s = jnp.where(mask, s, NEG_INF)
            m = m_sc[c]
            m_new = jnp.maximum(m, jnp.max(s, axis=0, keepdims=True))
            p_sc[nxt, c] = jnp.exp2(s - m_new).astype(BF16)
            acc_sc[c] = jnp.exp2(m - m_new) * (acc_sc[c] + pv)
            m_sc[c] = m_new

    kt0 = k_ref[0:tk, :]
    for c in range(2):
        s_sc[0, c] = jnp.dot(kt0, qts[c], preferred_element_type=F32)
        p_sc[0, c] = jnp.zeros((tk, tq), BF16)
        m_sc[c] = jnp.full((1, tq), NEG_INF, F32)
        acc_sc[c] = jnp.zeros((aug, tq), F32)

    def pair(t, carry):
        stage(2 * t, 0)
        stage(2 * t + 1, 1)
        return carry

    lax.fori_loop(0, i // 2, pair, 0)
    odd = i % 2 == 1

    @pl.when(odd)
    def _():
        stage(i - 1, 0)

    krow = lax.broadcasted_iota(jnp.int32, (tk, tq), 0)
    qcol = lax.broadcasted_iota(jnp.int32, (tk, tq), 1)

    def finish(cur):
        stage(i, cur, mask=krow <= qcol, last=True)
        vta = vt_sc[i]
        a0, a1 = (acc_sc[c] + jnp.dot(vta, p_sc[1 - cur, c], preferred_element_type=F32) for c in range(2))
        lam = _lam_value(lq1_ref[...], lk1_ref[...], lq2_ref[...], lk2_ref[...], lam_init)
        o = a0[:DA_DV] * (1.0 / a0[DA_DV:DA_DV + 1]) - lam * (a1[:DA_DV] * (1.0 / a1[DA_DV:DA_DV + 1]))
        ms = jnp.mean(o * o, axis=0, keepdims=True)
        o = o * lax.rsqrt(ms + RMS_EPS) * nw_ref[...] * (1.0 - lam_init)
        o_ref[...] = o.T

    pl.when(odd)(lambda: finish(1))
    pl.when(jnp.logical_not(odd))(lambda: finish(0))


def _flash_attn(q, kb, v, lam_vecs, norm_w_col, *, batch, seq, tq, tk, lam_init):
    assert tq == tk
    q3 = q.reshape(batch, seq, DA_WIDTH)
    k3 = kb.reshape(batch, seq, DA_WIDTH)
    v3 = v.reshape(batch, seq, DA_WIDTH)
    qspec = pl.BlockSpec((None, tq, DA_DV), lambda b, h, i: (b, i, h))
    kvspec = pl.BlockSpec((None, seq, DA_DV), lambda b, h, i: (b, 0, h))
    vec = _const_spec((1, DA_DK))
    aug = DA_DV + 16
    out = pl.pallas_call(
        functools.partial(_flash_kernel, tq=tq, tk=tk, lam_init=lam_init),
        grid=(batch, DA_HEADS, seq // tq),
        in_specs=[qspec, kvspec, kvspec, vec, vec, vec, vec, _const_spec((DA_DV, 1))],
        out_specs=qspec,
        out_shape=jax.ShapeDtypeStruct((batch, seq, DA_WIDTH), F32),
        scratch_shapes=[pltpu.VMEM((seq // tk, aug, tk), BF16), pltpu.VMEM((2, 2, tk, tq), F32),
                        pltpu.VMEM((2, 2, tk, tq), BF16), pltpu.VMEM((2, 1, tq), F32),
                        pltpu.VMEM((2, aug, tq), F32)],
        compiler_params=_cparams(("parallel", "parallel", "arbitrary")),
        name="flash_diff_attn",
    )(q3, k3, v3, *lam_vecs, norm_w_col)
    return out.reshape(batch * seq, DA_WIDTH)


def _paged_kernel(pt_ref, q_ref, ks_ref, vs_ref, lq1_ref, lk1_ref, lq2_ref, lk2_ref, nw_ref, *rest,
                  n_pages, lam_init):
    k_refs = rest[:n_pages]
    v_refs = rest[n_pages:2 * n_pages]
    o_ref = rest[2 * n_pages]
    q = q_ref[...]
    nhc = 2 * DA_HEADS
    jj = lax.broadcasted_iota(jnp.int32, (2 * nhc, DA_WIDTH), 0)
    ll = lax.broadcasted_iota(jnp.int32, (2 * nhc, DA_WIDTH), 1)
    qblk = jnp.where((jj < nhc) & ((ll >> 6) == (jj & (DA_HEADS - 1)) * 2 + (jj >> 2)), q, 0.0)
    qb = qblk.astype(BF16)

    s_all = jnp.concatenate(
        [jnp.dot(qb, k_refs[j][...].astype(BF16), preferred_element_type=F32)[:nhc] for j in range(n_pages)],
        axis=1)
    s_self = jnp.sum(qblk[:nhc] * ks_ref[...], axis=1, keepdims=True)
    m = jnp.maximum(jnp.max(s_all, axis=1, keepdims=True), s_self)
    p = jnp.exp2(s_all - m)
    p_self = jnp.exp2(s_self - m)
    inv_l = 1.0 / (jnp.sum(p, axis=1, keepdims=True) + p_self)
    lam = _lam_value(lq1_ref[...], lk1_ref[...], lq2_ref[...], lk2_ref[...], lam_init)
    pn = p * inv_l
    pn_self = p_self * inv_l
    w = pn - lam * pltpu.roll(pn, DA_HEADS, 0)
    w_self = pn_self - lam * pltpu.roll(pn_self, DA_HEADS, 0)
    w16 = jnp.concatenate([w, jnp.zeros_like(w)], axis=0).astype(BF16)
    accs = [jnp.zeros((2 * nhc, DA_DV), F32) for _ in range(DA_HEADS)]
    for j in range(n_pages):
        wj = w16[:, j * PAGE_SIZE:(j + 1) * PAGE_SIZE]
        for h in range(DA_HEADS):
            vh = v_refs[j][pl.ds(h, PAGE_SIZE, stride=DA_HEADS), :]
            accs[h] = accs[h] + jnp.dot(wj, vh.astype(BF16), preferred_element_type=F32)
    outs = []
    for h in range(DA_HEADS):
        o = accs[h][h:h + 1, :] + w_self[h:h + 1, :] * vs_ref[:, h * DA_DV:(h + 1) * DA_DV]
        ms = jnp.mean(o * o, axis=1, keepdims=True)
        outs.append(o * lax.rsqrt(ms + RMS_EPS) * nw_ref[...] * (1.0 - lam_init))
    o_ref[...] = jnp.concatenate(outs, axis=1)


def _paged_attn(q, k_self, v_self, cache_k4, cache_v4, page_table, lam_vecs, norm_w4, *, layer, lam_init):
    bs, n_pages = page_table.shape
    pt = page_table.reshape(-1)
    one = pl.BlockSpec((None, 1, DA_WIDTH), lambda b, pt: (b, 0, 0))
    vec = pl.BlockSpec((1, DA_DK), lambda b, pt: (0, 0))

    def page_spec(j):
        return pl.BlockSpec((None, None, DA_WIDTH, PAGE_SIZE), lambda b, pt: (layer, pt[b * n_pages + j], 0, 0))

    specs = [one, one, one, vec, vec, vec, vec, pl.BlockSpec((1, DA_DV), lambda b, pt: (0, 0))]
    specs += [page_spec(j) for j in range(n_pages)] * 2
    out = pl.pallas_call(
        functools.partial(_paged_kernel, n_pages=n_pages, lam_init=lam_init),
        grid_spec=pltpu.PrefetchScalarGridSpec(
            num_scalar_prefetch=1, grid=(bs,), in_specs=specs, out_specs=one),
        out_shape=jax.ShapeDtypeStruct((bs, 1, DA_WIDTH), F32),
        compiler_params=_cparams(("arbitrary",)),
        name="paged_diff_attn",
    )(pt, q.reshape(bs, 1, DA_WIDTH), k_self.reshape(bs, 1, DA_WIDTH), v_self.reshape(bs, 1, DA_WIDTH),
      *lam_vecs, norm_w4, *([cache_k4] * n_pages), *([cache_v4] * n_pages))
    return out.reshape(bs, DA_WIDTH)


def _head_expand(cols, width, per):
    rows = cols.shape[0]
    lane = lax.broadcasted_iota(jnp.int32, (rows, width), 1)
    out = jnp.broadcast_to(cols[:, 0:1], (rows, width))
    for h in range(1, width // per):
        out = jnp.where(lane >= h * per, cols[:, h:h + 1], out)
    return out


def _ssd_kernel(xbc_ref, z_ref, dt_ref, cw_ref, cb_ref, dtb_ref, alog_ref, dsk_ref, nw_ref,
                y_ref, st_ref, tail_sc, state_sc):
    c = pl.program_id(0)

    @pl.when(c == 0)
    def _():
        tail_sc[...] = jnp.zeros_like(tail_sc)
        state_sc[...] = jnp.zeros_like(state_sc)

    for b in range(xbc_ref.shape[0]):
        state = _ssd_chunk(xbc_ref.at[b], z_ref.at[b], dt_ref.at[b], cw_ref, cb_ref, dtb_ref, alog_ref, dsk_ref,
                           nw_ref, y_ref.at[b], tail_sc.at[b], state_sc.at[b])

        @pl.when(c == pl.num_programs(0) - 1)
        def _():
            st_ref[b] = state


def _ssd_chunk(xbc_ref, z_ref, dt_ref, cw_ref, cb_ref, dtb_ref, alog_ref, dsk_ref, nw_ref, y_ref, tail_sc,
               state_sc):
    q = SSM_CHUNK
    u = xbc_ref[...]
    full = jnp.concatenate([tail_sc[...], u], axis=0)
    tail_sc[...] = u[q - 8:, :]
    conv = full * cw_ref[SSM_CONV - 1:SSM_CONV, :]
    for j in range(1, SSM_CONV):
        conv = conv + pltpu.roll(full, j, 0) * cw_ref[SSM_CONV - 1 - j:SSM_CONV - j, :]
    xbc = _silu(conv[8:, :] + cb_ref[...])
    xs = xbc[:, :SSM_WIDTH]
    bm = (xbc[:, SSM_WIDTH:SSM_WIDTH + SSM_STATE], xbc[:, SSM_WIDTH + SSM_STATE:SSM_WIDTH + 2 * SSM_STATE])
    cm = (xbc[:, SSM_WIDTH + 2 * SSM_STATE:SSM_WIDTH + 3 * SSM_STATE], xbc[:, SSM_WIDTH + 3 * SSM_STATE:])

    dt = _softplus(dt_ref[...] + dtb_ref[...])
    dta = dt * (-jnp.exp(alog_ref[...]))
    ti = lax.broadcasted_iota(jnp.int32, (q, q), 0)
    si = lax.broadcasted_iota(jnp.int32, (q, q), 1)
    causal = si <= ti
    tri = jnp.where(causal, 1.0, 0.0)
    a_cs = jnp.dot(tri, dta, precision=lax.Precision.HIGHEST, preferred_element_type=F32)
    a_cs_t = a_cs.T
    last = a_cs[q - 1:q, :]

    dt_x = _head_expand(dt, SSM_WIDTH, SSM_HEADDIM)
    xdt = xs * dt_x
    xdt_end = xdt * _head_expand(jnp.exp(last - a_cs), SSM_WIDTH, SSM_HEADDIM)
    lane = lax.broadcasted_iota(jnp.int32, (q, SSM_WIDTH), 1)

    cb = [_dot_nt(cm[g], bm[g]) for g in range(2)]
    y_diag = jnp.zeros((q, SSM_WIDTH), F32)
    for h in range(SSM_HEADS):
        seg = a_cs[:, h:h + 1] - a_cs_t[h:h + 1, :]
        decay = jnp.exp(jnp.where(causal, seg, NEG_INF))
        xh = jnp.where((lane >= h * SSM_HEADDIM) & (lane < (h + 1) * SSM_HEADDIM), xdt, 0.0)
        y_diag = y_diag + _dot(cb[h // 2] * decay, xh)

    state = state_sc[...]
    sb = state.astype(BF16)
    y_off = jnp.where(lane < 2 * SSM_HEADDIM, _dot_nt(cm[0], sb), _dot_nt(cm[1], sb))
    y_off = y_off * _head_expand(jnp.exp(a_cs), SSM_WIDTH, SSM_HEADDIM)

    xt = xdt_end.T
    rowi = lax.broadcasted_iota(jnp.int32, (SSM_WIDTH, SSM_STATE), 0)
    new = jnp.where(rowi < 2 * SSM_HEADDIM, _dot(xt, bm[0]), _dot(xt, bm[1]))
    e_last = jnp.exp(last)
    dec_rows = jnp.broadcast_to(e_last[:, 0:1], (SSM_WIDTH, SSM_STATE))
    for h in range(1, SSM_HEADS):
        dec_rows = jnp.where(rowi >= h * SSM_HEADDIM, e_last[:, h:h + 1], dec_rows)
    state = state * dec_rows + new
    state_sc[...] = state

    y = y_diag + y_off + dsk_ref[...] * xs
    y = y * _silu(z_ref[...])
    ms = jnp.mean(y * y, axis=1, keepdims=True)
    y_ref[...] = y * lax.rsqrt(ms + RMS_EPS) * nw_ref[...]
    return state


def _ssd_prompt(xbc, z, dt_raw, conv_w, conv_b, dt_bias, a_log, d_skip_x, norm_w, *, batch, seq):
    nc = seq // SSM_CHUNK
    blk = lambda w: pl.BlockSpec((batch, SSM_CHUNK, w), lambda c: (0, c, 0))
    y, st = pl.pallas_call(
        _ssd_kernel,
        grid=(nc,),
        in_specs=[blk(CONV_CH), blk(SSM_WIDTH), blk(DT_PAD),
                  _const_spec((SSM_CONV, CONV_CH)), _const_spec((1, CONV_CH)),
                  _const_spec((1, DT_PAD)), _const_spec((1, DT_PAD)),
                  _const_spec((1, SSM_WIDTH)), _const_spec((1, SSM_WIDTH))],
        out_specs=[blk(SSM_WIDTH), pl.BlockSpec((batch, SSM_WIDTH, SSM_STATE), lambda c: (0, 0, 0))],
        out_shape=[jax.ShapeDtypeStruct((batch, seq, SSM_WIDTH), F32),
                   jax.ShapeDtypeStruct((batch, SSM_WIDTH, SSM_STATE), F32)],
        scratch_shapes=[pltpu.VMEM((batch, 8, CONV_CH), F32), pltpu.VMEM((batch, SSM_WIDTH, SSM_STATE), F32)],
        compiler_params=_cparams(("arbitrary",)),
        name="ssd_prompt",
    )(xbc.reshape(batch, seq, CONV_CH), z.reshape(batch, seq, SSM_WIDTH), dt_raw.reshape(batch, seq, DT_PAD),
      conv_w, conv_b, dt_bias, a_log, d_skip_x, norm_w)
    return y.reshape(batch * seq, SSM_WIDTH), st.reshape(batch, SSM_HEADS, SSM_HEADDIM, SSM_STATE)


def _pool_select(sums, inv_cnt, lane):
    out = sums[POOL_WINDOWS[0]] * inv_cnt[POOL_WINDOWS[0]]
    for g in range(1, len(POOL_WINDOWS)):
        w = POOL_WINDOWS[g]
        out = jnp.where(lane >= g * POOL_GDIM, sums[w] * inv_cnt[w], out)
    return out


def _pool_kernel(u_ref, w_ref, sc_ref, y_ref, tail_sc, *, tp):
    c = pl.program_id(1)
    halo = 16

    @pl.when(c == 0)
    def _():
        tail_sc[...] = jnp.zeros_like(tail_sc)

    u = u_ref[...]
    full = jnp.concatenate([tail_sc[...], u], axis=0)
    tail_sc[...] = u[tp - halo:, :]
    sums = {}
    run = full
    w = 1
    while w < max(POOL_WINDOWS):
        run = run + pltpu.roll(run, w, 0)
        w *= 2
        sums[w] = run[halo:, :]
    pos = c * tp + lax.broadcasted_iota(jnp.int32, (tp, 1), 0)
    inv_cnt = {w: 1.0 / jnp.minimum(w, pos + 1).astype(F32) for w in POOL_WINDOWS}
    lane = lax.broadcasted_iota(jnp.int32, (tp, POOL_WIDTH), 1)
    d = _pool_select(sums, inv_cnt, lane) - u
    y_ref[...] = _dot(d, w_ref[...]) * sc_ref[...]


def _pool_prompt(pu, pool_wbd, pool_scale, *, batch, seq, tp):
    blk = pl.BlockSpec((None, tp, POOL_WIDTH), lambda b, c: (b, c, 0))
    y = pl.pallas_call(
        functools.partial(_pool_kernel, tp=tp),
        grid=(batch, seq // tp),
        in_specs=[blk, _const_spec((POOL_WIDTH, POOL_WIDTH)), _const_spec((1, POOL_WIDTH))],
        out_specs=blk,
        out_shape=jax.ShapeDtypeStruct((batch, seq, POOL_WIDTH), F32),
        scratch_shapes=[pltpu.VMEM((16, POOL_WIDTH), F32)],
        compiler_params=_cparams(("parallel", "arbitrary")),
        name="pool_prompt",
    )(pu.reshape(batch, seq, POOL_WIDTH), pool_wbd, pool_scale)
    return y.reshape(batch * seq, POOL_WIDTH)


def _sample_prep_kernel(xbc_ref, cbuf_ref, cw_ref, cb_ref, dt_ref, dtb_ref, alog_ref, pu_ref, pbuf_ref,
                        pw_ref, psc_ref, xact_ref, dto_ref, dec_ref, yp_ref, cnew_ref, pnew_ref, *, pos0):
    u = xbc_ref[...]
    for j in range(SSM_CONV - 2):
        cnew_ref[j] = cbuf_ref[j + 1]
    cnew_ref[SSM_CONV - 2] = u
    for j in range(POOL_BUF - 1):
        pnew_ref[j] = pbuf_ref[j + 1]
    pnew_ref[POOL_BUF - 1] = pu_ref[...]
    conv = u * cw_ref[SSM_CONV - 1:SSM_CONV, :] + cb_ref[...]
    for j in range(SSM_CONV - 1):
        conv = conv + cbuf_ref[j] * cw_ref[j:j + 1, :]
    xact_ref[...] = _silu(conv)
    dt = _softplus(dt_ref[...] + dtb_ref[...])
    dto_ref[...] = dt
    dec_ref[...] = jnp.exp(dt * (-jnp.exp(alog_ref[...])))
    pu = pu_ref[...]
    run = pu
    sums = {}
    for j in range(1, max(POOL_WINDOWS)):
        run = run + pbuf_ref[POOL_BUF - j]
        if j + 1 in POOL_WINDOWS:
            sums[j + 1] = run
    inv_cnt = {w: 1.0 / float(min(w, pos0 + 1)) for w in POOL_WINDOWS}
    lane = lax.broadcasted_iota(jnp.int32, pu.shape, 1)
    d = _pool_select(sums, inv_cnt, lane) - pu
    yp_ref[...] = _dot(d, pw_ref[...]) * psc_ref[...]


def _sample_prep(xbc, conv_buf, conv_w, conv_b, dt_raw, dt_bias, a_log, pu, pool_buf, pool_wbd, pool_scale, *, pos0):
    bs = xbc.shape[0]
    args = (xbc, conv_buf, conv_w, conv_b, dt_raw, dt_bias, a_log, pu, pool_buf, pool_wbd, pool_scale)
    shapes = [(bs, CONV_CH), (bs, DT_PAD), (bs, DT_PAD), (bs, POOL_WIDTH), conv_buf.shape, pool_buf.shape]
    return pl.pallas_call(
        functools.partial(_sample_prep_kernel, pos0=pos0),
        grid=(1,),
        in_specs=[_const_spec(a.shape) for a in args],
        out_specs=[_const_spec(s) for s in shapes],
        out_shape=[jax.ShapeDtypeStruct(s, F32) for s in shapes],
        compiler_params=_cparams(("arbitrary",)),
        name="sample_prep",
    )(*args)


def _ssd_step_kernel(s_ref, x_ref, z_ref, dt_ref, dec_ref, b_ref, c_ref, dsk_ref, nw_ref, y_ref, so_ref, *, group):
    rowi = lax.broadcasted_iota(jnp.int32, (SSM_WIDTH, SSM_STATE), 0)
    for i in range(group):
        brow, crow = b_ref[i], c_ref[i]
        bsel = jnp.where(rowi < 2 * SSM_HEADDIM, brow[:, :SSM_STATE], brow[:, SSM_STATE:])
        csel = jnp.where(rowi < 2 * SSM_HEADDIM, crow[:, :SSM_STATE], crow[:, SSM_STATE:])
        x = x_ref[i]
        sn = s_ref[i] * dec_ref[i] + (x * dt_ref[i]) * bsel
        so_ref[i] = sn
        y = jnp.sum(sn * csel, axis=1, keepdims=True) + dsk_ref[...] * x
        y = y * _silu(z_ref[i])
        ms = jnp.sum(y * y, axis=0, keepdims=True) * (1.0 / SSM_WIDTH)
        y_ref[i] = y * lax.rsqrt(ms + RMS_EPS) * nw_ref[...]


def _ssd_step(state_all, layer, xact, z, dt, dec, d_skip_x, norm_w, *, group=8):
    bs = xact.shape[0]
    nb = bs // group
    rep = lambda t: jnp.repeat(t[:, :SSM_HEADS], SSM_HEADDIM, axis=1).reshape(bs, SSM_WIDTH, 1)
    colb = pl.BlockSpec((group, SSM_WIDTH, 1), lambda b: (b, 0, 0))
    rowb = pl.BlockSpec((group, 1, 2 * SSM_STATE), lambda b: (b, 0, 0))
    st_in = pl.BlockSpec((group, SSM_WIDTH, SSM_STATE), lambda b: (layer * nb + b, 0, 0))
    st_out = pl.BlockSpec((group, SSM_WIDTH, SSM_STATE), lambda b: (b, 0, 0))
    y, st = pl.pallas_call(
        functools.partial(_ssd_step_kernel, group=group),
        grid=(nb,),
        in_specs=[st_in, colb, colb, colb, colb, rowb, rowb,
                  _const_spec((SSM_WIDTH, 1)), _const_spec((SSM_WIDTH, 1))],
        out_specs=[colb, st_out],
        out_shape=[jax.ShapeDtypeStruct((bs, SSM_WIDTH, 1), F32),
                   jax.ShapeDtypeStruct((bs, SSM_WIDTH, SSM_STATE), F32)],
        compiler_params=_cparams(("parallel",)),
        name="ssd_step",
    )(state_all, xact[:, :SSM_WIDTH].reshape(bs, SSM_WIDTH, 1), z.reshape(bs, SSM_WIDTH, 1), rep(dt), rep(dec),
      xact[:, SSM_WIDTH:SSM_WIDTH + 2 * SSM_STATE].reshape(bs, 1, 2 * SSM_STATE),
      xact[:, SSM_WIDTH + 2 * SSM_STATE:].reshape(bs, 1, 2 * SSM_STATE),
      d_skip_x.reshape(SSM_WIDTH, 1), norm_w.reshape(SSM_WIDTH, 1))
    return y.reshape(bs, SSM_WIDTH), st.reshape(bs, SSM_HEADS, SSM_HEADDIM, SSM_STATE)


def _mm_kernel(x_ref, w_ref, *out_refs, scale):
    y = _dot(x_ref[...], w_ref[...])
    if scale != 1.0:
        y = y * scale
    for r in out_refs:
        r[...] = y.astype(r.dtype)


def _matmul(x, w, *, tm, scale=1.0, dtypes=(F32,)):
    m, k = x.shape
    n = w.shape[1]
    res = pl.pallas_call(
        functools.partial(_mm_kernel, scale=scale),
        grid=(m // tm,),
        in_specs=[pl.BlockSpec((tm, k), lambda i: (i, 0)), _const_spec(w.shape)],
        out_specs=[pl.BlockSpec((tm, n), lambda i: (i, 0)) for _ in dtypes],
        out_shape=[jax.ShapeDtypeStruct((m, n), d) for d in dtypes],
        compiler_params=_cparams(("parallel",)),
        name="matmul",
    )(x, w)
    return res


def _mm_res_ln_kernel(*refs, n_in):
    a_refs = refs[:n_in]
    w_refs = refs[n_in:2 * n_in]
    x_ref, g_ref, b_ref, o_ref = refs[2 * n_in:]
    h = _dot(a_refs[0][...], w_refs[0][...])
    for a, w in zip(a_refs[1:], w_refs[1:]):
        h = h + _dot(a[...], w[...])
    o_ref[...] = _layer_norm(ALPHA * x_ref[...] + h, g_ref[...], b_ref[...])


def _mm_res_ln(acts, weights, x, g, b, *, tm, x_row0=0):
    m = acts[0].shape[0]
    row = lambda a: pl.BlockSpec((tm, a.shape[1]), lambda i: (i, 0))
    return pl.pallas_call(
        functools.partial(_mm_res_ln_kernel, n_in=len(acts)),
        grid=(m // tm,),
        in_specs=[row(a) for a in acts] + [_const_spec(w.shape) for w in weights]
        + [pl.BlockSpec((tm, D_MODEL), lambda i: (x_row0 + i, 0)), _const_spec(g.shape), _const_spec(b.shape)],
        out_specs=pl.BlockSpec((tm, D_MODEL), lambda i: (i, 0)),
        out_shape=jax.ShapeDtypeStruct((m, D_MODEL), F32),
        compiler_params=_cparams(("parallel",)),
        name="mm_res_ln",
    )(*acts, *weights, x, g, b)


def _mem_prompt_kernel(x_ref, mk_ref, mv_ref, wq_ref, wo_ref, g_ref, b_ref, o_ref):
    x = x_ref[...]
    q = (_dot(x, wq_ref[...]) * (MEM_HD ** -0.5)).astype(BF16)
    outs = []
    for h in range(MEM_HEADS):
        sl = slice(h * MEM_HD, (h + 1) * MEM_HD)
        s = lax.dot_general(q[:, sl], mk_ref[:, sl], (((1,), (1,)), ((), ())), preferred_element_type=F32)
        p = jnp.exp(s - jnp.max(s, axis=1, keepdims=True))
        p = p * (1.0 / jnp.sum(p, axis=1, keepdims=True))
        outs.append(jnp.dot(p.astype(BF16), mv_ref[:, sl], preferred_element_type=F32))
    o = jnp.concatenate(outs, axis=1)
    o_ref[...] = _layer_norm(ALPHA * x + _dot(o, wo_ref[...]), g_ref[...], b_ref[...])


def _mem_attn_prompt(x, mkb, mvb, wq, wo, g, b, *, batch, seq, tm):
    n_mem = mkb.shape[0] // batch
    nt = seq // tm
    xblk = pl.BlockSpec((tm, D_MODEL), lambda bb, i: (bb * nt + i, 0))
    mblk = pl.BlockSpec((None, n_mem, D_MODEL), lambda bb, i: (bb, 0, 0))
    return pl.pallas_call(
        _mem_prompt_kernel,
        grid=(batch, nt),
        in_specs=[xblk, mblk, mblk, _const_spec(wq.shape), _const_spec(wo.shape),
                  _const_spec(g.shape), _const_spec(b.shape)],
        out_specs=xblk,
        out_shape=jax.ShapeDtypeStruct((batch * seq, D_MODEL), F32),
        compiler_params=_cparams(("parallel", "parallel")),
        name="mem_attn_prompt",
    )(x, mkb.reshape(batch, n_mem, D_MODEL), mvb.reshape(batch, n_mem, D_MODEL), wq, wo, g, b)


def _mem_sample_kernel(q_ref, mk_ref, mv_ref, o_ref, *, group):
    rows = mk_ref.shape[1] // 8
    for i in range(group):
        q8 = q_ref[i]
        r = jnp.sum(mk_ref[i].reshape(rows, 8, LANES) * q8, axis=2, keepdims=True)
        s = r + pltpu.roll(r, MEM_HEADS, 1)
        p = jnp.exp(s - jnp.max(s, axis=0, keepdims=True))
        p = p * (1.0 / jnp.sum(p, axis=0, keepdims=True))
        o_ref[i] = jnp.sum(mv_ref[i].reshape(rows, 8, LANES) * p, axis=0)


def _mem_rows(cache, n):
    n_mem = cache.shape[2]
    c = cache.reshape(n, n_mem, MEM_HEADS, MEM_HD // LANES, LANES)
    return jnp.transpose(c, (0, 1, 3, 2, 4)).reshape(n, n_mem * 2 * MEM_HEADS, LANES)


def _mem_attn_sample(q, mem_k_rows, mem_v_rows, *, layer, group):
    bs = q.shape[0]
    nb = bs // group
    halves = MEM_HD // LANES
    q8 = jnp.transpose(q.reshape(bs, MEM_HEADS, halves, LANES), (0, 2, 1, 3)).reshape(bs, halves * MEM_HEADS, LANES)
    qblk = pl.BlockSpec((group, halves * MEM_HEADS, LANES), lambda i: (i, 0, 0))
    mblk = pl.BlockSpec((group,) + mem_k_rows.shape[1:], lambda i: (layer * nb + i, 0, 0))
    out = pl.pallas_call(
        functools.partial(_mem_sample_kernel, group=group),
        grid=(nb,),
        in_specs=[qblk, mblk, mblk],
        out_specs=qblk,
        out_shape=jax.ShapeDtypeStruct(q8.shape, F32),
        compiler_params=_cparams(("parallel",)),
        name="mem_attn_sample",
    )(q8, mem_k_rows, mem_v_rows)
    return jnp.transpose(out.reshape(bs, halves, MEM_HEADS, LANES), (0, 2, 1, 3)).reshape(bs, D_MODEL)


def _ffn_kernel(x_ref, w1_ref, w3_ref, w2_ref, g_ref, b_ref, o_ref, *, fc):
    x = x_ref[...]
    xb = x.astype(BF16)
    acc = jnp.zeros(x.shape, F32)
    for c in range(w1_ref.shape[1] // fc):
        sl = slice(c * fc, (c + 1) * fc)
        h = _silu(jnp.dot(xb, w1_ref[:, sl], preferred_element_type=F32)) * jnp.dot(xb, w3_ref[:, sl],
                                                                                    preferred_element_type=F32)
        acc = acc + jnp.dot(h.astype(BF16), w2_ref[sl, :], preferred_element_type=F32)
    o_ref[...] = _layer_norm(ALPHA * x + acc, g_ref[...], b_ref[...])


def _ffn(x, w1, w3, w2, g, b, *, tm, fc):
    m = x.shape[0]
    row = pl.BlockSpec((tm, D_MODEL), lambda i: (i, 0))
    wspec = lambda w: pl.BlockSpec(w.shape, lambda i: (0, 0), pipeline_mode=pl.Buffered(1))
    return pl.pallas_call(
        functools.partial(_ffn_kernel, fc=fc),
        grid=(m // tm,),
        in_specs=[row, wspec(w1), wspec(w3), wspec(w2), _const_spec(g.shape), _const_spec(b.shape)],
        out_specs=row,
        out_shape=jax.ShapeDtypeStruct(x.shape, F32),
        compiler_params=_cparams(("parallel",), vmem_mb=56),
        name="ffn_dense",
    )(x, w1, w3, w2, g, b)


def _router_kernel(x_ref, wh_ref, wl_ref, b_ref, o_ref):
    x = x_ref[...]
    xh = x.astype(BF16)
    xl = (x - xh.astype(F32)).astype(BF16)
    wh = wh_ref[...]
    logits = (jnp.dot(xh, wh, preferred_element_type=F32) + jnp.dot(xl, wh, preferred_element_type=F32)
              + jnp.dot(xh, wl_ref[...], preferred_element_type=F32))
    lane = lax.broadcasted_iota(jnp.int32, logits.shape, 1)
    logits = jnp.where(lane < N_EXPERTS, logits + b_ref[...], NEG_INF)
    m1 = jnp.max(logits, axis=1, keepdims=True)
    i1 = jnp.min(jnp.where(logits == m1, lane, LANES), axis=1, keepdims=True)
    rest = jnp.where(lane == i1, NEG_INF, logits)
    m2 = jnp.max(rest, axis=1, keepdims=True)
    i2 = jnp.min(jnp.where(rest == m2, lane, LANES), axis=1, keepdims=True)
    e = jnp.exp(m2 - m1)
    g1 = 1.0 / (1.0 + e)
    g2 = e * g1
    out = jnp.where(lane == 0, i1.astype(F32), jnp.where(lane == 1, i2.astype(F32),
                    jnp.where(lane == 2, g1, jnp.where(lane == 3, g2, 0.0))))
    o_ref[...] = out


def _router(x, w_hi, w_lo, b_pad, *, tm):
    m = x.shape[0]
    return pl.pallas_call(
        _router_kernel,
        grid=(m // tm,),
        in_specs=[pl.BlockSpec((tm, D_MODEL), lambda i: (i, 0)), _const_spec(w_hi.shape), _const_spec(w_lo.shape),
                  _const_spec(b_pad.shape)],
        out_specs=pl.BlockSpec((tm, LANES), lambda i: (i, 0)),
        out_shape=jax.ShapeDtypeStruct((m, LANES), F32),
        compiler_params=_cparams(("parallel",)),
        name="moe_router",
    )(x, w_hi, w_lo, b_pad)


def _expert_kernel(te_ref, tr_ref, x_ref, w1_ref, w3_ref, w2_ref, o_ref, acc_sc, *, sub):
    i = pl.program_id(0)
    j = pl.program_id(1)
    rows = tr_ref[i]
    tm = x_ref.shape[0]

    @pl.when(j == 0)
    def _():
        acc_sc[...] = jnp.zeros_like(acc_sc)

    def swiglu_into_acc(sl):
        xb = x_ref[sl, :].astype(BF16)
        w1 = w1_ref[...].astype(BF16)
        w3 = w3_ref[...].astype(BF16)
        h = _silu(jnp.dot(xb, w1, preferred_element_type=F32)) * jnp.dot(xb, w3, preferred_element_type=F32)
        acc_sc[sl, :] += jnp.dot(h.astype(BF16), w2_ref[...].astype(BF16), preferred_element_type=F32)

    @pl.when(rows == tm)
    def _():
        swiglu_into_acc(slice(None))

    @pl.when((rows > 0) & (rows < tm))
    def _():
        for s in range(tm // sub):
            pl.when(rows > s * sub)(functools.partial(swiglu_into_acc, slice(s * sub, (s + 1) * sub)))

    @pl.when(j == pl.num_programs(1) - 1)
    def _():
        o_ref[...] = acc_sc[...]


def _experts(x_sorted, tile_expert, tile_rows, w1, w3, w2, *, tm, sub, fc):
    n_slots = x_sorted.shape[0]
    nj = w1.shape[2] // fc
    last = nj - 1

    def jj(i, j, tr):
        return jnp.where(tr[i] > 0, j, last)

    xs = pl.BlockSpec((tm, D_MODEL), lambda i, j, te, tr: (i, 0))
    w13 = pl.BlockSpec((None, D_MODEL, fc), lambda i, j, te, tr: (te[i], 0, jj(i, j, tr)))
    w2s = pl.BlockSpec((None, fc, D_MODEL), lambda i, j, te, tr: (te[i], jj(i, j, tr), 0))
    return pl.pallas_call(
        functools.partial(_expert_kernel, sub=sub),
        grid_spec=pltpu.PrefetchScalarGridSpec(
            num_scalar_prefetch=2, grid=(n_slots // tm, nj),
            in_specs=[xs, w13, w13, w2s], out_specs=xs,
            scratch_shapes=[pltpu.VMEM((tm, D_MODEL), F32)]),
        out_shape=jax.ShapeDtypeStruct((n_slots, D_MODEL), F32),
        compiler_params=_cparams(("arbitrary", "arbitrary"), vmem_mb=56),
        name="moe_experts",
    )(tile_expert, tile_rows, x_sorted, w1, w3, w2)


def _dispatch_kernel(slot_ref, x_ref, init_ref, xs_ref, sem, *, tm, tok0):
    del init_ref
    base = (tok0 + pl.program_id(0) * tm) * 2

    def issue(r, carry):
        for k in range(2):
            dst = slot_ref[base + 2 * r + k]
            pltpu.make_async_copy(x_ref.at[pl.ds(r, 1), :], xs_ref.at[pl.ds(dst, 1), :], sem).start()
        return carry

    lax.fori_loop(0, tm, issue, 0, unroll=8)
    for _ in range(2):
        pltpu.make_async_copy(x_ref, x_ref, sem).wait()


def _dispatch(x, slot, buf, *, tm, tok0):
    t = x.shape[0]
    return pl.pallas_call(
        functools.partial(_dispatch_kernel, tm=tm, tok0=tok0),
        grid_spec=pltpu.PrefetchScalarGridSpec(
            num_scalar_prefetch=1, grid=(t // tm,),
            in_specs=[pl.BlockSpec((tm, D_MODEL), lambda i, s: (i, 0)), pl.BlockSpec(memory_space=pl.ANY)],
            out_specs=pl.BlockSpec(memory_space=pl.ANY),
            scratch_shapes=[pltpu.SemaphoreType.DMA(())]),
        out_shape=jax.ShapeDtypeStruct(buf.shape, F32),
        input_output_aliases={2: 0},
        compiler_params=_cparams(("arbitrary",)),
        name="moe_dispatch",
    )(slot, x, buf)


def _combine_ln_kernel(slot_ref, x_ref, r_ref, y_ref, g_ref, b_ref, o_ref, ybuf, sems, *, tm, tok0):
    i = pl.program_id(0)
    cur = i % 2

    def issue(step, half):
        base = (tok0 + step * tm) * 2

        def body(r, carry):
            for k in range(2):
                src = slot_ref[base + 2 * r + k]
                pltpu.make_async_copy(y_ref.at[pl.ds(src, 1), :], ybuf.at[half, k, pl.ds(r, 1), :],
                                      sems.at[half]).start()
            return carry

        lax.fori_loop(0, tm, body, 0, unroll=8)

    @pl.when(i == 0)
    def _():
        issue(0, 0)

    @pl.when(i + 1 < pl.num_programs(0))
    def _():
        issue(i + 1, 1 - cur)

    pltpu.make_async_copy(ybuf.at[cur], ybuf.at[cur], sems.at[cur]).wait()
    r = r_ref[...]
    f = r[:, 2:3] * ybuf[cur, 0] + r[:, 3:4] * ybuf[cur, 1]
    o_ref[...] = _layer_norm(ALPHA * x_ref[...] + f, g_ref[...], b_ref[...])


def _combine_ln(x, y_sorted, slot, route, g, b, *, tm, tok0):
    rows = x.shape[0]
    row = lambda w: pl.BlockSpec((tm, w), lambda i, s: (i, 0))
    const = lambda a: pl.BlockSpec(a.shape, lambda i, s: (0,) * a.ndim)
    return pl.pallas_call(
        functools.partial(_combine_ln_kernel, tm=tm, tok0=tok0),
        grid_spec=pltpu.PrefetchScalarGridSpec(
            num_scalar_prefetch=1, grid=(rows // tm,),
            in_specs=[row(D_MODEL), row(LANES), pl.BlockSpec(memory_space=pl.ANY), const(g), const(b)],
            out_specs=row(D_MODEL),
            scratch_shapes=[pltpu.VMEM((2, 2, tm, D_MODEL), F32), pltpu.SemaphoreType.DMA((2,))]),
        out_shape=jax.ShapeDtypeStruct((rows, D_MODEL), F32),
        compiler_params=_cparams(("arbitrary",)),
        name="moe_combine_ln",
    )(slot, x, route, y_sorted, g, b)


def _moe(xp, xs, router_w, router_b, w1, w3, w2, g, b, *, tm_e, sub, fc, tm_p):
    n_p, n_s = xp.shape[0], xs.shape[0]
    t = n_p + n_s
    w_pad = jnp.zeros((D_MODEL, LANES), F32).at[:, :N_EXPERTS].set(router_w)
    w_hi = w_pad.astype(BF16)
    w_lo = (w_pad - w_hi.astype(F32)).astype(BF16)
    b_pad = jnp.zeros((1, LANES), F32).at[0, :N_EXPERTS].set(router_b)
    route_p = _router(xp, w_hi, w_lo, b_pad, tm=tm_p)
    route_s = _router(xs, w_hi, w_lo, b_pad, tm=n_s)
    top_i = jnp.concatenate([route_p[:, :2], route_s[:, :2]], axis=0).astype(jnp.int32).reshape(-1)
    onehot = (top_i[:, None] == jnp.arange(N_EXPERTS)[None, :]).astype(jnp.int32)
    csum = jnp.cumsum(onehot, axis=0)
    counts = csum[-1]
    rank = jnp.sum((csum - onehot) * onehot, axis=1)
    padded = ((counts + tm_e - 1) // tm_e) * tm_e
    ends = jnp.cumsum(padded)
    starts = ends - padded
    slot = (starts[top_i] + rank).astype(jnp.int32)
    n_tiles = (2 * t + N_EXPERTS * (tm_e - 1)) // tm_e + 1
    tile_start = jnp.arange(n_tiles, dtype=jnp.int32) * tm_e
    tile_expert = jnp.sum((tile_start[:, None] >= ends[None, :]).astype(jnp.int32), axis=1)
    tile_expert = jnp.minimum(tile_expert, N_EXPERTS - 1)
    tile_rows = jnp.clip((starts + counts)[tile_expert] - tile_start, 0, tm_e)
    tile_rows = jnp.where(tile_start < ends[-1], tile_rows, 0).astype(jnp.int32)
    last_e = tile_expert[jnp.maximum(ends[-1] // tm_e - 1, 0)]
    tile_expert = jnp.where(tile_rows > 0, tile_expert, last_e).astype(jnp.int32)
    x_sorted = jnp.zeros((n_tiles * tm_e, D_MODEL), F32)
    x_sorted = _dispatch(xp, slot, x_sorted, tm=tm_p, tok0=0)
    x_sorted = _dispatch(xs, slot, x_sorted, tm=n_s, tok0=n_p)
    y_sorted = _experts(x_sorted, tile_expert, tile_rows, w1, w3, w2, tm=tm_e, sub=sub, fc=fc)
    return (_combine_ln(xp, y_sorted, slot, route_p, g, b, tm=tm_p // 2, tok0=0),
            _combine_ln(xs, y_sorted, slot, route_s, g, b, tm=n_s, tok0=n_p))


def _pack_w_in(w_in_t):
    q, k, v, z, xbc, dt, pu = jnp.split(w_in_t, np.cumsum([512, 512, 512, 256, 768, 4]).tolist(), axis=0)
    pad = jnp.zeros((DT_PAD - SSM_HEADS, D_MODEL), w_in_t.dtype)
    return jnp.concatenate([q, k, v, z, xbc, pu, dt, pad], axis=0).astype(BF16)


def _block_diag(pool_w):
    out = jnp.zeros((POOL_WIDTH, POOL_WIDTH), F32)
    for g in range(len(POOL_WINDOWS)):
        out = out.at[g * POOL_GDIM:(g + 1) * POOL_GDIM, g * POOL_GDIM:(g + 1) * POOL_GDIM].set(pool_w[g])
    return out.astype(BF16)


def _pad_heads(v):
    return jnp.zeros((1, DT_PAD), F32).at[0, :SSM_HEADS].set(v)


def kernel(x_prompt, x_sample, cache_k, cache_v, cache_mem_k, cache_mem_v, state_ssm, state_conv, state_pool, page_table, mem_prompt, w_in, lam_q1, lam_k1, lam_q2, lam_k2, da_norm_w, conv_w, conv_b, dt_bias, a_log, d_skip, ssm_norm_w, pool_w, pool_scale, w_out, ln1_g, ln1_b, wq_mem, wk_mem, wv_mem, wo_mem, ln2_g, ln2_b, ffn_w1, ffn_w3, ffn_w2, moe_router, moe_router_b, moe_w1, moe_w3, moe_w2, ln3_g, ln3_b):
    bp, seq, _ = x_prompt.shape
    bs = x_sample.shape[0]
    n_pages = page_table.shape[1]
    past = n_pages * PAGE_SIZE
    n_mem = mem_prompt.shape[1]
    tp_rows = bp * seq
    TM = 512

    xp = x_prompt.reshape(tp_rows, D_MODEL)
    xs = x_sample.reshape(bs, D_MODEL)
    tab_p = _rope_tables(jnp.arange(seq))
    tab_s = _rope_tables(jnp.full((1,), past))
    cache_k4 = jnp.transpose(cache_k, (0, 1, 3, 4, 5, 2)).reshape(DEPTH, -1, DA_WIDTH, PAGE_SIZE)
    cache_v4 = cache_v.reshape(DEPTH, -1, PAGE_SIZE * DA_HEADS, DA_DV)
    mem_k_all = _mem_rows(cache_mem_k, DEPTH * bs)
    mem_v_all = _mem_rows(cache_mem_v, DEPTH * bs)
    state_all = state_ssm.reshape(DEPTH * bs, SSM_WIDTH, SSM_STATE)
    conv_hist = jnp.transpose(state_conv, (0, 2, 1, 3))
    pool_hist = jnp.transpose(state_pool, (0, 2, 1, 3))
    w_in_t = jnp.transpose(w_in, (2, 0, 1))
    mem_flat = mem_prompt.reshape(bp * n_mem, D_MODEL)
    row = lambda v: v.reshape(1, -1)

    outs = {k: [] for k in ("kp", "vp", "ks", "vs", "mk", "mv", "hp", "hs", "cp", "cs", "pp", "ps")}
    for l in range(DEPTH):
        lam_init = 0.8 - 0.6 * math.exp(-0.3 * l)
        w_cat = _pack_w_in(w_in_t[:, l, :])
        lam_vecs = (row(lam_q1[l]), row(lam_k1[l]), row(lam_q2[l]), row(lam_k2[l]))
        nw = row(da_norm_w[l])
        dtb, alog = _pad_heads(dt_bias[l]), _pad_heads(a_log[l])
        dsk = row(jnp.repeat(d_skip[l], SSM_HEADDIM))
        snw = row(ssm_norm_w[l])
        pwbd = _block_diag(pool_w[l])
        psc = row(pool_scale[l])
        wo_b = w_out[l].astype(BF16)
        wo_parts = (wo_b[:DA_WIDTH], wo_b[DA_WIDTH:DA_WIDTH + SSM_WIDTH], wo_b[DA_WIDTH + SSM_WIDTH:])
        g1, b1, g2, b2, g3, b3 = (row(ln1_g[l]), row(ln1_b[l]), row(ln2_g[l]), row(ln2_b[l]),
                                  row(ln3_g[l]), row(ln3_b[l]))

        qp, kb, vb, kpg, vrow, zp, xbcp, pup, dtp = _proj_in(xp, w_cat, tab_p, prompt=True, tm=TM, seq=seq,
                                                             rows=tp_rows)
        oa_p = _flash_attn(qp, kb, vb, lam_vecs, nw.reshape(DA_DV, 1), batch=bp, seq=seq, tq=512, tk=512,
                           lam_init=lam_init)
        y_p, h_p = _ssd_prompt(xbcp, zp, dtp, conv_w[l], row(conv_b[l]), dtb, alog, dsk, snw, batch=bp, seq=seq)
        yp_p = _pool_prompt(pup, pwbd, psc, batch=bp, seq=seq, tp=512)
        xp = _mm_res_ln((oa_p, y_p, yp_p), wo_parts, xp, g1, b1, tm=TM)

        q_s, k_s, v_s, z_s, xbc_s, pu_s, dt_s = _proj_in(xs, w_cat, tab_s, prompt=False, tm=bs, seq=1, rows=bs)
        oa_s = _paged_attn(q_s, k_s, v_s, cache_k4, cache_v4, page_table, lam_vecs, nw, layer=l, lam_init=lam_init)
        xact, dt_a, dec, yp_s, conv_new, pool_new = _sample_prep(xbc_s, conv_hist[l], conv_w[l], row(conv_b[l]), dt_s, dtb, alog,
                                             pu_s, pool_hist[l], pwbd, psc, pos0=past)
        y_s, h_s = _ssd_step(state_all, l, xact, z_s, dt_a, dec, dsk, snw)
        xs = _mm_res_ln((oa_s, y_s, yp_s), wo_parts, xs, g1, b1, tm=bs)

        wq_b, wk_b, wv_b, wom_b = (wq_mem[l].astype(BF16), wk_mem[l].astype(BF16), wv_mem[l].astype(BF16),
                                   wo_mem[l].astype(BF16))
        mk, mkb = _matmul(mem_flat, wk_b, tm=bp * n_mem // 2, dtypes=(F32, BF16))
        mv, mvb = _matmul(mem_flat, wv_b, tm=bp * n_mem // 2, dtypes=(F32, BF16))
        xp = _mem_attn_prompt(xp, mkb, mvb, wq_b, wom_b, g2, b2, batch=bp, seq=seq, tm=TM)
        (qm_s,) = _matmul(xs, wq_b, tm=bs, scale=MEM_HD ** -0.5)
        om_s = _mem_attn_sample(qm_s, mem_k_all, mem_v_all, layer=l, group=4)
        xs = _mm_res_ln((om_s,), (wom_b,), xs, g2, b2, tm=bs)

        j = l // 2
        if l % 2 == 0:
            fw1, fw3, fw2 = ffn_w1[j].astype(BF16), ffn_w3[j].astype(BF16), ffn_w2[j].astype(BF16)
            xp = _ffn(xp, fw1, fw3, fw2, g3, b3, tm=TM, fc=256)
            xs = _ffn(xs, fw1, fw3, fw2, g3, b3, tm=bs, fc=256)
        else:
            xp, xs = _moe(xp, xs, moe_router[j], moe_router_b[j], moe_w1[j], moe_w3[j], moe_w2[j], g3, b3,
                          tm_e=1024, sub=256, fc=512, tm_p=512)

        outs["kp"].append(jnp.transpose(kpg.reshape(bp, seq // PAGE_SIZE, DA_HEADS, 2, DA_DK, PAGE_SIZE),
                                        (0, 1, 5, 2, 3, 4)))
        outs["vp"].append(vrow.reshape(bp, seq // PAGE_SIZE, PAGE_SIZE, DA_HEADS, DA_DV))
        outs["ks"].append(k_s.reshape(bs, 1, DA_HEADS, 2, DA_DK))
        outs["vs"].append(v_s.reshape(bs, 1, DA_HEADS, DA_DV))
        outs["mk"].append(mk.reshape(bp, n_mem, MEM_HEADS, MEM_HD))
        outs["mv"].append(mv.reshape(bp, n_mem, MEM_HEADS, MEM_HD))
        outs["hp"].append(h_p)
        outs["hs"].append(h_s)
        outs["cp"].append(xbcp.reshape(bp, seq, CONV_CH)[:, seq - (SSM_CONV - 1):])
        outs["cs"].append(jnp.transpose(conv_new, (1, 0, 2)))
        outs["pp"].append(pup.reshape(bp, seq, POOL_WIDTH)[:, seq - POOL_BUF:])
        outs["ps"].append(jnp.transpose(pool_new, (1, 0, 2)))

    st = lambda k: jnp.stack(outs[k])
    return (xp.reshape(bp, seq, D_MODEL), xs.reshape(bs, 1, D_MODEL), st("kp"), st("vp"), st("ks"), st("vs"),
            st("mk"), st("mv"), st("hp"), st("hs"), st("cp"), st("cs"), st("pp"), st("ps"))
```

```python
import functools
import math

import jax
import jax.numpy as jnp
import numpy as np
from jax import lax
from jax.experimental import pallas as pl
from jax.experimental.pallas import tpu as pltpu

F32 = jnp.float32
BF16 = jnp.bfloat16

D_MODEL = 1024
DEPTH = 2
PAGE_SIZE = 128
DA_HEADS = 4
DA_WIDTH = 512
DA_DV = 128
DA_DK = 64
ROT_DIM = 16
ROPE_THETA = 500000.0
SSM_WIDTH = 256
SSM_HEADS = 4
SSM_HEADDIM = 64
SSM_STATE = 128
SSM_CONV = 4
SSM_CHUNK = 128
CONV_CH = 768
POOL_WIDTH = 256
POOL_WINDOWS = (2, 4, 8, 16)
POOL_GDIM = 64
POOL_BUF = 15
MEM_HEADS = 4
MEM_HD = 256
N_EXPERTS = 8
ALPHA = (2.0 * DEPTH) ** 0.25
LN_EPS = 1e-5
RMS_EPS = 1e-6

LANES = 128
DT_PAD = LANES
_C_Q, _C_K, _C_V, _C_Z, _C_X, _C_P, _C_DT, _C_END = 0, 512, 1024, 1536, 1792, 2560, 2816, 2944

NEG_INF = float("-inf")
Q_SCALE = DA_DK ** -0.5 * math.log2(math.e)


def _cparams(sem, vmem_mb=48):
    return pltpu.CompilerParams(dimension_semantics=sem, vmem_limit_bytes=vmem_mb * 1024 * 1024)


def _dot(a, b):
    return jnp.dot(a.astype(BF16), b.astype(BF16), preferred_element_type=F32)


def _dot_nt(a, b):
    return lax.dot_general(a.astype(BF16), b.astype(BF16), (((1,), (1,)), ((), ())),
                           preferred_element_type=F32)


def _layer_norm(x, g, b):
    mu = jnp.mean(x, axis=-1, keepdims=True)
    xc = x - mu
    var = jnp.mean(xc * xc, axis=-1, keepdims=True)
    return xc * lax.rsqrt(var + LN_EPS) * g + b


def _silu(x):
    return x * (1.0 / (1.0 + jnp.exp(-x)))


def _softplus(x):
    return jnp.maximum(x, 0.0) + jnp.log(1.0 + jnp.exp(-jnp.abs(x)))


def _const_spec(shape):
    nd = len(shape)
    return pl.BlockSpec(shape, lambda *_: (0,) * nd)


def _rope_tables(pos):
    half = ROT_DIM // 2
    d = jnp.arange(LANES) % DA_DK
    inv = ROPE_THETA ** (-(d % half).astype(F32) * 2.0 / ROT_DIM)
    ang = pos.astype(F32)[:, None] * inv[None, :]
    cos, sin = jnp.cos(ang), jnp.sin(ang)
    cc = jnp.where(d < ROT_DIM, cos, 1.0)
    s1 = jnp.where(d < half, -sin, 0.0)
    s2 = jnp.where((d >= half) & (d < ROT_DIM), sin, 0.0)
    return cc, s1, s2


def _proj_in_kernel(x_ref, w_ref, c_ref, s1_ref, s2_ref, *out_refs, prompt):
    xb = x_ref[...].astype(BF16)
    cc, s1, s2 = c_ref[...], s1_ref[...], s2_ref[...]

    def rope(t):
        outs = []
        for h in range(DA_HEADS):
            th = t[:, h * LANES:(h + 1) * LANES]
            outs.append(th * cc + pltpu.roll(th, LANES - ROT_DIM // 2, 1) * s1
                        + pltpu.roll(th, ROT_DIM // 2, 1) * s2)
        return jnp.concatenate(outs, axis=1)

    def seg(a, b):
        return lax.dot_general(xb, w_ref[a:b, :], (((1,), (1,)), ((), ())), preferred_element_type=F32)

    q = rope(seg(_C_Q, _C_K)) * Q_SCALE
    k = rope(seg(_C_K, _C_V))
    v = seg(_C_V, _C_Z)
    if prompt:
        q_ref, kb_ref, vb_ref, kpg_ref, vrow_ref, z_ref, xbc_ref, pu_ref, dt_ref = out_refs
        kb_ref[...] = k.astype(BF16)
        vb_ref[...] = v.astype(BF16)
        for pg in range(kpg_ref.shape[0]):
            kpg_ref[pg] = k[pg * PAGE_SIZE:(pg + 1) * PAGE_SIZE, :].T
        for h in range(DA_HEADS):
            vrow_ref[pl.ds(h, v.shape[0], stride=DA_HEADS), :] = v[:, h * DA_DV:(h + 1) * DA_DV]
    else:
        q_ref, k_ref, v_ref, z_ref, xbc_ref, pu_ref, dt_ref = out_refs
        k_ref[...] = k
        v_ref[...] = v
    q_ref[...] = q
    z_ref[...] = seg(_C_Z, _C_X)
    xbc_ref[...] = seg(_C_X, _C_P)
    pu_ref[...] = seg(_C_P, _C_DT)
    dt_ref[...] = seg(_C_DT, _C_END)


def _proj_in(x, w_cat, tables, *, prompt, tm, seq, rows, row0=0):
    m = rows
    grid = (m // tm,)
    xrow = pl.BlockSpec((tm, D_MODEL), lambda i: (row0 + i, 0))

    def rows_out(w, dt):
        return pl.BlockSpec((tm, w), lambda i: (i, 0)), jax.ShapeDtypeStruct((m, w), dt)

    if prompt:
        npos = seq // tm
        ppt = tm // PAGE_SIZE
        tab = pl.BlockSpec((tm, LANES), lambda i: (i % npos, 0))
        outs = [rows_out(DA_WIDTH, F32), rows_out(DA_WIDTH, BF16), rows_out(DA_WIDTH, BF16),
                (pl.BlockSpec((ppt, DA_WIDTH, PAGE_SIZE), lambda i: (i, 0, 0)),
                 jax.ShapeDtypeStruct((m // PAGE_SIZE, DA_WIDTH, PAGE_SIZE), F32)),
                (pl.BlockSpec((tm * DA_HEADS, DA_DV), lambda i: (i, 0)),
                 jax.ShapeDtypeStruct((m * DA_HEADS, DA_DV), F32))]
    else:
        tab = _const_spec((1, LANES))
        outs = [rows_out(DA_WIDTH, F32), rows_out(DA_WIDTH, F32), rows_out(DA_WIDTH, F32)]
    outs += [rows_out(SSM_WIDTH, F32), rows_out(CONV_CH, F32), rows_out(POOL_WIDTH, F32), rows_out(DT_PAD, F32)]
    return pl.pallas_call(
        functools.partial(_proj_in_kernel, prompt=prompt),
        grid=grid,
        in_specs=[xrow, _const_spec(w_cat.shape), tab, tab, tab],
        out_specs=[o[0] for o in outs],
        out_shape=[o[1] for o in outs],
        compiler_params=_cparams(("parallel",)),
        name="proj_in",
    )(x, w_cat, *tables)


def _lam_value(lq1, lk1, lq2, lk2, lam_init):
    return (jnp.exp(jnp.sum(lq1 * lk1, axis=1, keepdims=True))
            - jnp.exp(jnp.sum(lq2 * lk2, axis=1, keepdims=True)) + lam_init)


def _flash_kernel(q_ref, k_ref, v_ref, lq1_ref, lk1_ref, lq2_ref, lk2_ref, nw_ref, o_ref,
                  vt_sc, s_sc, p_sc, m_sc, acc_sc, *, tq, tk, lam_init):
    i = pl.program_id(2)
    n_kt, aug, _ = vt_sc.shape

    @pl.when(i == 0)
    def _():
        ones = jnp.ones((aug - DA_DV, tk), BF16)
        for j in range(n_kt):
            vt = v_ref[j * tk:(j + 1) * tk, :].astype(F32).T.astype(BF16)
            vt_sc[j] = jnp.concatenate([vt, ones], axis=0)

    qt = q_ref[...].T
    drow = lax.broadcasted_iota(jnp.int32, qt.shape, 0)
    qts = (jnp.where(drow < DA_DK, qt, 0.0).astype(BF16), jnp.where(drow >= DA_DK, qt, 0.0).astype(BF16))

    def stage(j, cur, mask=None, last=False):
        nxt = 1 - cur
        if not last:
            kt = k_ref[pl.ds(pl.multiple_of((j + 1) * tk, tk), tk), :]
            for c in range(2):
                s_sc[nxt, c] = jnp.dot(kt, qts[c], preferred_element_type=F32)
        vta = vt_sc[jnp.maximum(j - 1, 0)]
        for c in range(2):
            pv = jnp.dot(vta, p_sc[cur, c], preferred_element_type=F32)
            s = s_sc[cur, c]
            if mask is not None:
                s = jnp.where(mask, s, NEG_INF)
            m = m_sc[c]
            m_new = jnp.maximum(m, jnp.max(s, axis=0, keepdims=True))
            p_sc[nxt, c] = jnp.exp2(s - m_new).astype(BF16)
            acc_sc[c] = jnp.exp2(m - m_new) * (acc_sc[c] + pv)
            m_sc[c] = m_new

    kt0 = k_ref[0:tk, :]
    for c in range(2):
        s_sc[0, c] = jnp.dot(kt0, qts[c], preferred_element_type=F32)
        p_sc[0, c] = jnp.zeros((tk, tq), BF16)
        m_sc[c] = jnp.full((1, tq), NEG_INF, F32)
        acc_sc[c] = jnp.zeros((aug, tq), F32)

    def pair(t, carry):
        stage(2 * t, 0)
        stage(2 * t + 1, 1)
        return carry

    lax.fori_loop(0, i // 2, pair, 0)
    odd = i % 2 == 1

    @pl.when(odd)
    def _():
        stage(i - 1, 0)

    krow = lax.broadcasted_iota(jnp.int32, (tk, tq), 0)
    qcol = lax.broadcasted_iota(jnp.int32, (tk, tq), 1)

    def finish(cur):
        stage(i, cur, mask=krow <= qcol, last=True)
        vta = vt_sc[i]
        a0, a1 = (acc_sc[c] + jnp.dot(vta, p_sc[1 - cur, c], preferred_element_type=F32) for c in range(2))
        lam = _lam_value(lq1_ref[...], lk1_ref[...], lq2_ref[...], lk2_ref[...], lam_init)
        o = a0[:DA_DV] * (1.0 / a0[DA_DV:DA_DV + 1]) - lam * (a1[:DA_DV] * (1.0 / a1[DA_DV:DA_DV + 1]))
        ms = jnp.mean(o * o, axis=0, keepdims=True)
        o = o * lax.rsqrt(ms + RMS_EPS) * nw_ref[...] * (1.0 - lam_init)
        o_ref[...] = o.T

    pl.when(odd)(lambda: finish(1))
    pl.when(jnp.logical_not(odd))(lambda: finish(0))


def _flash_attn(q, kb, v, lam_vecs, norm_w_col, *, batch, seq, tq, tk, lam_init):
    assert tq == tk
    q3 = q.reshape(batch, seq, DA_WIDTH)
    k3 = kb.reshape(batch, seq, DA_WIDTH)
    v3 = v.reshape(batch, seq, DA_WIDTH)
    qspec = pl.BlockSpec((None, tq, DA_DV), lambda b, h, i: (b, i, h))
    kvspec = pl.BlockSpec((None, seq, DA_DV), lambda b, h, i: (b, 0, h))
    vec = _const_spec((1, DA_DK))
    aug = DA_DV + 16
    out = pl.pallas_call(
        functools.partial(_flash_kernel, tq=tq, tk=tk, lam_init=lam_init),
        grid=(batch, DA_HEADS, seq // tq),
        in_specs=[qspec, kvspec, kvspec, vec, vec, vec, vec, _const_spec((DA_DV, 1))],
        out_specs=qspec,
        out_shape=jax.ShapeDtypeStruct((batch, seq, DA_WIDTH), F32),
        scratch_shapes=[pltpu.VMEM((seq // tk, aug, tk), BF16), pltpu.VMEM((2, 2, tk, tq), F32),
                        pltpu.VMEM((2, 2, tk, tq), BF16), pltpu.VMEM((2, 1, tq), F32),
                        pltpu.VMEM((2, aug, tq), F32)],
        compiler_params=_cparams(("parallel", "parallel", "arbitrary")),
        name="flash_diff_attn",
    )(q3, k3, v3, *lam_vecs, norm_w_col)
    return out.reshape(batch * seq, DA_WIDTH)


def _paged_kernel(pt_ref, q_ref, ks_ref, vs_ref, lq1_ref, lk1_ref, lq2_ref, lk2_ref, nw_ref, *rest,
                  n_pages, lam_init):
    k_refs = rest[:n_pages]
    v_refs = rest[n_pages:2 * n_pages]
    o_ref = rest[2 * n_pages]
    q = q_ref[...]
    nhc = 2 * DA_HEADS
    jj = lax.broadcasted_iota(jnp.int32, (2 * nhc, DA_WIDTH), 0)
    ll = lax.broadcasted_iota(jnp.int32, (2 * nhc, DA_WIDTH), 1)
    qblk = jnp.where((jj < nhc) & ((ll >> 6) == (jj & (DA_HEADS - 1)) * 2 + (jj >> 2)), q, 0.0)
    qb = qblk.astype(BF16)

    s_all = jnp.concatenate(
        [jnp.dot(qb, k_refs[j][...].astype(BF16), preferred_element_type=F32)[:nhc] for j in range(n_pages)],
        axis=1)
    s_self = jnp.sum(qblk[:nhc] * ks_ref[...], axis=1, keepdims=True)
    m = jnp.maximum(jnp.max(s_all, axis=1, keepdims=True), s_self)
    p = jnp.exp2(s_all - m)
    p_self = jnp.exp2(s_self - m)
    inv_l = 1.0 / (jnp.sum(p, axis=1, keepdims=True) + p_self)
    lam = _lam_value(lq1_ref[...], lk1_ref[...], lq2_ref[...], lk2_ref[...], lam_init)
    pn = p * inv_l
    pn_self = p_self * inv_l
    w = pn - lam * pltpu.roll(pn, DA_HEADS, 0)
    w_self = pn_self - lam * pltpu.roll(pn_self, DA_HEADS, 0)
    w16 = jnp.concatenate([w, jnp.zeros_like(w)], axis=0).astype(BF16)
    accs = [jnp.zeros((2 * nhc, DA_DV), F32) for _ in range(DA_HEADS)]
    for j in range(n_pages):
        wj = w16[:, j * PAGE_SIZE:(j + 1) * PAGE_SIZE]
        for h in range(DA_HEADS):
            vh = v_refs[j][pl.ds(h, PAGE_SIZE, stride=DA_HEADS), :]
            accs[h] = accs[h] + jnp.dot(wj, vh.astype(BF16), preferred_element_type=F32)
    outs = []
    for h in range(DA_HEADS):
        o = accs[h][h:h + 1, :] + w_self[h:h + 1, :] * vs_ref[:, h * DA_DV:(h + 1) * DA_DV]
        ms = jnp.mean(o * o, axis=1, keepdims=True)
        outs.append(o * lax.rsqrt(ms + RMS_EPS) * nw_ref[...] * (1.0 - lam_init))
    o_ref[...] = jnp.concatenate(outs, axis=1)


def _paged_attn(q, k_self, v_self, cache_k4, cache_v4, page_table, lam_vecs, norm_w4, *, layer, lam_init):
    bs, n_pages = page_table.shape
    pt = page_table.reshape(-1)
    one = pl.BlockSpec((None, 1, DA_WIDTH), lambda b, pt: (b, 0, 0))
    vec = pl.BlockSpec((1, DA_DK), lambda b, pt: (0, 0))

    def page_spec(j):
        return pl.BlockSpec((None, None, DA_WIDTH, PAGE_SIZE), lambda b, pt: (layer, pt[b * n_pages + j], 0, 0))

    specs = [one, one, one, vec, vec, vec, vec, pl.BlockSpec((1, DA_DV), lambda b, pt: (0, 0))]
    specs += [page_spec(j) for j in range(n_pages)] * 2
    out = pl.pallas_call(
        functools.partial(_paged_kernel, n_pages=n_pages, lam_init=lam_init),
        grid_spec=pltpu.PrefetchScalarGridSpec(
            num_scalar_prefetch=1, grid=(bs,), in_specs=specs, out_specs=one),
        out_shape=jax.ShapeDtypeStruct((bs, 1, DA_WIDTH), F32),
        compiler_params=_cparams(("arbitrary",)),
        name="paged_diff_attn",
    )(pt, q.reshape(bs, 1, DA_WIDTH), k_self.reshape(bs, 1, DA_WIDTH), v_self.reshape(bs, 1, DA_WIDTH),
      *lam_vecs, norm_w4, *([cache_k4] * n_pages), *([cache_v4] * n_pages))
    return out.reshape(bs, DA_WIDTH)


def _head_expand(cols, width, per):
    rows = cols.shape[0]
    lane = lax.broadcasted_iota(jnp.int32, (rows, width), 1)
    out = jnp.broadcast_to(cols[:, 0:1], (rows, width))
    for h in range(1, width // per):
        out = jnp.where(lane >= h * per, cols[:, h:h + 1], out)
    return out


def _ssd_kernel(xbc_ref, z_ref, dt_ref, cw_ref, cb_ref, dtb_ref, alog_ref, dsk_ref, nw_ref,
                y_ref, st_ref, tail_sc, state_sc):
    c = pl.program_id(0)

    @pl.when(c == 0)
    def _():
        tail_sc[...] = jnp.zeros_like(tail_sc)
        state_sc[...] = jnp.zeros_like(state_sc)

    for b in range(xbc_ref.shape[0]):
        state = _ssd_chunk(xbc_ref.at[b], z_ref.at[b], dt_ref.at[b], cw_ref, cb_ref, dtb_ref, alog_ref, dsk_ref,
                           nw_ref, y_ref.at[b], tail_sc.at[b], state_sc.at[b])

        @pl.when(c == pl.num_programs(0) - 1)
        def _():
            st_ref[b] = state


def _ssd_chunk(xbc_ref, z_ref, dt_ref, cw_ref, cb_ref, dtb_ref, alog_ref, dsk_ref, nw_ref, y_ref, tail_sc,
               state_sc):
    q = SSM_CHUNK
    u = xbc_ref[...]
    full = jnp.concatenate([tail_sc[...], u], axis=0)
    tail_sc[...] = u[q - 8:, :]
    conv = full * cw_ref[SSM_CONV - 1:SSM_CONV, :]
    for j in range(1, SSM_CONV):
        conv = conv + pltpu.roll(full, j, 0) * cw_ref[SSM_CONV - 1 - j:SSM_CONV - j, :]
    xbc = _silu(conv[8:, :] + cb_ref[...])
    xs = xbc[:, :SSM_WIDTH]
    bm = (xbc[:, SSM_WIDTH:SSM_WIDTH + SSM_STATE], xbc[:, SSM_WIDTH + SSM_STATE:SSM_WIDTH + 2 * SSM_STATE])
    cm = (xbc[:, SSM_WIDTH + 2 * SSM_STATE:SSM_WIDTH + 3 * SSM_STATE], xbc[:, SSM_WIDTH + 3 * SSM_STATE:])

    dt = _softplus(dt_ref[...] + dtb_ref[...])
    dta = dt * (-jnp.exp(alog_ref[...]))
    ti = lax.broadcasted_iota(jnp.int32, (q, q), 0)
    si = lax.broadcasted_iota(jnp.int32, (q, q), 1)
    causal = si <= ti
    tri = jnp.where(causal, 1.0, 0.0)
    a_cs = jnp.dot(tri, dta, precision=lax.Precision.HIGHEST, preferred_element_type=F32)
    a_cs_t = a_cs.T
    last = a_cs[q - 1:q, :]

    dt_x = _head_expand(dt, SSM_WIDTH, SSM_HEADDIM)
    xdt = xs * dt_x
    xdt_end = xdt * _head_expand(jnp.exp(last - a_cs), SSM_WIDTH, SSM_HEADDIM)
    lane = lax.broadcasted_iota(jnp.int32, (q, SSM_WIDTH), 1)

    cb = [_dot_nt(cm[g], bm[g]) for g in range(2)]
    y_diag = jnp.zeros((q, SSM_WIDTH), F32)
    for h in range(SSM_HEADS):
        seg = a_cs[:, h:h + 1] - a_cs_t[h:h + 1, :]
        decay = jnp.exp(jnp.where(causal, seg, NEG_INF))
        xh = jnp.where((lane >= h * SSM_HEADDIM) & (lane < (h + 1) * SSM_HEADDIM), xdt, 0.0)
        y_diag = y_diag + _dot(cb[h // 2] * decay, xh)

    state = state_sc[...]
    sb = state.astype(BF16)
    y_off = jnp.where(lane < 2 * SSM_HEADDIM, _dot_nt(cm[0], sb), _dot_nt(cm[1], sb))
    y_off = y_off * _head_expand(jnp.exp(a_cs), SSM_WIDTH, SSM_HEADDIM)

    xt = xdt_end.T
    rowi = lax.broadcasted_iota(jnp.int32, (SSM_WIDTH, SSM_STATE), 0)
    new = jnp.where(rowi < 2 * SSM_HEADDIM, _dot(xt, bm[0]), _dot(xt, bm[1]))
    e_last = jnp.exp(last)
    dec_rows = jnp.broadcast_to(e_last[:, 0:1], (SSM_WIDTH, SSM_STATE))
    for h in range(1, SSM_HEADS):
        dec_rows = jnp.where(rowi >= h * SSM_HEADDIM, e_last[:, h:h + 1], dec_rows)
    state = state * dec_rows + new
    state_sc[...] = state

    y = y_diag + y_off + dsk_ref[...] * xs
    y = y * _silu(z_ref[...])
    ms = jnp.mean(y * y, axis=1, keepdims=True)
    y_ref[...] = y * lax.rsqrt(ms + RMS_EPS) * nw_ref[...]
    return state


def _ssd_prompt(xbc, z, dt_raw, conv_w, conv_b, dt_bias, a_log, d_skip_x, norm_w, *, batch, seq):
    nc = seq // SSM_CHUNK
    blk = lambda w: pl.BlockSpec((batch, SSM_CHUNK, w), lambda c: (0, c, 0))
    y, st = pl.pallas_call(
        _ssd_kernel,
        grid=(nc,),
        in_specs=[blk(CONV_CH), blk(SSM_WIDTH), blk(DT_PAD),
                  _const_spec((SSM_CONV, CONV_CH)), _const_spec((1, CONV_CH)),
                  _const_spec((1, DT_PAD)), _const_spec((1, DT_PAD)),
                  _const_spec((1, SSM_WIDTH)), _const_spec((1, SSM_WIDTH))],
        out_specs=[blk(SSM_WIDTH), pl.BlockSpec((batch, SSM_WIDTH, SSM_STATE), lambda c: (0, 0, 0))],
        out_shape=[jax.ShapeDtypeStruct((batch, seq, SSM_WIDTH), F32),
                   jax.ShapeDtypeStruct((batch, SSM_WIDTH, SSM_STATE), F32)],
        scratch_shapes=[pltpu.VMEM((batch, 8, CONV_CH), F32), pltpu.VMEM((batch, SSM_WIDTH, SSM_STATE), F32)],
        compiler_params=_cparams(("arbitrary",)),
        name="ssd_prompt",
    )(xbc.reshape(batch, seq, CONV_CH), z.reshape(batch, seq, SSM_WIDTH), dt_raw.reshape(batch, seq, DT_PAD),
      conv_w, conv_b, dt_bias, a_log, d_skip_x, norm_w)
    return y.reshape(batch * seq, SSM_WIDTH), st.reshape(batch, SSM_HEADS, SSM_HEADDIM, SSM_STATE)


def _pool_select(sums, inv_cnt, lane):
    out = sums[POOL_WINDOWS[0]] * inv_cnt[POOL_WINDOWS[0]]
    for g in range(1, len(POOL_WINDOWS)):
        w = POOL_WINDOWS[g]
        out = jnp.where(lane >= g * POOL_GDIM, sums[w] * inv_cnt[w], out)
    return out


def _pool_kernel(u_ref, w_ref, sc_ref, y_ref, tail_sc, *, tp):
    c = pl.program_id(1)
    halo = 16

    @pl.when(c == 0)
    def _():
        tail_sc[...] = jnp.zeros_like(tail_sc)

    u = u_ref[...]
    full = jnp.concatenate([tail_sc[...], u], axis=0)
    tail_sc[...] = u[tp - halo:, :]
    sums = {}
    run = full
    w = 1
    while w < max(POOL_WINDOWS):
        run = run + pltpu.roll(run, w, 0)
        w *= 2
        sums[w] = run[halo:, :]
    pos = c * tp + lax.broadcasted_iota(jnp.int32, (tp, 1), 0)
    inv_cnt = {w: 1.0 / jnp.minimum(w, pos + 1).astype(F32) for w in POOL_WINDOWS}
    lane = lax.broadcasted_iota(jnp.int32, (tp, POOL_WIDTH), 1)
    d = _pool_select(sums, inv_cnt, lane) - u
    y_ref[...] = _dot(d, w_ref[...]) * sc_ref[...]


def _pool_prompt(pu, pool_wbd, pool_scale, *, batch, seq, tp):
    blk = pl.BlockSpec((None, tp, POOL_WIDTH), lambda b, c: (b, c, 0))
    y = pl.pallas_call(
        functools.partial(_pool_kernel, tp=tp),
        grid=(batch, seq // tp),
        in_specs=[blk, _const_spec((POOL_WIDTH, POOL_WIDTH)), _const_spec((1, POOL_WIDTH))],
        out_specs=blk,
        out_shape=jax.ShapeDtypeStruct((batch, seq, POOL_WIDTH), F32),
        scratch_shapes=[pltpu.VMEM((16, POOL_WIDTH), F32)],
        compiler_params=_cparams(("parallel", "arbitrary")),
        name="pool_prompt",
    )(pu.reshape(batch, seq, POOL_WIDTH), pool_wbd, pool_scale)
    return y.reshape(batch * seq, POOL_WIDTH)


def _sample_prep_kernel(xbc_ref, cbuf_ref, cw_ref, cb_ref, dt_ref, dtb_ref, alog_ref, pu_ref, pbuf_ref,
                        pw_ref, psc_ref, xact_ref, dto_ref, dec_ref, yp_ref, cnew_ref, pnew_ref, *, pos0):
    u = xbc_ref[...]
    for j in range(SSM_CONV - 2):
        cnew_ref[j] = cbuf_ref[j + 1]
    cnew_ref[SSM_CONV - 2] = u
    for j in range(POOL_BUF - 1):
        pnew_ref[j] = pbuf_ref[j + 1]
    pnew_ref[POOL_BUF - 1] = pu_ref[...]
    conv = u * cw_ref[SSM_CONV - 1:SSM_CONV, :] + cb_ref[...]
    for j in range(SSM_CONV - 1):
        conv = conv + cbuf_ref[j] * cw_ref[j:j + 1, :]
    xact_ref[...] = _silu(conv)
    dt = _softplus(dt_ref[...] + dtb_ref[...])
    dto_ref[...] = dt
    dec_ref[...] = jnp.exp(dt * (-jnp.exp(alog_ref[...])))
    pu = pu_ref[...]
    run = pu
    sums = {}
    for j in range(1, max(POOL_WINDOWS)):
        run = run + pbuf_ref[POOL_BUF - j]
        if j + 1 in POOL_WINDOWS:
            sums[j + 1] = run
    inv_cnt = {w: 1.0 / float(min(w, pos0 + 1)) for w in POOL_WINDOWS}
    lane = lax.broadcasted_iota(jnp.int32, pu.shape, 1)
    d = _pool_select(sums, inv_cnt, lane) - pu
    yp_ref[...] = _dot(d, pw_ref[...]) * psc_ref[...]


def _sample_prep(xbc, conv_buf, conv_w, conv_b, dt_raw, dt_bias, a_log, pu, pool_buf, pool_wbd, pool_scale, *, pos0):
    bs = xbc.shape[0]
    args = (xbc, conv_buf, conv_w, conv_b, dt_raw, dt_bias, a_log, pu, pool_buf, pool_wbd, pool_scale)
    shapes = [(bs, CONV_CH), (bs, DT_PAD), (bs, DT_PAD), (bs, POOL_WIDTH), conv_buf.shape, pool_buf.shape]
    return pl.pallas_call(
        functools.partial(_sample_prep_kernel, pos0=pos0),
        grid=(1,),
        in_specs=[_const_spec(a.shape) for a in args],
        out_specs=[_const_spec(s) for s in shapes],
        out_shape=[jax.ShapeDtypeStruct(s, F32) for s in shapes],
        compiler_params=_cparams(("arbitrary",)),
        name="sample_prep",
    )(*args)


def _ssd_step_kernel(s_ref, x_ref, z_ref, dt_ref, dec_ref, b_ref, c_ref, dsk_ref, nw_ref, y_ref, so_ref, *, group):
    rowi = lax.broadcasted_iota(jnp.int32, (SSM_WIDTH, SSM_STATE), 0)
    for i in range(group):
        brow, crow = b_ref[i], c_ref[i]
        bsel = jnp.where(rowi < 2 * SSM_HEADDIM, brow[:, :SSM_STATE], brow[:, SSM_STATE:])
        csel = jnp.where(rowi < 2 * SSM_HEADDIM, crow[:, :SSM_STATE], crow[:, SSM_STATE:])
        x = x_ref[i]
        sn = s_ref[i] * dec_ref[i] + (x * dt_ref[i]) * bsel
        so_ref[i] = sn
        y = jnp.sum(sn * csel, axis=1, keepdims=True) + dsk_ref[...] * x
        y = y * _silu(z_ref[i])
        ms = jnp.sum(y * y, axis=0, keepdims=True) * (1.0 / SSM_WIDTH)
        y_ref[i] = y * lax.rsqrt(ms + RMS_EPS) * nw_ref[...]


def _ssd_step(state_all, layer, xact, z, dt, dec, d_skip_x, norm_w, *, group=8):
    bs = xact.shape[0]
    nb = bs // group
    rep = lambda t: jnp.repeat(t[:, :SSM_HEADS], SSM_HEADDIM, axis=1).reshape(bs, SSM_WIDTH, 1)
    colb = pl.BlockSpec((group, SSM_WIDTH, 1), lambda b: (b, 0, 0))
    rowb = pl.BlockSpec((group, 1, 2 * SSM_STATE), lambda b: (b, 0, 0))
    st_in = pl.BlockSpec((group, SSM_WIDTH, SSM_STATE), lambda b: (layer * nb + b, 0, 0))
    st_out = pl.BlockSpec((group, SSM_WIDTH, SSM_STATE), lambda b: (b, 0, 0))
    y, st = pl.pallas_call(
        functools.partial(_ssd_step_kernel, group=group),
        grid=(nb,),
        in_specs=[st_in, colb, colb, colb, colb, rowb, rowb,
                  _const_spec((SSM_WIDTH, 1)), _const_spec((SSM_WIDTH, 1))],
        out_specs=[colb, st_out],
        out_shape=[jax.ShapeDtypeStruct((bs, SSM_WIDTH, 1), F32),
                   jax.ShapeDtypeStruct((bs, SSM_WIDTH, SSM_STATE), F32)],
        compiler_params=_cparams(("parallel",)),
        name="ssd_step",
    )(state_all, xact[:, :SSM_WIDTH].reshape(bs, SSM_WIDTH, 1), z.reshape(bs, SSM_WIDTH, 1), rep(dt), rep(dec),
      xact[:, SSM_WIDTH:SSM_WIDTH + 2 * SSM_STATE].reshape(bs, 1, 2 * SSM_STATE),
      xact[:, SSM_WIDTH + 2 * SSM_STATE:].reshape(bs, 1, 2 * SSM_STATE),
      d_skip_x.reshape(SSM_WIDTH, 1), norm_w.reshape(SSM_WIDTH, 1))
    return y.reshape(bs, SSM_WIDTH), st.reshape(bs, SSM_HEADS, SSM_HEADDIM, SSM_STATE)


def _mm_kernel(x_ref, w_ref, *out_refs, scale):
    y = _dot(x_ref[...], w_ref[...])
    if scale != 1.0:
        y = y * scale
    for r in out_refs:
        r[...] = y.astype(r.dtype)


def _matmul(x, w, *, tm, scale=1.0, dtypes=(F32,)):
    m, k = x.shape
    n = w.shape[1]
    res = pl.pallas_call(
        functools.partial(_mm_kernel, scale=scale),
        grid=(m // tm,),
        in_specs=[pl.BlockSpec((tm, k), lambda i: (i, 0)), _const_spec(w.shape)],
        out_specs=[pl.BlockSpec((tm, n), lambda i: (i, 0)) for _ in dtypes],
        out_shape=[jax.ShapeDtypeStruct((m, n), d) for d in dtypes],
        compiler_params=_cparams(("parallel",)),
        name="matmul",
    )(x, w)
    return res


def _mm_res_ln_kernel(*refs, n_in):
    a_refs = refs[:n_in]
    w_refs = refs[n_in:2 * n_in]
    x_ref, g_ref, b_ref, o_ref = refs[2 * n_in:]
    h = _dot(a_refs[0][...], w_refs[0][...])
    for a, w in zip(a_refs[1:], w_refs[1:]):
        h = h + _dot(a[...], w[...])
    o_ref[...] = _layer_norm(ALPHA * x_ref[...] + h, g_ref[...], b_ref[...])


def _mm_res_ln(acts, weights, x, g, b, *, tm, x_row0=0):
    m = acts[0].shape[0]
    row = lambda a: pl.BlockSpec((tm, a.shape[1]), lambda i: (i, 0))
    return pl.pallas_call(
        functools.partial(_mm_res_ln_kernel, n_in=len(acts)),
        grid=(m // tm,),
        in_specs=[row(a) for a in acts] + [_const_spec(w.shape) for w in weights]
        + [pl.BlockSpec((tm, D_MODEL), lambda i: (x_row0 + i, 0)), _const_spec(g.shape), _const_spec(b.shape)],
        out_specs=pl.BlockSpec((tm, D_MODEL), lambda i: (i, 0)),
        out_shape=jax.ShapeDtypeStruct((m, D_MODEL), F32),
        compiler_params=_cparams(("parallel",)),
        name="mm_res_ln",
    )(*acts, *weights, x, g, b)


def _mix_mem_kernel(oa_ref, y_ref, yp_ref, x_ref, wa_ref, wy_ref, wp_ref, g1_ref, b1_ref,
                    mk_ref, mv_ref, wq_ref, wo_ref, g2_ref, b2_ref, *rest, route):
    h = _dot(oa_ref[...], wa_ref[...]) + _dot(y_ref[...], wy_ref[...]) + _dot(yp_ref[...], wp_ref[...])
    x = _layer_norm(ALPHA * x_ref[...] + h, g1_ref[...], b1_ref[...])
    q = (_dot(x, wq_ref[...]) * (MEM_HD ** -0.5)).astype(BF16)
    outs = []
    for hd in range(MEM_HEADS):
        sl = slice(hd * MEM_HD, (hd + 1) * MEM_HD)
        s = lax.dot_general(q[:, sl], mk_ref[:, sl], (((1,), (1,)), ((), ())), preferred_element_type=F32)
        p = jnp.exp(s - jnp.max(s, axis=1, keepdims=True))
        p = p * (1.0 / jnp.sum(p, axis=1, keepdims=True))
        outs.append(jnp.dot(p.astype(BF16), mv_ref[:, sl], preferred_element_type=F32))
    o = jnp.concatenate(outs, axis=1)
    x2 = _layer_norm(ALPHA * x + _dot(o, wo_ref[...]), g2_ref[...], b2_ref[...])
    if route:
        wh_ref, wl_ref, rb_ref, o_ref, r_ref = rest
        r_ref[...] = _route(x2, wh_ref, wl_ref, rb_ref)
    else:
        (o_ref,) = rest
    o_ref[...] = x2


def _mix_mem_prompt(acts, w_parts, x, g1, b1, mkb, mvb, wq, wo, g2, b2, router=None, *, batch, seq, tm):
    n_mem = mkb.shape[0] // batch
    nt = seq // tm
    rblk = lambda w: pl.BlockSpec((tm, w), lambda bb, i: (bb * nt + i, 0))
    mblk = pl.BlockSpec((None, n_mem, D_MODEL), lambda bb, i: (bb, 0, 0))
    const = lambda a: pl.BlockSpec(a.shape, lambda bb, i: (0,) * a.ndim)
    consts = (*w_parts, g1, b1)
    tail = (wq, wo, g2, b2) + (tuple(router) if router else ())
    out_specs = [rblk(D_MODEL)] + ([rblk(LANES)] if router else [])
    out_shape = [jax.ShapeDtypeStruct((batch * seq, D_MODEL), F32)]
    if router:
        out_shape.append(jax.ShapeDtypeStruct((batch * seq, LANES), F32))
    return pl.pallas_call(
        functools.partial(_mix_mem_kernel, route=bool(router)),
        grid=(batch, nt),
        in_specs=[rblk(a.shape[1]) for a in acts] + [rblk(D_MODEL)] + [const(a) for a in consts]
        + [mblk, mblk] + [const(a) for a in tail],
        out_specs=out_specs,
        out_shape=out_shape,
        compiler_params=_cparams(("parallel", "parallel")),
        name="mix_mem_prompt",
    )(*acts, x, *consts, mkb.reshape(batch, n_mem, D_MODEL), mvb.reshape(batch, n_mem, D_MODEL), *tail)


def _mem_sample_kernel(q_ref, mk_ref, mv_ref, o_ref, *, group):
    rows = mk_ref.shape[1] // 8
    for i in range(group):
        q8 = q_ref[i]
        r = jnp.sum(mk_ref[i].reshape(rows, 8, LANES) * q8, axis=2, keepdims=True)
        s = r + pltpu.roll(r, MEM_HEADS, 1)
        p = jnp.exp(s - jnp.max(s, axis=0, keepdims=True))
        p = p * (1.0 / jnp.sum(p, axis=0, keepdims=True))
        o_ref[i] = jnp.sum(mv_ref[i].reshape(rows, 8, LANES) * p, axis=0)


def _mem_rows(cache, n):
    n_mem = cache.shape[2]
    c = cache.reshape(n, n_mem, MEM_HEADS, MEM_HD // LANES, LANES)
    return jnp.transpose(c, (0, 1, 3, 2, 4)).reshape(n, n_mem * 2 * MEM_HEADS, LANES)


def _mem_attn_sample(q, mem_k_rows, mem_v_rows, *, layer, group):
    bs = q.shape[0]
    nb = bs // group
    halves = MEM_HD // LANES
    q8 = jnp.transpose(q.reshape(bs, MEM_HEADS, halves, LANES), (0, 2, 1, 3)).reshape(bs, halves * MEM_HEADS, LANES)
    qblk = pl.BlockSpec((group, halves * MEM_HEADS, LANES), lambda i: (i, 0, 0))
    mblk = pl.BlockSpec((group,) + mem_k_rows.shape[1:], lambda i: (layer * nb + i, 0, 0))
    out = pl.pallas_call(
        functools.partial(_mem_sample_kernel, group=group),
        grid=(nb,),
        in_specs=[qblk, mblk, mblk],
        out_specs=qblk,
        out_shape=jax.ShapeDtypeStruct(q8.shape, F32),
        compiler_params=_cparams(("parallel",)),
        name="mem_attn_sample",
    )(q8, mem_k_rows, mem_v_rows)
    return jnp.transpose(out.reshape(bs, halves, MEM_HEADS, LANES), (0, 2, 1, 3)).reshape(bs, D_MODEL)


def _ffn_kernel(x_ref, w1_ref, w3_ref, w2_ref, g_ref, b_ref, o_ref, *, fc):
    x = x_ref[...]
    xb = x.astype(BF16)
    acc = jnp.zeros(x.shape, F32)
    for c in range(w1_ref.shape[1] // fc):
        sl = slice(c * fc, (c + 1) * fc)
        h = _silu(jnp.dot(xb, w1_ref[:, sl], preferred_element_type=F32)) * jnp.dot(xb, w3_ref[:, sl],
                                                                                    preferred_element_type=F32)
        acc = acc + jnp.dot(h.astype(BF16), w2_ref[sl, :], preferred_element_type=F32)
    o_ref[...] = _layer_norm(ALPHA * x + acc, g_ref[...], b_ref[...])


def _ffn(x, w1, w3, w2, g, b, *, tm, fc):
    m = x.shape[0]
    row = pl.BlockSpec((tm, D_MODEL), lambda i: (i, 0))
    wspec = lambda w: pl.BlockSpec(w.shape, lambda i: (0, 0), pipeline_mode=pl.Buffered(1))
    return pl.pallas_call(
        functools.partial(_ffn_kernel, fc=fc),
        grid=(m // tm,),
        in_specs=[row, wspec(w1), wspec(w3), wspec(w2), _const_spec(g.shape), _const_spec(b.shape)],
        out_specs=row,
        out_shape=jax.ShapeDtypeStruct(x.shape, F32),
        compiler_params=_cparams(("parallel",), vmem_mb=56),
        name="ffn_dense",
    )(x, w1, w3, w2, g, b)


def _route(x, wh_ref, wl_ref, b_ref):
    xh = x.astype(BF16)
    xl = (x - xh.astype(F32)).astype(BF16)
    wh = wh_ref[...]
    logits = (jnp.dot(xh, wh, preferred_element_type=F32) + jnp.dot(xl, wh, preferred_element_type=F32)
              + jnp.dot(xh, wl_ref[...], preferred_element_type=F32))
    lane = lax.broadcasted_iota(jnp.int32, logits.shape, 1)
    logits = jnp.where(lane < N_EXPERTS, logits + b_ref[...], NEG_INF)
    m1 = jnp.max(logits, axis=1, keepdims=True)
    i1 = jnp.min(jnp.where(logits == m1, lane, LANES), axis=1, keepdims=True)
    rest = jnp.where(lane == i1, NEG_INF, logits)
    m2 = jnp.max(rest, axis=1, keepdims=True)
    i2 = jnp.min(jnp.where(rest == m2, lane, LANES), axis=1, keepdims=True)
    e = jnp.exp(m2 - m1)
    g1 = 1.0 / (1.0 + e)
    g2 = e * g1
    return jnp.where(lane == 0, i1.astype(F32), jnp.where(lane == 1, i2.astype(F32),
                     jnp.where(lane == 2, g1, jnp.where(lane == 3, g2, 0.0))))


def _router_kernel(x_ref, wh_ref, wl_ref, b_ref, o_ref):
    o_ref[...] = _route(x_ref[...], wh_ref, wl_ref, b_ref)


def _router(x, w_hi, w_lo, b_pad, *, tm):
    m = x.shape[0]
    return pl.pallas_call(
        _router_kernel,
        grid=(m // tm,),
        in_specs=[pl.BlockSpec((tm, D_MODEL), lambda i: (i, 0)), _const_spec(w_hi.shape), _const_spec(w_lo.shape),
                  _const_spec(b_pad.shape)],
        out_specs=pl.BlockSpec((tm, LANES), lambda i: (i, 0)),
        out_shape=jax.ShapeDtypeStruct((m, LANES), F32),
        compiler_params=_cparams(("parallel",)),
        name="moe_router",
    )(x, w_hi, w_lo, b_pad)


def _expert_kernel(te_ref, tr_ref, x_ref, w1_ref, w3_ref, w2_ref, o_ref, acc_sc, *, sub):
    i = pl.program_id(0)
    j = pl.program_id(1)
    rows = tr_ref[i]
    tm = x_ref.shape[0]

    @pl.when(j == 0)
    def _():
        acc_sc[...] = jnp.zeros_like(acc_sc)

    def swiglu_into_acc(sl):
        xb = x_ref[sl, :].astype(BF16)
        w1 = w1_ref[...].astype(BF16)
        w3 = w3_ref[...].astype(BF16)
        h = _silu(jnp.dot(xb, w1, preferred_element_type=F32)) * jnp.dot(xb, w3, preferred_element_type=F32)
        acc_sc[sl, :] += jnp.dot(h.astype(BF16), w2_ref[...].astype(BF16), preferred_element_type=F32)

    @pl.when(rows == tm)
    def _():
        swiglu_into_acc(slice(None))

    @pl.when((rows > 0) & (rows < tm))
    def _():
        for s in range(tm // sub):
            pl.when(rows > s * sub)(functools.partial(swiglu_into_acc, slice(s * sub, (s + 1) * sub)))

    @pl.when(j == pl.num_programs(1) - 1)
    def _():
        o_ref[...] = acc_sc[...]


def _experts(x_sorted, tile_expert, tile_rows, w1, w3, w2, *, tm, sub, fc):
    n_slots = x_sorted.shape[0]
    nj = w1.shape[2] // fc
    last = nj - 1

    def jj(i, j, tr):
        return jnp.where(tr[i] > 0, j, last)

    xs = pl.BlockSpec((tm, D_MODEL), lambda i, j, te, tr: (i, 0))
    w13 = pl.BlockSpec((None, D_MODEL, fc), lambda i, j, te, tr: (te[i], 0, jj(i, j, tr)))
    w2s = pl.BlockSpec((None, fc, D_MODEL), lambda i, j, te, tr: (te[i], jj(i, j, tr), 0))
    return pl.pallas_call(
        functools.partial(_expert_kernel, sub=sub),
        grid_spec=pltpu.PrefetchScalarGridSpec(
            num_scalar_prefetch=2, grid=(n_slots // tm, nj),
            in_specs=[xs, w13, w13, w2s], out_specs=xs,
            scratch_shapes=[pltpu.VMEM((tm, D_MODEL), F32)]),
        out_shape=jax.ShapeDtypeStruct((n_slots, D_MODEL), F32),
        compiler_params=_cparams(("arbitrary", "arbitrary"), vmem_mb=56),
        name="moe_experts",
    )(tile_expert, tile_rows, x_sorted, w1, w3, w2)


def _dispatch_kernel(slot_ref, x_ref, init_ref, xs_ref, sem, *, tm, tok0):
    del init_ref
    base = (tok0 + pl.program_id(0) * tm) * 2

    def issue(r, carry):
        for k in range(2):
            dst = slot_ref[base + 2 * r + k]
            pltpu.make_async_copy(x_ref.at[pl.ds(r, 1), :], xs_ref.at[pl.ds(dst, 1), :], sem).start()
        return carry

    lax.fori_loop(0, tm, issue, 0, unroll=8)
    for _ in range(2):
        pltpu.make_async_copy(x_ref, x_ref, sem).wait()


def _dispatch(x, slot, buf, *, tm, tok0):
    t = x.shape[0]
    return pl.pallas_call(
        functools.partial(_dispatch_kernel, tm=tm, tok0=tok0),
        grid_spec=pltpu.PrefetchScalarGridSpec(
            num_scalar_prefetch=1, grid=(t // tm,),
            in_specs=[pl.BlockSpec((tm, D_MODEL), lambda i, s: (i, 0)), pl.BlockSpec(memory_space=pl.ANY)],
            out_specs=pl.BlockSpec(memory_space=pl.ANY),
            scratch_shapes=[pltpu.SemaphoreType.DMA(())]),
        out_shape=jax.ShapeDtypeStruct(buf.shape, F32),
        input_output_aliases={2: 0},
        compiler_params=_cparams(("arbitrary",)),
        name="moe_dispatch",
    )(slot, x, buf)


def _combine_ln_kernel(slot_ref, x_ref, r_ref, y_ref, g_ref, b_ref, o_ref, ybuf, sems, *, tm, tok0):
    i = pl.program_id(0)
    cur = i % 2

    def issue(step, half):
        base = (tok0 + step * tm) * 2

        def body(r, carry):
            for k in range(2):
                src = slot_ref[base + 2 * r + k]
                pltpu.make_async_copy(y_ref.at[pl.ds(src, 1), :], ybuf.at[half, k, pl.ds(r, 1), :],
                                      sems.at[half]).start()
            return carry

        lax.fori_loop(0, tm, body, 0, unroll=8)

    @pl.when(i == 0)
    def _():
        issue(0, 0)

    @pl.when(i + 1 < pl.num_programs(0))
    def _():
        issue(i + 1, 1 - cur)

    pltpu.make_async_copy(ybuf.at[cur], ybuf.at[cur], sems.at[cur]).wait()
    r = r_ref[...]
    f = r[:, 2:3] * ybuf[cur, 0] + r[:, 3:4] * ybuf[cur, 1]
    o_ref[...] = _layer_norm(ALPHA * x_ref[...] + f, g_ref[...], b_ref[...])


def _combine_ln(x, y_sorted, slot, route, g, b, *, tm, tok0):
    rows = x.shape[0]
    row = lambda w: pl.BlockSpec((tm, w), lambda i, s: (i, 0))
    const = lambda a: pl.BlockSpec(a.shape, lambda i, s: (0,) * a.ndim)
    return pl.pallas_call(
        functools.partial(_combine_ln_kernel, tm=tm, tok0=tok0),
        grid_spec=pltpu.PrefetchScalarGridSpec(
            num_scalar_prefetch=1, grid=(rows // tm,),
            in_specs=[row(D_MODEL), row(LANES), pl.BlockSpec(memory_space=pl.ANY), const(g), const(b)],
            out_specs=row(D_MODEL),
            scratch_shapes=[pltpu.VMEM((2, 2, tm, D_MODEL), F32), pltpu.SemaphoreType.DMA((2,))]),
        out_shape=jax.ShapeDtypeStruct((rows, D_MODEL), F32),
        compiler_params=_cparams(("arbitrary",)),
        name="moe_combine_ln",
    )(slot, x, route, y_sorted, g, b)


def _router_params(router_w, router_b):
    w_pad = jnp.zeros((D_MODEL, LANES), F32).at[:, :N_EXPERTS].set(router_w)
    w_hi = w_pad.astype(BF16)
    w_lo = (w_pad - w_hi.astype(F32)).astype(BF16)
    return w_hi, w_lo, jnp.zeros((1, LANES), F32).at[0, :N_EXPERTS].set(router_b)


def _moe(xp, xs, route_p, router, w1, w3, w2, g, b, *, tm_e, sub, fc, tm_p):
    n_p, n_s = xp.shape[0], xs.shape[0]
    t = n_p + n_s
    route_s = _router(xs, *router, tm=n_s)
    top_i = jnp.concatenate([route_p[:, :2], route_s[:, :2]], axis=0).astype(jnp.int32).reshape(-1)
    onehot = (top_i[:, None] == jnp.arange(N_EXPERTS)[None, :]).astype(jnp.int32)
    csum = jnp.cumsum(onehot, axis=0)
    counts = csum[-1]
    rank = jnp.sum((csum - onehot) * onehot, axis=1)
    padded = ((counts + tm_e - 1) // tm_e) * tm_e
    ends = jnp.cumsum(padded)
    starts = ends - padded
    slot = (starts[top_i] + rank).astype(jnp.int32)
    n_tiles = (2 * t + N_EXPERTS * (tm_e - 1)) // tm_e + 1
    tile_start = jnp.arange(n_tiles, dtype=jnp.int32) * tm_e
    tile_expert = jnp.sum((tile_start[:, None] >= ends[None, :]).astype(jnp.int32), axis=1)
    tile_expert = jnp.minimum(tile_expert, N_EXPERTS - 1)
    tile_rows = jnp.clip((starts + counts)[tile_expert] - tile_start, 0, tm_e)
    tile_rows = jnp.where(tile_start < ends[-1], tile_rows, 0).astype(jnp.int32)
    last_e = tile_expert[jnp.maximum(ends[-1] // tm_e - 1, 0)]
    tile_expert = jnp.where(tile_rows > 0, tile_expert, last_e).astype(jnp.int32)
    x_sorted = jnp.zeros((n_tiles * tm_e, D_MODEL), F32)
    x_sorted = _dispatch(xp, slot, x_sorted, tm=tm_p, tok0=0)
    x_sorted = _dispatch(xs, slot, x_sorted, tm=n_s, tok0=n_p)
    y_sorted = _experts(x_sorted, tile_expert, tile_rows, w1, w3, w2, tm=tm_e, sub=sub, fc=fc)
    return (_combine_ln(xp, y_sorted, slot, route_p, g, b, tm=tm_p // 2, tok0=0),
            _combine_ln(xs, y_sorted, slot, route_s, g, b, tm=n_s, tok0=n_p))


def _pack_w_in(w_in_t):
    q, k, v, z, xbc, dt, pu = jnp.split(w_in_t, np.cumsum([512, 512, 512, 256, 768, 4]).tolist(), axis=0)
    pad = jnp.zeros((DT_PAD - SSM_HEADS, D_MODEL), w_in_t.dtype)
    return jnp.concatenate([q, k, v, z, xbc, pu, dt, pad], axis=0).astype(BF16)


def _block_diag(pool_w):
    out = jnp.zeros((POOL_WIDTH, POOL_WIDTH), F32)
    for g in range(len(POOL_WINDOWS)):
        out = out.at[g * POOL_GDIM:(g + 1) * POOL_GDIM, g * POOL_GDIM:(g + 1) * POOL_GDIM].set(pool_w[g])
    return out.astype(BF16)


def _pad_heads(v):
    return jnp.zeros((1, DT_PAD), F32).at[0, :SSM_HEADS].set(v)


def kernel(x_prompt, x_sample, cache_k, cache_v, cache_mem_k, cache_mem_v, state_ssm, state_conv, state_pool, page_table, mem_prompt, w_in, lam_q1, lam_k1, lam_q2, lam_k2, da_norm_w, conv_w, conv_b, dt_bias, a_log, d_skip, ssm_norm_w, pool_w, pool_scale, w_out, ln1_g, ln1_b, wq_mem, wk_mem, wv_mem, wo_mem, ln2_g, ln2_b, ffn_w1, ffn_w3, ffn_w2, moe_router, moe_router_b, moe_w1, moe_w3, moe_w2, ln3_g, ln3_b):
    bp, seq, _ = x_prompt.shape
    bs = x_sample.shape[0]
    n_pages = page_table.shape[1]
    past = n_pages * PAGE_SIZE
    n_mem = mem_prompt.shape[1]
    tp_rows = bp * seq
    TM = 512

    xp = x_prompt.reshape(tp_rows, D_MODEL)
    xs = x_sample.reshape(bs, D_MODEL)
    tab_p = _rope_tables(jnp.arange(seq))
    tab_s = _rope_tables(jnp.full((1,), past))
    cache_k4 = jnp.transpose(cache_k, (0, 1, 3, 4, 5, 2)).reshape(DEPTH, -1, DA_WIDTH, PAGE_SIZE)
    cache_v4 = cache_v.reshape(DEPTH, -1, PAGE_SIZE * DA_HEADS, DA_DV)
    mem_k_all = _mem_rows(cache_mem_k, DEPTH * bs)
    mem_v_all = _mem_rows(cache_mem_v, DEPTH * bs)
    state_all = state_ssm.reshape(DEPTH * bs, SSM_WIDTH, SSM_STATE)
    conv_hist = jnp.transpose(state_conv, (0, 2, 1, 3))
    pool_hist = jnp.transpose(state_pool, (0, 2, 1, 3))
    w_in_t = jnp.transpose(w_in, (2, 0, 1))
    mem_flat = mem_prompt.reshape(bp * n_mem, D_MODEL)
    row = lambda v: v.reshape(1, -1)

    outs = {k: [] for k in ("kp", "vp", "ks", "vs", "mk", "mv", "hp", "hs", "cp", "cs", "pp", "ps")}
    for l in range(DEPTH):
        lam_init = 0.8 - 0.6 * math.exp(-0.3 * l)
        w_cat = _pack_w_in(w_in_t[:, l, :])
        lam_vecs = (row(lam_q1[l]), row(lam_k1[l]), row(lam_q2[l]), row(lam_k2[l]))
        nw = row(da_norm_w[l])
        dtb, alog = _pad_heads(dt_bias[l]), _pad_heads(a_log[l])
        dsk = row(jnp.repeat(d_skip[l], SSM_HEADDIM))
        snw = row(ssm_norm_w[l])
        pwbd = _block_diag(pool_w[l])
        psc = row(pool_scale[l])
        wo_b = w_out[l].astype(BF16)
        wo_parts = (wo_b[:DA_WIDTH], wo_b[DA_WIDTH:DA_WIDTH + SSM_WIDTH], wo_b[DA_WIDTH + SSM_WIDTH:])
        g1, b1, g2, b2, g3, b3 = (row(ln1_g[l]), row(ln1_b[l]), row(ln2_g[l]), row(ln2_b[l]),
                                  row(ln3_g[l]), row(ln3_b[l]))

        qp, kb, vb, kpg, vrow, zp, xbcp, pup, dtp = _proj_in(xp, w_cat, tab_p, prompt=True, tm=TM, seq=seq,
                                                             rows=tp_rows)
        oa_p = _flash_attn(qp, kb, vb, lam_vecs, nw.reshape(DA_DV, 1), batch=bp, seq=seq, tq=512, tk=512,
                           lam_init=lam_init)
        y_p, h_p = _ssd_prompt(xbcp, zp, dtp, conv_w[l], row(conv_b[l]), dtb, alog, dsk, snw, batch=bp, seq=seq)
        yp_p = _pool_prompt(pup, pwbd, psc, batch=bp, seq=seq, tp=512)

        q_s, k_s, v_s, z_s, xbc_s, pu_s, dt_s = _proj_in(xs, w_cat, tab_s, prompt=False, tm=bs, seq=1, rows=bs)
        oa_s = _paged_attn(q_s, k_s, v_s, cache_k4, cache_v4, page_table, lam_vecs, nw, layer=l, lam_init=lam_init)
        xact, dt_a, dec, yp_s, conv_new, pool_new = _sample_prep(
            xbc_s, conv_hist[l], conv_w[l], row(conv_b[l]), dt_s, dtb, alog, pu_s, pool_hist[l], pwbd, psc, pos0=past)
        y_s, h_s = _ssd_step(state_all, l, xact, z_s, dt_a, dec, dsk, snw)
        xs = _mm_res_ln((oa_s, y_s, yp_s), wo_parts, xs, g1, b1, tm=bs)

        wq_b, wk_b, wv_b, wom_b = (wq_mem[l].astype(BF16), wk_mem[l].astype(BF16), wv_mem[l].astype(BF16),
                                   wo_mem[l].astype(BF16))
        mk, mkb = _matmul(mem_flat, wk_b, tm=bp * n_mem // 2, dtypes=(F32, BF16))
        mv, mvb = _matmul(mem_flat, wv_b, tm=bp * n_mem // 2, dtypes=(F32, BF16))
        router = _router_params(moe_router[l // 2], moe_router_b[l // 2]) if l % 2 == 1 else None
        res = _mix_mem_prompt((oa_p, y_p, yp_p), wo_parts, xp, g1, b1, mkb, mvb, wq_b, wom_b, g2, b2, router,
                              batch=bp, seq=seq, tm=TM)
        xp, route_p = res if router else (res[0], None)
        (qm_s,) = _matmul(xs, wq_b, tm=bs, scale=MEM_HD ** -0.5)
        om_s = _mem_attn_sample(qm_s, mem_k_all, mem_v_all, layer=l, group=4)
        xs = _mm_res_ln((om_s,), (wom_b,), xs, g2, b2, tm=bs)

        j = l // 2
        if l % 2 == 0:
            fw1, fw3, fw2 = ffn_w1[j].astype(BF16), ffn_w3[j].astype(BF16), ffn_w2[j].astype(BF16)
            xp = _ffn(xp, fw1, fw3, fw2, g3, b3, tm=TM, fc=256)
            xs = _ffn(xs, fw1, fw3, fw2, g3, b3, tm=bs, fc=256)
        else:
            xp, xs = _moe(xp, xs, route_p, router, moe_w1[j], moe_w3[j], moe_w2[j], g3, b3,
                          tm_e=1024, sub=256, fc=512, tm_p=512)

        outs["kp"].append(jnp.transpose(kpg.reshape(bp, seq // PAGE_SIZE, DA_HEADS, 2, DA_DK, PAGE_SIZE),
                                        (0, 1, 5, 2, 3, 4)))
        outs["vp"].append(vrow.reshape(bp, seq // PAGE_SIZE, PAGE_SIZE, DA_HEADS, DA_DV))
        outs["ks"].append(k_s.reshape(bs, 1, DA_HEADS, 2, DA_DK))
        outs["vs"].append(v_s.reshape(bs, 1, DA_HEADS, DA_DV))
        outs["mk"].append(mk.reshape(bp, n_mem, MEM_HEADS, MEM_HD))
        outs["mv"].append(mv.reshape(bp, n_mem, MEM_HEADS, MEM_HD))
        outs["hp"].append(h_p)
        outs["hs"].append(h_s)
        outs["cp"].append(xbcp.reshape(bp, seq, CONV_CH)[:, seq - (SSM_CONV - 1):])
        outs["cs"].append(jnp.transpose(conv_new, (1, 0, 2)))
        outs["pp"].append(pup.reshape(bp, seq, POOL_WIDTH)[:, seq - POOL_BUF:])
        outs["ps"].append(jnp.transpose(pool_new, (1, 0, 2)))

    st = lambda k: jnp.stack(outs[k])
    return (xp.reshape(bp, seq, D_MODEL), xs.reshape(bs, 1, D_MODEL), st("kp"), st("vp"), st("ks"), st("vs"),
            st("mk"), st("mv"), st("hp"), st("hs"), st("cp"), st("cs"), st("pp"), st("ps"))
```

```python
import functools
import math

import jax
import jax.numpy as jnp
from jax import lax
from jax.experimental import pallas as pl
from jax.experimental.pallas import tpu as pltpu

F32 = jnp.float32
BF16 = jnp.bfloat16

D_MODEL = 1024
DEPTH = 2
PAGE_SIZE = 128
DA_HEADS = 4
DA_WIDTH = 512
DA_DV = 128
DA_DK = 64
ROT_DIM = 16
ROPE_THETA = 500000.0
SSM_WIDTH = 256
SSM_HEADS = 4
SSM_HEADDIM = 64
SSM_STATE = 128
SSM_CONV = 4
SSM_CHUNK = 128
CONV_CH = 768
POOL_WIDTH = 256
POOL_WINDOWS = (2, 4, 8, 16)
POOL_GDIM = 64
POOL_BUF = 15
MEM_HEADS = 4
MEM_HD = 256
N_EXPERTS = 8
ALPHA = (2.0 * DEPTH) ** 0.25
LN_EPS = 1e-5
RMS_EPS = 1e-6

LANES = 128
DT_PAD = LANES
_C_Q, _C_K, _C_V, _C_Z, _C_X, _C_P, _C_DT, _C_END = 0, 512, 1024, 1536, 1792, 2560, 2816, 2944

ROW_TILE = 512
ATTN_TILE = 512
POOL_TILE = 512
FFN_CHUNK = 256
EXPERT_TILE = 1024
EXPERT_SUB = 256
EXPERT_CHUNK = 512
MEM_SAMPLE_GROUP = 4

NEG_INF = float("-inf")
Q_SCALE = DA_DK ** -0.5 * math.log2(math.e)


def _cparams(sem, vmem_mb=48):
    return pltpu.CompilerParams(dimension_semantics=sem, vmem_limit_bytes=vmem_mb * 1024 * 1024)


def _dot(a, b):
    return jnp.dot(a.astype(BF16), b.astype(BF16), preferred_element_type=F32)


def _dot_nt(a, b):
    return lax.dot_general(a.astype(BF16), b.astype(BF16), (((1,), (1,)), ((), ())),
                           preferred_element_type=F32)


def _layer_norm(x, g, b):
    mu = jnp.mean(x, axis=-1, keepdims=True)
    xc = x - mu
    var = jnp.mean(xc * xc, axis=-1, keepdims=True)
    return xc * lax.rsqrt(var + LN_EPS) * g + b


def _silu(x):
    return x * (1.0 / (1.0 + jnp.exp(-x)))


def _softplus(x):
    return jnp.maximum(x, 0.0) + jnp.log(1.0 + jnp.exp(-jnp.abs(x)))


def _const_spec(shape):
    nd = len(shape)
    return pl.BlockSpec(shape, lambda *_: (0,) * nd)


def _rope_tables(pos):
    half = ROT_DIM // 2
    d = jnp.arange(LANES) % DA_DK
    inv = ROPE_THETA ** (-(d % half).astype(F32) * 2.0 / ROT_DIM)
    ang = pos.astype(F32)[:, None] * inv[None, :]
    cos, sin = jnp.cos(ang), jnp.sin(ang)
    cc = jnp.where(d < ROT_DIM, cos, 1.0)
    s1 = jnp.where(d < half, -sin, 0.0)
    s2 = jnp.where((d >= half) & (d < ROT_DIM), sin, 0.0)
    return cc, s1, s2


def _proj_in_kernel(x_ref, w_ref, wt_ref, c_ref, s1_ref, s2_ref, *out_refs, prompt):
    xb = x_ref[...].astype(BF16)
    cc, s1, s2 = c_ref[...], s1_ref[...], s2_ref[...]

    def rope(t):
        outs = []
        for h in range(DA_HEADS):
            th = t[:, h * LANES:(h + 1) * LANES]
            outs.append(th * cc + pltpu.roll(th, LANES - ROT_DIM // 2, 1) * s1
                        + pltpu.roll(th, ROT_DIM // 2, 1) * s2)
        return jnp.concatenate(outs, axis=1)

    def seg(a, b):
        ref, off = (w_ref, 0) if b <= _C_P else (wt_ref, _C_P)
        return lax.dot_general(xb, ref[a - off:b - off, :], (((1,), (1,)), ((), ())), preferred_element_type=F32)

    q = rope(seg(_C_Q, _C_K)) * Q_SCALE
    k = rope(seg(_C_K, _C_V))
    v = seg(_C_V, _C_Z)
    if prompt:
        q_ref, kb_ref, vb_ref, kpg_ref, vrow_ref, z_ref, xbc_ref, pu_ref, dt_ref = out_refs
        kb_ref[...] = k.astype(BF16)
        vb_ref[...] = v.astype(BF16)
        for pg in range(kpg_ref.shape[0]):
            kpg_ref[pg] = k[pg * PAGE_SIZE:(pg + 1) * PAGE_SIZE, :].T
        for h in range(DA_HEADS):
            vrow_ref[pl.ds(h, v.shape[0], stride=DA_HEADS), :] = v[:, h * DA_DV:(h + 1) * DA_DV]
    else:
        q_ref, k_ref, v_ref, z_ref, xbc_ref, pu_ref, dt_ref = out_refs
        k_ref[...] = k
        v_ref[...] = v
    q_ref[...] = q
    z_ref[...] = seg(_C_Z, _C_X)
    xbc_ref[...] = seg(_C_X, _C_P)
    pu_ref[...] = seg(_C_P, _C_DT)
    dt_ref[...] = seg(_C_DT, _C_END)


def _proj_in(x, w_cat, tables, *, prompt, tm, seq, rows, row0=0):
    m = rows
    grid = (m // tm,)
    xrow = pl.BlockSpec((tm, D_MODEL), lambda i: (row0 + i, 0))

    def rows_out(w, dt):
        return pl.BlockSpec((tm, w), lambda i: (i, 0)), jax.ShapeDtypeStruct((m, w), dt)

    if prompt:
        npos = seq // tm
        ppt = tm // PAGE_SIZE
        tab = pl.BlockSpec((tm, LANES), lambda i: (i % npos, 0))
        outs = [rows_out(DA_WIDTH, F32), rows_out(DA_WIDTH, BF16), rows_out(DA_WIDTH, BF16),
                (pl.BlockSpec((ppt, DA_WIDTH, PAGE_SIZE), lambda i: (i, 0, 0)),
                 jax.ShapeDtypeStruct((m // PAGE_SIZE, DA_WIDTH, PAGE_SIZE), F32)),
                (pl.BlockSpec((tm * DA_HEADS, DA_DV), lambda i: (i, 0)),
                 jax.ShapeDtypeStruct((m * DA_HEADS, DA_DV), F32))]
    else:
        tab = _const_spec((1, LANES))
        outs = [rows_out(DA_WIDTH, F32), rows_out(DA_WIDTH, F32), rows_out(DA_WIDTH, F32)]
    outs += [rows_out(SSM_WIDTH, F32), rows_out(CONV_CH, F32), rows_out(POOL_WIDTH, F32), rows_out(DT_PAD, F32)]
    return pl.pallas_call(
        functools.partial(_proj_in_kernel, prompt=prompt),
        grid=grid,
        in_specs=[xrow, _const_spec(w_cat[0].shape), _const_spec(w_cat[1].shape), tab, tab, tab],
        out_specs=[o[0] for o in outs],
        out_shape=[o[1] for o in outs],
        compiler_params=_cparams(("parallel",)),
        name="proj_in",
    )(x, *w_cat, *tables)


def _lam_value(lq1, lk1, lq2, lk2, lam_init):
    return (jnp.exp(jnp.sum(lq1 * lk1, axis=1, keepdims=True))
            - jnp.exp(jnp.sum(lq2 * lk2, axis=1, keepdims=True)) + lam_init)


def _flash_kernel(q_ref, k_ref, v_ref, lq1_ref, lk1_ref, lq2_ref, lk2_ref, nw_ref, o_ref,
                  vt_sc, s_sc, p_sc, m_sc, acc_sc, *, tq, tk, lam_init):
    i = pl.program_id(2)
    n_kt, aug, _ = vt_sc.shape

    @pl.when(i == 0)
    def _():
        ones = jnp.ones((aug - DA_DV, tk), BF16)
        for j in range(n_kt):
            vt = v_ref[j * tk:(j + 1) * tk, :].astype(F32).T.astype(BF16)
            vt_sc[j] = jnp.concatenate([vt, ones], axis=0)

    qt = q_ref[...].T
    drow = lax.broadcasted_iota(jnp.int32, qt.shape, 0)
    qts = (jnp.where(drow < DA_DK, qt, 0.0).astype(BF16), jnp.where(drow >= DA_DK, qt, 0.0).astype(BF16))

    def stage(j, cur, mask=None, last=False):
        nxt = 1 - cur
        if not last:
            kt = k_ref[pl.ds(pl.multiple_of((j + 1) * tk, tk), tk), :]
            for c in range(2):
                s_sc[nxt, c] = jnp.dot(kt, qts[c], preferred_element_type=F32)
        vta = vt_sc[jnp.maximum(j - 1, 0)]
        for c in range(2):
            pv = jnp.dot(vta, p_sc[cur, c], preferred_element_type=F32)
            s = s_sc[cur, c]
            if mask is not None:
                s = jnp.where(mask, s, NEG_INF)
            m = m_sc[c]
            m_new = jnp.maximum(m, jnp.max(s, axis=0, keepdims=True))
            p_sc[nxt, c] = jnp.exp2(s - m_new).astype(BF16)
            acc_sc[c] = jnp.exp2(m - m_new) * (acc_sc[c] + pv)
            m_sc[c] = m_new

    kt0 = k_ref[0:tk, :]
    for c in range(2):
        s_sc[0, c] = jnp.dot(kt0, qts[c], preferred_element_type=F32)
        p_sc[0, c] = jnp.zeros((tk, tq), BF16)
        m_sc[c] = jnp.full((1, tq), NEG_INF, F32)
        acc_sc[c] = jnp.zeros((aug, tq), F32)

    def pair(t, carry):
        stage(2 * t, 0)
        stage(2 * t + 1, 1)
        return carry

    lax.fori_loop(0, i // 2, pair, 0)
    odd = i % 2 == 1

    @pl.when(odd)
    def _():
        stage(i - 1, 0)

    krow = lax.broadcasted_iota(jnp.int32, (tk, tq), 0)
    qcol = lax.broadcasted_iota(jnp.int32, (tk, tq), 1)

    def finish(cur):
        stage(i, cur, mask=krow <= qcol, last=True)
        vta = vt_sc[i]
        a0, a1 = (acc_sc[c] + jnp.dot(vta, p_sc[1 - cur, c], preferred_element_type=F32) for c in range(2))
        lam = _lam_value(lq1_ref[...], lk1_ref[...], lq2_ref[...], lk2_ref[...], lam_init)
        o = a0[:DA_DV] * (1.0 / a0[DA_DV:DA_DV + 1]) - lam * (a1[:DA_DV] * (1.0 / a1[DA_DV:DA_DV + 1]))
        ms = jnp.mean(o * o, axis=0, keepdims=True)
        o = o * lax.rsqrt(ms + RMS_EPS) * nw_ref[...] * (1.0 - lam_init)
        o_ref[...] = o.T

    pl.when(odd)(lambda: finish(1))
    pl.when(jnp.logical_not(odd))(lambda: finish(0))


def _flash_attn(q, kb, v, lam_vecs, norm_w_col, *, batch, seq, tq, tk, lam_init):
    assert tq == tk
    q3 = q.reshape(batch, seq, DA_WIDTH)
    k3 = kb.reshape(batch, seq, DA_WIDTH)
    v3 = v.reshape(batch, seq, DA_WIDTH)
    qspec = pl.BlockSpec((None, tq, DA_DV), lambda b, h, i: (b, i, h))
    kvspec = pl.BlockSpec((None, seq, DA_DV), lambda b, h, i: (b, 0, h))
    vec = _const_spec((1, DA_DK))
    aug = DA_DV + 16
    out = pl.pallas_call(
        functools.partial(_flash_kernel, tq=tq, tk=tk, lam_init=lam_init),
        grid=(batch, DA_HEADS, seq // tq),
        in_specs=[qspec, kvspec, kvspec, vec, vec, vec, vec, _const_spec((DA_DV, 1))],
        out_specs=qspec,
        out_shape=jax.ShapeDtypeStruct((batch, seq, DA_WIDTH), F32),
        scratch_shapes=[pltpu.VMEM((seq // tk, aug, tk), BF16), pltpu.VMEM((2, 2, tk, tq), F32),
                        pltpu.VMEM((2, 2, tk, tq), BF16), pltpu.VMEM((2, 1, tq), F32),
                        pltpu.VMEM((2, aug, tq), F32)],
        compiler_params=_cparams(("parallel", "parallel", "arbitrary")),
        name="flash_diff_attn",
    )(q3, k3, v3, *lam_vecs, norm_w_col)
    return out.reshape(batch * seq, DA_WIDTH)


def _paged_kernel(pt_ref, q_ref, ks_ref, vs_ref, lq1_ref, lk1_ref, lq2_ref, lk2_ref, nw_ref, *rest,
                  n_pages, lam_init):
    k_refs = rest[:n_pages]
    v_refs = rest[n_pages:2 * n_pages]
    o_ref = rest[2 * n_pages]
    q = q_ref[...]
    nhc = 2 * DA_HEADS
    jj = lax.broadcasted_iota(jnp.int32, (2 * nhc, DA_WIDTH), 0)
    ll = lax.broadcasted_iota(jnp.int32, (2 * nhc, DA_WIDTH), 1)
    qblk = jnp.where((jj < nhc) & ((ll >> 6) == (jj & (DA_HEADS - 1)) * 2 + (jj >> 2)), q, 0.0)
    qb = qblk.astype(BF16)

    s_all = jnp.concatenate(
        [jnp.dot(qb, k_refs[j][...].astype(BF16), preferred_element_type=F32)[:nhc] for j in range(n_pages)],
        axis=1)
    s_self = jnp.sum(qblk[:nhc] * ks_ref[...], axis=1, keepdims=True)
    m = jnp.maximum(jnp.max(s_all, axis=1, keepdims=True), s_self)
    p = jnp.exp2(s_all - m)
    p_self = jnp.exp2(s_self - m)
    inv_l = 1.0 / (jnp.sum(p, axis=1, keepdims=True) + p_self)
    lam = _lam_value(lq1_ref[...], lk1_ref[...], lq2_ref[...], lk2_ref[...], lam_init)
    pn = p * inv_l
    pn_self = p_self * inv_l
    w = pn - lam * pltpu.roll(pn, DA_HEADS, 0)
    w_self = pn_self - lam * pltpu.roll(pn_self, DA_HEADS, 0)
    w16 = jnp.concatenate([w, jnp.zeros_like(w)], axis=0).astype(BF16)
    accs = [jnp.zeros((2 * nhc, DA_DV), F32) for _ in range(DA_HEADS)]
    for j in range(n_pages):
        wj = w16[:, j * PAGE_SIZE:(j + 1) * PAGE_SIZE]
        for h in range(DA_HEADS):
            vh = v_refs[j][pl.ds(h, PAGE_SIZE, stride=DA_HEADS), :]
            accs[h] = accs[h] + jnp.dot(wj, vh.astype(BF16), preferred_element_type=F32)
    outs = []
    for h in range(DA_HEADS):
        o = accs[h][h:h + 1, :] + w_self[h:h + 1, :] * vs_ref[:, h * DA_DV:(h + 1) * DA_DV]
        ms = jnp.mean(o * o, axis=1, keepdims=True)
        outs.append(o * lax.rsqrt(ms + RMS_EPS) * nw_ref[...] * (1.0 - lam_init))
    o_ref[...] = jnp.concatenate(outs, axis=1)


def _paged_attn(q, k_self, v_self, cache_k4, cache_v4, page_table, lam_vecs, norm_w4, *, layer, lam_init):
    bs, n_pages = page_table.shape
    pt = page_table.reshape(-1)
    one = pl.BlockSpec((None, 1, DA_WIDTH), lambda b, pt: (b, 0, 0))
    vec = pl.BlockSpec((1, DA_DK), lambda b, pt: (0, 0))

    def page_spec(j):
        return pl.BlockSpec((None, None, DA_WIDTH, PAGE_SIZE), lambda b, pt: (layer, pt[b * n_pages + j], 0, 0))

    specs = [one, one, one, vec, vec, vec, vec, pl.BlockSpec((1, DA_DV), lambda b, pt: (0, 0))]
    specs += [page_spec(j) for j in range(n_pages)] * 2
    out = pl.pallas_call(
        functools.partial(_paged_kernel, n_pages=n_pages, lam_init=lam_init),
        grid_spec=pltpu.PrefetchScalarGridSpec(
            num_scalar_prefetch=1, grid=(bs,), in_specs=specs, out_specs=one),
        out_shape=jax.ShapeDtypeStruct((bs, 1, DA_WIDTH), F32),
        compiler_params=_cparams(("arbitrary",)),
        name="paged_diff_attn",
    )(pt, q.reshape(bs, 1, DA_WIDTH), k_self.reshape(bs, 1, DA_WIDTH), v_self.reshape(bs, 1, DA_WIDTH),
      *lam_vecs, norm_w4, *([cache_k4] * n_pages), *([cache_v4] * n_pages))
    return out.reshape(bs, DA_WIDTH)


def _head_expand(cols, width, per):
    rows = cols.shape[0]
    lane = lax.broadcasted_iota(jnp.int32, (rows, width), 1)
    out = jnp.broadcast_to(cols[:, 0:1], (rows, width))
    for h in range(1, width // per):
        out = jnp.where(lane >= h * per, cols[:, h:h + 1], out)
    return out


def _ssd_kernel(xbc_ref, z_ref, dt_ref, cw_ref, cb_ref, dtb_ref, alog_ref, dsk_ref, nw_ref,
                y_ref, st_ref, tail_sc, state_sc):
    c = pl.program_id(0)

    @pl.when(c == 0)
    def _():
        tail_sc[...] = jnp.zeros_like(tail_sc)
        state_sc[...] = jnp.zeros_like(state_sc)

    for b in range(xbc_ref.shape[0]):
        state = _ssd_chunk(xbc_ref.at[b], z_ref.at[b], dt_ref.at[b], cw_ref, cb_ref, dtb_ref, alog_ref, dsk_ref,
                           nw_ref, y_ref.at[b], tail_sc.at[b], state_sc.at[b])

        @pl.when(c == pl.num_programs(0) - 1)
        def _():
            st_ref[b] = state


def _ssd_chunk(xbc_ref, z_ref, dt_ref, cw_ref, cb_ref, dtb_ref, alog_ref, dsk_ref, nw_ref, y_ref, tail_sc,
               state_sc):
    q = SSM_CHUNK
    u = xbc_ref[...]
    full = jnp.concatenate([tail_sc[...], u], axis=0)
    tail_sc[...] = u[q - 8:, :]
    conv = full * cw_ref[SSM_CONV - 1:SSM_CONV, :]
    for j in range(1, SSM_CONV):
        conv = conv + pltpu.roll(full, j, 0) * cw_ref[SSM_CONV - 1 - j:SSM_CONV - j, :]
    xbc = _silu(conv[8:, :] + cb_ref[...])
    xs = xbc[:, :SSM_WIDTH]
    bm = (xbc[:, SSM_WIDTH:SSM_WIDTH + SSM_STATE], xbc[:, SSM_WIDTH + SSM_STATE:SSM_WIDTH + 2 * SSM_STATE])
    cm = (xbc[:, SSM_WIDTH + 2 * SSM_STATE:SSM_WIDTH + 3 * SSM_STATE], xbc[:, SSM_WIDTH + 3 * SSM_STATE:])

    dt = _softplus(dt_ref[...] + dtb_ref[...])
    dta = dt * (-jnp.exp(alog_ref[...]))
    ti = lax.broadcasted_iota(jnp.int32, (q, q), 0)
    si = lax.broadcasted_iota(jnp.int32, (q, q), 1)
    causal = si <= ti
    tri = jnp.where(causal, 1.0, 0.0)
    a_cs = jnp.dot(tri, dta, precision=lax.Precision.HIGHEST, preferred_element_type=F32)
    a_cs_t = a_cs.T
    last = a_cs[q - 1:q, :]

    dt_x = _head_expand(dt, SSM_WIDTH, SSM_HEADDIM)
    xdt = xs * dt_x
    xdt_end = xdt * _head_expand(jnp.exp(last - a_cs), SSM_WIDTH, SSM_HEADDIM)
    lane = lax.broadcasted_iota(jnp.int32, (q, SSM_WIDTH), 1)

    cb = [_dot_nt(cm[g], bm[g]) for g in range(2)]
    y_diag = jnp.zeros((q, SSM_WIDTH), F32)
    for h in range(SSM_HEADS):
        seg = a_cs[:, h:h + 1] - a_cs_t[h:h + 1, :]
        decay = jnp.exp(jnp.where(causal, seg, NEG_INF))
        xh = jnp.where((lane >= h * SSM_HEADDIM) & (lane < (h + 1) * SSM_HEADDIM), xdt, 0.0)
        y_diag = y_diag + _dot(cb[h // 2] * decay, xh)

    state = state_sc[...]
    sb = state.astype(BF16)
    y_off = jnp.where(lane < 2 * SSM_HEADDIM, _dot_nt(cm[0], sb), _dot_nt(cm[1], sb))
    y_off = y_off * _head_expand(jnp.exp(a_cs), SSM_WIDTH, SSM_HEADDIM)

    xt = xdt_end.T
    rowi = lax.broadcasted_iota(jnp.int32, (SSM_WIDTH, SSM_STATE), 0)
    new = jnp.where(rowi < 2 * SSM_HEADDIM, _dot(xt, bm[0]), _dot(xt, bm[1]))
    e_last = jnp.exp(last)
    dec_rows = jnp.broadcast_to(e_last[:, 0:1], (SSM_WIDTH, SSM_STATE))
    for h in range(1, SSM_HEADS):
        dec_rows = jnp.where(rowi >= h * SSM_HEADDIM, e_last[:, h:h + 1], dec_rows)
    state = state * dec_rows + new
    state_sc[...] = state

    y = y_diag + y_off + dsk_ref[...] * xs
    y = y * _silu(z_ref[...])
    ms = jnp.mean(y * y, axis=1, keepdims=True)
    y_ref[...] = y * lax.rsqrt(ms + RMS_EPS) * nw_ref[...]
    return state


def _ssd_prompt(xbc, z, dt_raw, conv_w, conv_b, dt_bias, a_log, d_skip_x, norm_w, *, batch, seq):
    nc = seq // SSM_CHUNK
    blk = lambda w: pl.BlockSpec((batch, SSM_CHUNK, w), lambda c: (0, c, 0))
    y, st = pl.pallas_call(
        _ssd_kernel,
        grid=(nc,),
        in_specs=[blk(CONV_CH), blk(SSM_WIDTH), blk(DT_PAD),
                  _const_spec((SSM_CONV, CONV_CH)), _const_spec((1, CONV_CH)),
                  _const_spec((1, DT_PAD)), _const_spec((1, DT_PAD)),
                  _const_spec((1, SSM_WIDTH)), _const_spec((1, SSM_WIDTH))],
        out_specs=[blk(SSM_WIDTH), pl.BlockSpec((batch, SSM_WIDTH, SSM_STATE), lambda c: (0, 0, 0))],
        out_shape=[jax.ShapeDtypeStruct((batch, seq, SSM_WIDTH), F32),
                   jax.ShapeDtypeStruct((batch, SSM_WIDTH, SSM_STATE), F32)],
        scratch_shapes=[pltpu.VMEM((batch, 8, CONV_CH), F32), pltpu.VMEM((batch, SSM_WIDTH, SSM_STATE), F32)],
        compiler_params=_cparams(("arbitrary",)),
        name="ssd_prompt",
    )(xbc.reshape(batch, seq, CONV_CH), z.reshape(batch, seq, SSM_WIDTH), dt_raw.reshape(batch, seq, DT_PAD),
      conv_w, conv_b, dt_bias, a_log, d_skip_x, norm_w)
    return y.reshape(batch * seq, SSM_WIDTH), st.reshape(batch, SSM_HEADS, SSM_HEADDIM, SSM_STATE)


def _pool_select(sums, inv_cnt, lane):
    out = sums[POOL_WINDOWS[0]] * inv_cnt[POOL_WINDOWS[0]]
    for g in range(1, len(POOL_WINDOWS)):
        w = POOL_WINDOWS[g]
        out = jnp.where(lane >= g * POOL_GDIM, sums[w] * inv_cnt[w], out)
    return out


def _pool_kernel(u_ref, w_ref, sc_ref, y_ref, tail_sc, *, tp):
    c = pl.program_id(1)
    halo = 16

    @pl.when(c == 0)
    def _():
        tail_sc[...] = jnp.zeros_like(tail_sc)

    u = u_ref[...]
    full = jnp.concatenate([tail_sc[...], u], axis=0)
    tail_sc[...] = u[tp - halo:, :]
    sums = {}
    run = full
    w = 1
    while w < max(POOL_WINDOWS):
        run = run + pltpu.roll(run, w, 0)
        w *= 2
        sums[w] = run[halo:, :]
    pos = c * tp + lax.broadcasted_iota(jnp.int32, (tp, 1), 0)
    inv_cnt = {w: 1.0 / jnp.minimum(w, pos + 1).astype(F32) for w in POOL_WINDOWS}
    lane = lax.broadcasted_iota(jnp.int32, (tp, POOL_WIDTH), 1)
    d = _pool_select(sums, inv_cnt, lane) - u
    y_ref[...] = _dot(d, w_ref[...]) * sc_ref[...]


def _pool_prompt(pu, pool_wbd, pool_scale, *, batch, seq, tp):
    blk = pl.BlockSpec((None, tp, POOL_WIDTH), lambda b, c: (b, c, 0))
    y = pl.pallas_call(
        functools.partial(_pool_kernel, tp=tp),
        grid=(batch, seq // tp),
        in_specs=[blk, _const_spec((POOL_WIDTH, POOL_WIDTH)), _const_spec((1, POOL_WIDTH))],
        out_specs=blk,
        out_shape=jax.ShapeDtypeStruct((batch, seq, POOL_WIDTH), F32),
        scratch_shapes=[pltpu.VMEM((16, POOL_WIDTH), F32)],
        compiler_params=_cparams(("parallel", "arbitrary")),
        name="pool_prompt",
    )(pu.reshape(batch, seq, POOL_WIDTH), pool_wbd, pool_scale)
    return y.reshape(batch * seq, POOL_WIDTH)


def _sample_prep_kernel(xbc_ref, cbuf_ref, cw_ref, cb_ref, dt_ref, dtb_ref, alog_ref, pu_ref, pbuf_ref,
                        pw_ref, psc_ref, xact_ref, dto_ref, dec_ref, yp_ref, cnew_ref, pnew_ref, *, pos0):
    u = xbc_ref[...]
    for j in range(SSM_CONV - 2):
        cnew_ref[j] = cbuf_ref[j + 1]
    cnew_ref[SSM_CONV - 2] = u
    for j in range(POOL_BUF - 1):
        pnew_ref[j] = pbuf_ref[j + 1]
    pnew_ref[POOL_BUF - 1] = pu_ref[...]
    conv = u * cw_ref[SSM_CONV - 1:SSM_CONV, :] + cb_ref[...]
    for j in range(SSM_CONV - 1):
        conv = conv + cbuf_ref[j] * cw_ref[j:j + 1, :]
    xact_ref[...] = _silu(conv)
    dt = _softplus(dt_ref[...] + dtb_ref[...])
    dto_ref[...] = dt
    dec_ref[...] = jnp.exp(dt * (-jnp.exp(alog_ref[...])))
    pu = pu_ref[...]
    run = pu
    sums = {}
    for j in range(1, max(POOL_WINDOWS)):
        run = run + pbuf_ref[POOL_BUF - j]
        if j + 1 in POOL_WINDOWS:
            sums[j + 1] = run
    inv_cnt = {w: 1.0 / float(min(w, pos0 + 1)) for w in POOL_WINDOWS}
    lane = lax.broadcasted_iota(jnp.int32, pu.shape, 1)
    d = _pool_select(sums, inv_cnt, lane) - pu
    yp_ref[...] = _dot(d, pw_ref[...]) * psc_ref[...]


def _sample_prep(xbc, conv_buf, conv_w, conv_b, dt_raw, dt_bias, a_log, pu, pool_buf, pool_wbd, pool_scale, *, pos0):
    bs = xbc.shape[0]
    args = (xbc, conv_buf, conv_w, conv_b, dt_raw, dt_bias, a_log, pu, pool_buf, pool_wbd, pool_scale)
    shapes = [(bs, CONV_CH), (bs, DT_PAD), (bs, DT_PAD), (bs, POOL_WIDTH), conv_buf.shape, pool_buf.shape]
    return pl.pallas_call(
        functools.partial(_sample_prep_kernel, pos0=pos0),
        grid=(1,),
        in_specs=[_const_spec(a.shape) for a in args],
        out_specs=[_const_spec(s) for s in shapes],
        out_shape=[jax.ShapeDtypeStruct(s, F32) for s in shapes],
        compiler_params=_cparams(("arbitrary",)),
        name="sample_prep",
    )(*args)


def _ssd_step_kernel(s_ref, x_ref, z_ref, dt_ref, dec_ref, b_ref, c_ref, dsk_ref, nw_ref, y_ref, so_ref, *, group):
    rowi = lax.broadcasted_iota(jnp.int32, (SSM_WIDTH, SSM_STATE), 0)
    for i in range(group):
        brow, crow = b_ref[i], c_ref[i]
        bsel = jnp.where(rowi < 2 * SSM_HEADDIM, brow[:, :SSM_STATE], brow[:, SSM_STATE:])
        csel = jnp.where(rowi < 2 * SSM_HEADDIM, crow[:, :SSM_STATE], crow[:, SSM_STATE:])
        x = x_ref[i]
        sn = s_ref[i] * dec_ref[i] + (x * dt_ref[i]) * bsel
        so_ref[i] = sn
        y = jnp.sum(sn * csel, axis=1, keepdims=True) + dsk_ref[...] * x
        y = y * _silu(z_ref[i])
        ms = jnp.sum(y * y, axis=0, keepdims=True) * (1.0 / SSM_WIDTH)
        y_ref[i] = y * lax.rsqrt(ms + RMS_EPS) * nw_ref[...]


def _ssd_step(state_all, layer, xact, z, dt, dec, d_skip_x, norm_w, *, group=8):
    bs = xact.shape[0]
    nb = bs // group
    rep = lambda t: jnp.repeat(t[:, :SSM_HEADS], SSM_HEADDIM, axis=1).reshape(bs, SSM_WIDTH, 1)
    colb = pl.BlockSpec((group, SSM_WIDTH, 1), lambda b: (b, 0, 0))
    rowb = pl.BlockSpec((group, 1, 2 * SSM_STATE), lambda b: (b, 0, 0))
    st_in = pl.BlockSpec((group, SSM_WIDTH, SSM_STATE), lambda b: (layer * nb + b, 0, 0))
    st_out = pl.BlockSpec((group, SSM_WIDTH, SSM_STATE), lambda b: (b, 0, 0))
    y, st = pl.pallas_call(
        functools.partial(_ssd_step_kernel, group=group),
        grid=(nb,),
        in_specs=[st_in, colb, colb, colb, colb, rowb, rowb,
                  _const_spec((SSM_WIDTH, 1)), _const_spec((SSM_WIDTH, 1))],
        out_specs=[colb, st_out],
        out_shape=[jax.ShapeDtypeStruct((bs, SSM_WIDTH, 1), F32),
                   jax.ShapeDtypeStruct((bs, SSM_WIDTH, SSM_STATE), F32)],
        compiler_params=_cparams(("parallel",)),
        name="ssd_step",
    )(state_all, xact[:, :SSM_WIDTH].reshape(bs, SSM_WIDTH, 1), z.reshape(bs, SSM_WIDTH, 1), rep(dt), rep(dec),
      xact[:, SSM_WIDTH:SSM_WIDTH + 2 * SSM_STATE].reshape(bs, 1, 2 * SSM_STATE),
      xact[:, SSM_WIDTH + 2 * SSM_STATE:].reshape(bs, 1, 2 * SSM_STATE),
      d_skip_x.reshape(SSM_WIDTH, 1), norm_w.reshape(SSM_WIDTH, 1))
    return y.reshape(bs, SSM_WIDTH), st.reshape(bs, SSM_HEADS, SSM_HEADDIM, SSM_STATE)


def _mm_kernel(x_ref, w_ref, *out_refs, scale):
    y = _dot(x_ref[...], w_ref[...])
    if scale != 1.0:
        y = y * scale
    for r in out_refs:
        r[...] = y.astype(r.dtype)


def _matmul(x, w, *, tm, scale=1.0, dtypes=(F32,)):
    m, k = x.shape
    n = w.shape[1]
    res = pl.pallas_call(
        functools.partial(_mm_kernel, scale=scale),
        grid=(m // tm,),
        in_specs=[pl.BlockSpec((tm, k), lambda i: (i, 0)), _const_spec(w.shape)],
        out_specs=[pl.BlockSpec((tm, n), lambda i: (i, 0)) for _ in dtypes],
        out_shape=[jax.ShapeDtypeStruct((m, n), d) for d in dtypes],
        compiler_params=_cparams(("parallel",)),
        name="matmul",
    )(x, w)
    return res


def _mm_res_ln_kernel(*refs, n_in):
    a_refs = refs[:n_in]
    w_refs = refs[n_in:2 * n_in]
    x_ref, g_ref, b_ref, o_ref = refs[2 * n_in:]
    h = _dot(a_refs[0][...], w_refs[0][...])
    for a, w in zip(a_refs[1:], w_refs[1:]):
        h = h + _dot(a[...], w[...])
    o_ref[...] = _layer_norm(ALPHA * x_ref[...] + h, g_ref[...], b_ref[...])


def _mm_res_ln(acts, weights, x, g, b, *, tm, x_row0=0):
    m = acts[0].shape[0]
    row = lambda a: pl.BlockSpec((tm, a.shape[1]), lambda i: (i, 0))
    return pl.pallas_call(
        functools.partial(_mm_res_ln_kernel, n_in=len(acts)),
        grid=(m // tm,),
        in_specs=[row(a) for a in acts] + [_const_spec(w.shape) for w in weights]
        + [pl.BlockSpec((tm, D_MODEL), lambda i: (x_row0 + i, 0)), _const_spec(g.shape), _const_spec(b.shape)],
        out_specs=pl.BlockSpec((tm, D_MODEL), lambda i: (i, 0)),
        out_shape=jax.ShapeDtypeStruct((m, D_MODEL), F32),
        compiler_params=_cparams(("parallel",)),
        name="mm_res_ln",
    )(*acts, *weights, x, g, b)


def _mix_mem_kernel(oa_ref, y_ref, yp_ref, x_ref, wa_ref, wy_ref, wp_ref, g1_ref, b1_ref,
                    mk_ref, mv_ref, wq_ref, wo_ref, g2_ref, b2_ref, *rest, route):
    h = _dot(oa_ref[...], wa_ref[...]) + _dot(y_ref[...], wy_ref[...]) + _dot(yp_ref[...], wp_ref[...])
    x = _layer_norm(ALPHA * x_ref[...] + h, g1_ref[...], b1_ref[...])
    q = (_dot(x, wq_ref[...]) * (MEM_HD ** -0.5)).astype(BF16)
    outs = []
    for hd in range(MEM_HEADS):
        sl = slice(hd * MEM_HD, (hd + 1) * MEM_HD)
        s = lax.dot_general(q[:, sl], mk_ref[:, sl], (((1,), (1,)), ((), ())), preferred_element_type=F32)
        p = jnp.exp(s - jnp.max(s, axis=1, keepdims=True))
        p = p * (1.0 / jnp.sum(p, axis=1, keepdims=True))
        outs.append(jnp.dot(p.astype(BF16), mv_ref[:, sl], preferred_element_type=F32))
    o = jnp.concatenate(outs, axis=1)
    x2 = _layer_norm(ALPHA * x + _dot(o, wo_ref[...]), g2_ref[...], b2_ref[...])
    if route:
        wh_ref, wl_ref, rb_ref, o_ref, r_ref = rest
        r_ref[...] = _route(x2, wh_ref, wl_ref, rb_ref)
    else:
        (o_ref,) = rest
    o_ref[...] = x2


def _mix_mem_prompt(acts, w_parts, x, g1, b1, mkb, mvb, wq, wo, g2, b2, router=None, *, batch, seq, tm):
    n_mem = mkb.shape[0] // batch
    nt = seq // tm
    rblk = lambda w: pl.BlockSpec((tm, w), lambda bb, i: (bb * nt + i, 0))
    mblk = pl.BlockSpec((None, n_mem, D_MODEL), lambda bb, i: (bb, 0, 0))
    const = lambda a: pl.BlockSpec(a.shape, lambda bb, i: (0,) * a.ndim)
    consts = (*w_parts, g1, b1)
    tail = (wq, wo, g2, b2) + (tuple(router) if router else ())
    out_specs = [rblk(D_MODEL)] + ([rblk(LANES)] if router else [])
    out_shape = [jax.ShapeDtypeStruct((batch * seq, D_MODEL), F32)]
    if router:
        out_shape.append(jax.ShapeDtypeStruct((batch * seq, LANES), F32))
    return pl.pallas_call(
        functools.partial(_mix_mem_kernel, route=bool(router)),
        grid=(batch, nt),
        in_specs=[rblk(a.shape[1]) for a in acts] + [rblk(D_MODEL)] + [const(a) for a in consts]
        + [mblk, mblk] + [const(a) for a in tail],
        out_specs=out_specs,
        out_shape=out_shape,
        compiler_params=_cparams(("parallel", "parallel")),
        name="mix_mem_prompt",
    )(*acts, x, *consts, mkb.reshape(batch, n_mem, D_MODEL), mvb.reshape(batch, n_mem, D_MODEL), *tail)


def _mem_sample_kernel(q_ref, mk_ref, mv_ref, o_ref, *, group):
    rows = mk_ref.shape[1] // 8
    for i in range(group):
        q8 = q_ref[i]
        r = jnp.sum(mk_ref[i].reshape(rows, 8, LANES) * q8, axis=2, keepdims=True)
        s = r + pltpu.roll(r, MEM_HEADS, 1)
        p = jnp.exp(s - jnp.max(s, axis=0, keepdims=True))
        p = p * (1.0 / jnp.sum(p, axis=0, keepdims=True))
        o_ref[i] = jnp.sum(mv_ref[i].reshape(rows, 8, LANES) * p, axis=0)


def _mem_rows(cache, n):
    n_mem = cache.shape[2]
    c = cache.reshape(n, n_mem, MEM_HEADS, MEM_HD // LANES, LANES)
    return jnp.transpose(c, (0, 1, 3, 2, 4)).reshape(n, n_mem * 2 * MEM_HEADS, LANES)


def _mem_attn_sample(q, mem_k_rows, mem_v_rows, *, layer, group):
    bs = q.shape[0]
    nb = bs // group
    halves = MEM_HD // LANES
    q8 = jnp.transpose(q.reshape(bs, MEM_HEADS, halves, LANES), (0, 2, 1, 3)).reshape(bs, halves * MEM_HEADS, LANES)
    qblk = pl.BlockSpec((group, halves * MEM_HEADS, LANES), lambda i: (i, 0, 0))
    mblk = pl.BlockSpec((group,) + mem_k_rows.shape[1:], lambda i: (layer * nb + i, 0, 0))
    out = pl.pallas_call(
        functools.partial(_mem_sample_kernel, group=group),
        grid=(nb,),
        in_specs=[qblk, mblk, mblk],
        out_specs=qblk,
        out_shape=jax.ShapeDtypeStruct(q8.shape, F32),
        compiler_params=_cparams(("parallel",)),
        name="mem_attn_sample",
    )(q8, mem_k_rows, mem_v_rows)
    return jnp.transpose(out.reshape(bs, halves, MEM_HEADS, LANES), (0, 2, 1, 3)).reshape(bs, D_MODEL)


def _ffn_kernel(x_ref, w1_ref, w3_ref, w2_ref, g_ref, b_ref, o_ref, *, fc):
    x = x_ref[...]
    xb = x.astype(BF16)
    acc = jnp.zeros(x.shape, F32)
    for c in range(w1_ref.shape[1] // fc):
        sl = slice(c * fc, (c + 1) * fc)
        h = _silu(jnp.dot(xb, w1_ref[:, sl], preferred_element_type=F32)) * jnp.dot(xb, w3_ref[:, sl],
                                                                                    preferred_element_type=F32)
        acc = acc + jnp.dot(h.astype(BF16), w2_ref[sl, :], preferred_element_type=F32)
    o_ref[...] = _layer_norm(ALPHA * x + acc, g_ref[...], b_ref[...])


def _ffn(x, w1, w3, w2, g, b, *, tm, fc):
    m = x.shape[0]
    row = pl.BlockSpec((tm, D_MODEL), lambda i: (i, 0))
    wspec = lambda w: pl.BlockSpec(w.shape, lambda i: (0, 0), pipeline_mode=pl.Buffered(1))
    return pl.pallas_call(
        functools.partial(_ffn_kernel, fc=fc),
        grid=(m // tm,),
        in_specs=[row, wspec(w1), wspec(w3), wspec(w2), _const_spec(g.shape), _const_spec(b.shape)],
        out_specs=row,
        out_shape=jax.ShapeDtypeStruct(x.shape, F32),
        compiler_params=_cparams(("parallel",), vmem_mb=56),
        name="ffn_dense",
    )(x, w1, w3, w2, g, b)


def _route(x, wh_ref, wl_ref, b_ref):
    xh = x.astype(BF16)
    xl = (x - xh.astype(F32)).astype(BF16)
    wh = wh_ref[...]
    logits = (jnp.dot(xh, wh, preferred_element_type=F32) + jnp.dot(xl, wh, preferred_element_type=F32)
              + jnp.dot(xh, wl_ref[...], preferred_element_type=F32))
    lane = lax.broadcasted_iota(jnp.int32, logits.shape, 1)
    logits = jnp.where(lane < N_EXPERTS, logits + b_ref[...], NEG_INF)
    m1 = jnp.max(logits, axis=1, keepdims=True)
    i1 = jnp.min(jnp.where(logits == m1, lane, LANES), axis=1, keepdims=True)
    rest = jnp.where(lane == i1, NEG_INF, logits)
    m2 = jnp.max(rest, axis=1, keepdims=True)
    i2 = jnp.min(jnp.where(rest == m2, lane, LANES), axis=1, keepdims=True)
    e = jnp.exp(m2 - m1)
    g1 = 1.0 / (1.0 + e)
    g2 = e * g1
    return jnp.where(lane == 0, i1.astype(F32), jnp.where(lane == 1, i2.astype(F32),
                     jnp.where(lane == 2, g1, jnp.where(lane == 3, g2, 0.0))))


def _router_kernel(x_ref, wh_ref, wl_ref, b_ref, o_ref):
    o_ref[...] = _route(x_ref[...], wh_ref, wl_ref, b_ref)


def _router(x, w_hi, w_lo, b_pad, *, tm):
    m = x.shape[0]
    return pl.pallas_call(
        _router_kernel,
        grid=(m // tm,),
        in_specs=[pl.BlockSpec((tm, D_MODEL), lambda i: (i, 0)), _const_spec(w_hi.shape), _const_spec(w_lo.shape),
                  _const_spec(b_pad.shape)],
        out_specs=pl.BlockSpec((tm, LANES), lambda i: (i, 0)),
        out_shape=jax.ShapeDtypeStruct((m, LANES), F32),
        compiler_params=_cparams(("parallel",)),
        name="moe_router",
    )(x, w_hi, w_lo, b_pad)


def _expert_kernel(te_ref, tr_ref, x_ref, w1_ref, w3_ref, w2_ref, o_ref, acc_sc, *, sub):
    i = pl.program_id(0)
    j = pl.program_id(1)
    rows = tr_ref[i]
    tm = x_ref.shape[0]

    @pl.when(j == 0)
    def _():
        acc_sc[...] = jnp.zeros_like(acc_sc)

    def swiglu_into_acc(sl, w1, w3, w2):
        xb = x_ref[sl, :].astype(BF16)
        h = _silu(jnp.dot(xb, w1, preferred_element_type=F32)) * jnp.dot(xb, w3, preferred_element_type=F32)
        acc_sc[sl, :] += jnp.dot(h.astype(BF16), w2, preferred_element_type=F32)

    def weights():
        return w1_ref[...].astype(BF16), w3_ref[...].astype(BF16), w2_ref[...].astype(BF16)

    @pl.when(rows == tm)
    def _():
        swiglu_into_acc(slice(None), *weights())

    @pl.when((rows > 0) & (rows < tm))
    def _():
        ws = weights()
        for s in range(tm // sub):
            pl.when(rows > s * sub)(functools.partial(swiglu_into_acc, slice(s * sub, (s + 1) * sub), *ws))

    @pl.when(j == pl.num_programs(1) - 1)
    def _():
        o_ref[...] = acc_sc[...]


def _experts(x_sorted, tile_expert, tile_rows, w1, w3, w2, *, tm, sub, fc):
    n_slots = x_sorted.shape[0]
    nj = w1.shape[2] // fc
    last = nj - 1

    def jj(i, j, tr):
        return jnp.where(tr[i] > 0, j, last)

    xs = pl.BlockSpec((tm, D_MODEL), lambda i, j, te, tr: (i, 0))
    w13 = pl.BlockSpec((None, D_MODEL, fc), lambda i, j, te, tr: (te[i], 0, jj(i, j, tr)))
    w2s = pl.BlockSpec((None, fc, D_MODEL), lambda i, j, te, tr: (te[i], jj(i, j, tr), 0))
    return pl.pallas_call(
        functools.partial(_expert_kernel, sub=sub),
        grid_spec=pltpu.PrefetchScalarGridSpec(
            num_scalar_prefetch=2, grid=(n_slots // tm, nj),
            in_specs=[xs, w13, w13, w2s], out_specs=xs,
            scratch_shapes=[pltpu.VMEM((tm, D_MODEL), F32)]),
        out_shape=jax.ShapeDtypeStruct((n_slots, D_MODEL), F32),
        compiler_params=_cparams(("arbitrary", "arbitrary"), vmem_mb=56),
        name="moe_experts",
    )(tile_expert, tile_rows, x_sorted, w1, w3, w2)


def _dispatch_kernel(slot_ref, x_ref, init_ref, xs_ref, sem, *, tm, tok0):
    del init_ref
    base = (tok0 + pl.program_id(0) * tm) * 2

    def issue(r, carry):
        for k in range(2):
            dst = slot_ref[base + 2 * r + k]
            pltpu.make_async_copy(x_ref.at[pl.ds(r, 1), :], xs_ref.at[pl.ds(dst, 1), :], sem).start()
        return carry

    lax.fori_loop(0, tm, issue, 0, unroll=8)
    for _ in range(2):
        pltpu.make_async_copy(x_ref, x_ref, sem).wait()


def _dispatch(x, slot, buf, *, tm, tok0):
    t = x.shape[0]
    return pl.pallas_call(
        functools.partial(_dispatch_kernel, tm=tm, tok0=tok0),
        grid_spec=pltpu.PrefetchScalarGridSpec(
            num_scalar_prefetch=1, grid=(t // tm,),
            in_specs=[pl.BlockSpec((tm, D_MODEL), lambda i, s: (i, 0)), pl.BlockSpec(memory_space=pl.ANY)],
            out_specs=pl.BlockSpec(memory_space=pl.ANY),
            scratch_shapes=[pltpu.SemaphoreType.DMA(())]),
        out_shape=jax.ShapeDtypeStruct(buf.shape, F32),
        input_output_aliases={2: 0},
        compiler_params=_cparams(("arbitrary",)),
        name="moe_dispatch",
    )(slot, x, buf)


def _combine_ln_kernel(slot_ref, x_ref, r_ref, y_ref, g_ref, b_ref, o_ref, ybuf, sems, *, tm, tok0):
    i = pl.program_id(0)
    cur = i % 2

    def issue(step, half):
        base = (tok0 + step * tm) * 2

        def body(r, carry):
            for k in range(2):
                src = slot_ref[base + 2 * r + k]
                pltpu.make_async_copy(y_ref.at[pl.ds(src, 1), :], ybuf.at[half, k, pl.ds(r, 1), :],
                                      sems.at[half]).start()
            return carry

        lax.fori_loop(0, tm, body, 0, unroll=8)

    @pl.when(i == 0)
    def _():
        issue(0, 0)

    @pl.when(i + 1 < pl.num_programs(0))
    def _():
        issue(i + 1, 1 - cur)

    pltpu.make_async_copy(ybuf.at[cur], ybuf.at[cur], sems.at[cur]).wait()
    r = r_ref[...]
    f = r[:, 2:3] * ybuf[cur, 0] + r[:, 3:4] * ybuf[cur, 1]
    o_ref[...] = _layer_norm(ALPHA * x_ref[...] + f, g_ref[...], b_ref[...])


def _combine_ln(x, y_sorted, slot, route, g, b, *, tm, tok0):
    rows = x.shape[0]
    row = lambda w: pl.BlockSpec((tm, w), lambda i, s: (i, 0))
    const = lambda a: pl.BlockSpec(a.shape, lambda i, s: (0,) * a.ndim)
    return pl.pallas_call(
        functools.partial(_combine_ln_kernel, tm=tm, tok0=tok0),
        grid_spec=pltpu.PrefetchScalarGridSpec(
            num_scalar_prefetch=1, grid=(rows // tm,),
            in_specs=[row(D_MODEL), row(LANES), pl.BlockSpec(memory_space=pl.ANY), const(g), const(b)],
            out_specs=row(D_MODEL),
            scratch_shapes=[pltpu.VMEM((2, 2, tm, D_MODEL), F32), pltpu.SemaphoreType.DMA((2,))]),
        out_shape=jax.ShapeDtypeStruct((rows, D_MODEL), F32),
        compiler_params=_cparams(("arbitrary",)),
        name="moe_combine_ln",
    )(slot, x, route, y_sorted, g, b)


def _router_params(router_w, router_b):
    w_pad = jnp.zeros((D_MODEL, LANES), F32).at[:, :N_EXPERTS].set(router_w)
    w_hi = w_pad.astype(BF16)
    w_lo = (w_pad - w_hi.astype(F32)).astype(BF16)
    return w_hi, w_lo, jnp.zeros((1, LANES), F32).at[0, :N_EXPERTS].set(router_b)


def _moe(xp, xs, route_p, router, w1, w3, w2, g, b, *, tm_e, sub, fc, tm_p):
    n_p, n_s = xp.shape[0], xs.shape[0]
    t = n_p + n_s
    route_s = _router(xs, *router, tm=n_s)
    top_i = jnp.concatenate([route_p[:, :2], route_s[:, :2]], axis=0).astype(jnp.int32).reshape(-1)
    onehot = (top_i[:, None] == jnp.arange(N_EXPERTS)[None, :]).astype(jnp.int32)
    csum = jnp.cumsum(onehot, axis=0)
    counts = csum[-1]
    rank = jnp.sum((csum - onehot) * onehot, axis=1)
    padded = ((counts + tm_e - 1) // tm_e) * tm_e
    ends = jnp.cumsum(padded)
    starts = ends - padded
    slot = (starts[top_i] + rank).astype(jnp.int32)
    n_tiles = (2 * t + N_EXPERTS * (tm_e - 1)) // tm_e + 1
    tile_start = jnp.arange(n_tiles, dtype=jnp.int32) * tm_e
    tile_expert = jnp.sum((tile_start[:, None] >= ends[None, :]).astype(jnp.int32), axis=1)
    tile_expert = jnp.minimum(tile_expert, N_EXPERTS - 1)
    tile_rows = jnp.clip((starts + counts)[tile_expert] - tile_start, 0, tm_e)
    tile_rows = jnp.where(tile_start < ends[-1], tile_rows, 0).astype(jnp.int32)
    last_e = tile_expert[jnp.maximum(ends[-1] // tm_e - 1, 0)]
    tile_expert = jnp.where(tile_rows > 0, tile_expert, last_e).astype(jnp.int32)
    x_sorted = jnp.zeros((n_tiles * tm_e, D_MODEL), F32)
    x_sorted = _dispatch(xp, slot, x_sorted, tm=tm_p, tok0=0)
    x_sorted = _dispatch(xs, slot, x_sorted, tm=n_s, tok0=n_p)
    y_sorted = _experts(x_sorted, tile_expert, tile_rows, w1, w3, w2, tm=tm_e, sub=sub, fc=fc)
    return (_combine_ln(xp, y_sorted, slot, route_p, g, b, tm=tm_p // 2, tok0=0),
            _combine_ln(xs, y_sorted, slot, route_s, g, b, tm=n_s, tok0=n_p))


def _pack_w_in(w_in_t):
    dt, pu = w_in_t[_C_P:_C_P + SSM_HEADS], w_in_t[_C_P + SSM_HEADS:]
    pad = jnp.zeros((DT_PAD - SSM_HEADS, D_MODEL), w_in_t.dtype)
    return w_in_t[:_C_P].astype(BF16), jnp.concatenate([pu, dt, pad], axis=0).astype(BF16)


def _block_diag(pool_w):
    out = jnp.zeros((POOL_WIDTH, POOL_WIDTH), F32)
    for g in range(len(POOL_WINDOWS)):
        out = out.at[g * POOL_GDIM:(g + 1) * POOL_GDIM, g * POOL_GDIM:(g + 1) * POOL_GDIM].set(pool_w[g])
    return out.astype(BF16)


def _pad_heads(v):
    return jnp.zeros((1, DT_PAD), F32).at[0, :SSM_HEADS].set(v)


def kernel(x_prompt, x_sample, cache_k, cache_v, cache_mem_k, cache_mem_v, state_ssm, state_conv, state_pool, page_table, mem_prompt, w_in, lam_q1, lam_k1, lam_q2, lam_k2, da_norm_w, conv_w, conv_b, dt_bias, a_log, d_skip, ssm_norm_w, pool_w, pool_scale, w_out, ln1_g, ln1_b, wq_mem, wk_mem, wv_mem, wo_mem, ln2_g, ln2_b, ffn_w1, ffn_w3, ffn_w2, moe_router, moe_router_b, moe_w1, moe_w3, moe_w2, ln3_g, ln3_b):
    bp, seq, _ = x_prompt.shape
    bs = x_sample.shape[0]
    n_pages = page_table.shape[1]
    past = n_pages * PAGE_SIZE
    n_mem = mem_prompt.shape[1]
    tp_rows = bp * seq
    TM = ROW_TILE

    xp = x_prompt.reshape(tp_rows, D_MODEL)
    xs = x_sample.reshape(bs, D_MODEL)
    tab_p = _rope_tables(jnp.arange(seq))
    tab_s = _rope_tables(jnp.full((1,), past))
    cache_k4 = jnp.transpose(cache_k, (0, 1, 3, 4, 5, 2)).reshape(DEPTH, -1, DA_WIDTH, PAGE_SIZE)
    cache_v4 = cache_v.reshape(DEPTH, -1, PAGE_SIZE * DA_HEADS, DA_DV)
    mem_k_all = _mem_rows(cache_mem_k, DEPTH * bs)
    mem_v_all = _mem_rows(cache_mem_v, DEPTH * bs)
    state_all = state_ssm.reshape(DEPTH * bs, SSM_WIDTH, SSM_STATE)
    conv_hist = jnp.transpose(state_conv, (0, 2, 1, 3))
    pool_hist = jnp.transpose(state_pool, (0, 2, 1, 3))
    w_in_t = jnp.transpose(w_in, (2, 0, 1))
    mem_flat = mem_prompt.reshape(bp * n_mem, D_MODEL)
    row = lambda v: v.reshape(1, -1)

    outs = {k: [] for k in ("kp", "vp", "ks", "vs", "mk", "mv", "hp", "hs", "cp", "cs", "pp", "ps")}
    for l in range(DEPTH):
        lam_init = 0.8 - 0.6 * math.exp(-0.3 * l)
        w_cat = _pack_w_in(w_in_t[:, l, :])
        lam_vecs = (row(lam_q1[l]), row(lam_k1[l]), row(lam_q2[l]), row(lam_k2[l]))
        nw = row(da_norm_w[l])
        dtb, alog = _pad_heads(dt_bias[l]), _pad_heads(a_log[l])
        dsk = row(jnp.repeat(d_skip[l], SSM_HEADDIM))
        snw = row(ssm_norm_w[l])
        pwbd = _block_diag(pool_w[l])
        psc = row(pool_scale[l])
        wo_b = w_out[l].astype(BF16)
        wo_parts = (wo_b[:DA_WIDTH], wo_b[DA_WIDTH:DA_WIDTH + SSM_WIDTH], wo_b[DA_WIDTH + SSM_WIDTH:])
        g1, b1, g2, b2, g3, b3 = (row(ln1_g[l]), row(ln1_b[l]), row(ln2_g[l]), row(ln2_b[l]),
                                  row(ln3_g[l]), row(ln3_b[l]))

        qp, kb, vb, kpg, vrow, zp, xbcp, pup, dtp = _proj_in(xp, w_cat, tab_p, prompt=True, tm=TM, seq=seq,
                                                             rows=tp_rows)
        oa_p = _flash_attn(qp, kb, vb, lam_vecs, nw.reshape(DA_DV, 1), batch=bp, seq=seq, tq=ATTN_TILE, tk=ATTN_TILE,
                           lam_init=lam_init)
        y_p, h_p = _ssd_prompt(xbcp, zp, dtp, conv_w[l], row(conv_b[l]), dtb, alog, dsk, snw, batch=bp, seq=seq)
        yp_p = _pool_prompt(pup, pwbd, psc, batch=bp, seq=seq, tp=POOL_TILE)

        q_s, k_s, v_s, z_s, xbc_s, pu_s, dt_s = _proj_in(xs, w_cat, tab_s, prompt=False, tm=bs, seq=1, rows=bs)
        oa_s = _paged_attn(q_s, k_s, v_s, cache_k4, cache_v4, page_table, lam_vecs, nw, layer=l, lam_init=lam_init)
        xact, dt_a, dec, yp_s, conv_new, pool_new = _sample_prep(
            xbc_s, conv_hist[l], conv_w[l], row(conv_b[l]), dt_s, dtb, alog, pu_s, pool_hist[l], pwbd, psc, pos0=past)
        y_s, h_s = _ssd_step(state_all, l, xact, z_s, dt_a, dec, dsk, snw)
        xs = _mm_res_ln((oa_s, y_s, yp_s), wo_parts, xs, g1, b1, tm=bs)

        wq_b, wk_b, wv_b, wom_b = (wq_mem[l].astype(BF16), wk_mem[l].astype(BF16), wv_mem[l].astype(BF16),
                                   wo_mem[l].astype(BF16))
        mk, mkb = _matmul(mem_flat, wk_b, tm=bp * n_mem // 2, dtypes=(F32, BF16))
        mv, mvb = _matmul(mem_flat, wv_b, tm=bp * n_mem // 2, dtypes=(F32, BF16))
        router = _router_params(moe_router[l // 2], moe_router_b[l // 2]) if l % 2 == 1 else None
        res = _mix_mem_prompt((oa_p, y_p, yp_p), wo_parts, xp, g1, b1, mkb, mvb, wq_b, wom_b, g2, b2, router,
                              batch=bp, seq=seq, tm=TM)
        xp, route_p = res if router else (res[0], None)
        (qm_s,) = _matmul(xs, wq_b, tm=bs, scale=MEM_HD ** -0.5)
        om_s = _mem_attn_sample(qm_s, mem_k_all, mem_v_all, layer=l, group=MEM_SAMPLE_GROUP)
        xs = _mm_res_ln((om_s,), (wom_b,), xs, g2, b2, tm=bs)

        j = l // 2
        if l % 2 == 0:
            fw1, fw3, fw2 = ffn_w1[j].astype(BF16), ffn_w3[j].astype(BF16), ffn_w2[j].astype(BF16)
            xp = _ffn(xp, fw1, fw3, fw2, g3, b3, tm=TM, fc=FFN_CHUNK)
            xs = _ffn(xs, fw1, fw3, fw2, g3, b3, tm=bs, fc=FFN_CHUNK)
        else:
            xp, xs = _moe(xp, xs, route_p, router, moe_w1[j], moe_w3[j], moe_w2[j], g3, b3,
                          tm_e=EXPERT_TILE, sub=EXPERT_SUB, fc=EXPERT_CHUNK, tm_p=TM)

        outs["kp"].append(jnp.transpose(kpg.reshape(bp, seq // PAGE_SIZE, DA_HEADS, 2, DA_DK, PAGE_SIZE),
                                        (0, 1, 5, 2, 3, 4)))
        outs["vp"].append(vrow.reshape(bp, seq // PAGE_SIZE, PAGE_SIZE, DA_HEADS, DA_DV))
        outs["ks"].append(k_s.reshape(bs, 1, DA_HEADS, 2, DA_DK))
        outs["vs"].append(v_s.reshape(bs, 1, DA_HEADS, DA_DV))
        outs["mk"].append(mk.reshape(bp, n_mem, MEM_HEADS, MEM_HD))
        outs["mv"].append(mv.reshape(bp, n_mem, MEM_HEADS, MEM_HD))
        outs["hp"].append(h_p)
        outs["hs"].append(h_s)
        outs["cp"].append(xbcp.reshape(bp, seq, CONV_CH)[:, seq - (SSM_CONV - 1):])
        outs["cs"].append(jnp.transpose(conv_new, (1, 0, 2)))
        outs["pp"].append(pup.reshape(bp, seq, POOL_WIDTH)[:, seq - POOL_BUF:])
        outs["ps"].append(jnp.transpose(pool_new, (1, 0, 2)))

    st = lambda k: jnp.stack(outs[k])
    return (xp.reshape(bp, seq, D_MODEL), xs.reshape(bs, 1, D_MODEL), st("kp"), st("vp"), st("ks"), st("vs"),
            st("mk"), st("mv"), st("hp"), st("hs"), st("cp"), st("cs"), st("pp"), st("ps"))
```

```python
import functools
import math

import jax
import jax.numpy as jnp
from jax import lax
from jax.experimental import pallas as pl
from jax.experimental.pallas import tpu as pltpu

F32 = jnp.float32
BF16 = jnp.bfloat16

D_MODEL = 1024
DEPTH = 2
PAGE_SIZE = 128
DA_HEADS = 4
DA_WIDTH = 512
DA_DV = 128
DA_DK = 64
ROT_DIM = 16
ROPE_THETA = 500000.0
SSM_WIDTH = 256
SSM_HEADS = 4
SSM_HEADDIM = 64
SSM_STATE = 128
SSM_CONV = 4
SSM_CHUNK = 128
CONV_CH = 768
POOL_WIDTH = 256
POOL_WINDOWS = (2, 4, 8, 16)
POOL_GDIM = 64
POOL_BUF = 15
MEM_HEADS = 4
MEM_HD = 256
N_EXPERTS = 8
ALPHA = (2.0 * DEPTH) ** 0.25
LN_EPS = 1e-5
RMS_EPS = 1e-6

LANES = 128
DT_PAD = LANES
_C_Q, _C_K, _C_V, _C_Z, _C_X, _C_P, _C_DT, _C_END = 0, 512, 1024, 1536, 1792, 2560, 2816, 2944

ROW_TILE = 1024
ATTN_TILE = 512
POOL_TILE = 512
FFN_CHUNK = 256
EXPERT_TILE = 1024
EXPERT_SUB = 256
EXPERT_CHUNK = 512
MEM_SAMPLE_GROUP = 4

NEG_INF = float("-inf")
Q_SCALE = DA_DK ** -0.5 * math.log2(math.e)


def _cparams(sem, vmem_mb=48):
    return pltpu.CompilerParams(dimension_semantics=sem, vmem_limit_bytes=vmem_mb * 1024 * 1024)


def _dot(a, b):
    return jnp.dot(a.astype(BF16), b.astype(BF16), preferred_element_type=F32)


def _dot_nt(a, b):
    return lax.dot_general(a.astype(BF16), b.astype(BF16), (((1,), (1,)), ((), ())),
                           preferred_element_type=F32)


def _layer_norm(x, g, b):
    mu = jnp.mean(x, axis=-1, keepdims=True)
    xc = x - mu
    var = jnp.mean(xc * xc, axis=-1, keepdims=True)
    return xc * lax.rsqrt(var + LN_EPS) * g + b


def _silu(x):
    return x * (1.0 / (1.0 + jnp.exp(-x)))


def _softplus(x):
    return jnp.maximum(x, 0.0) + jnp.log(1.0 + jnp.exp(-jnp.abs(x)))


def _const_spec(shape):
    nd = len(shape)
    return pl.BlockSpec(shape, lambda *_: (0,) * nd)


def _rope_tables(pos):
    half = ROT_DIM // 2
    d = jnp.arange(LANES) % DA_DK
    inv = ROPE_THETA ** (-(d % half).astype(F32) * 2.0 / ROT_DIM)
    ang = pos.astype(F32)[:, None] * inv[None, :]
    cos, sin = jnp.cos(ang), jnp.sin(ang)
    cc = jnp.where(d < ROT_DIM, cos, 1.0)
    s1 = jnp.where(d < half, -sin, 0.0)
    s2 = jnp.where((d >= half) & (d < ROT_DIM), sin, 0.0)
    return cc, s1, s2


def _proj_in_kernel(x_ref, w_ref, wt_ref, c_ref, s1_ref, s2_ref, *out_refs, prompt):
    xb = x_ref[...].astype(BF16)
    cc, s1, s2 = c_ref[...], s1_ref[...], s2_ref[...]

    def rope(t):
        outs = []
        for h in range(DA_HEADS):
            th = t[:, h * LANES:(h + 1) * LANES]
            outs.append(th * cc + pltpu.roll(th, LANES - ROT_DIM // 2, 1) * s1
                        + pltpu.roll(th, ROT_DIM // 2, 1) * s2)
        return jnp.concatenate(outs, axis=1)

    def seg(a, b):
        ref, off = (w_ref, 0) if b <= _C_P else (wt_ref, _C_P)
        return lax.dot_general(xb, ref[a - off:b - off, :], (((1,), (1,)), ((), ())), preferred_element_type=F32)

    q = rope(seg(_C_Q, _C_K)) * Q_SCALE
    k = rope(seg(_C_K, _C_V))
    v = seg(_C_V, _C_Z)
    if prompt:
        q_ref, kb_ref, vb_ref, kpg_ref, vrow_ref, z_ref, xbc_ref, pu_ref, dt_ref = out_refs
        kb_ref[...] = k.astype(BF16)
        vb_ref[...] = v.astype(BF16)
        for pg in range(kpg_ref.shape[0]):
            kpg_ref[pg] = k[pg * PAGE_SIZE:(pg + 1) * PAGE_SIZE, :].T
        for h in range(DA_HEADS):
            vrow_ref[pl.ds(h, v.shape[0], stride=DA_HEADS), :] = v[:, h * DA_DV:(h + 1) * DA_DV]
    else:
        q_ref, k_ref, v_ref, z_ref, xbc_ref, pu_ref, dt_ref = out_refs
        k_ref[...] = k
        v_ref[...] = v
    q_ref[...] = q
    z_ref[...] = seg(_C_Z, _C_X)
    xbc_ref[...] = seg(_C_X, _C_P)
    pu_ref[...] = seg(_C_P, _C_DT)
    dt_ref[...] = seg(_C_DT, _C_END)


def _proj_in(x, w_cat, tables, *, prompt, tm, seq, rows, row0=0):
    m = rows
    grid = (m // tm,)
    xrow = pl.BlockSpec((tm, D_MODEL), lambda i: (row0 + i, 0))

    def rows_out(w, dt):
        return pl.BlockSpec((tm, w), lambda i: (i, 0)), jax.ShapeDtypeStruct((m, w), dt)

    if prompt:
        npos = seq // tm
        ppt = tm // PAGE_SIZE
        tab = pl.BlockSpec((tm, LANES), lambda i: (i % npos, 0))
        outs = [rows_out(DA_WIDTH, F32), rows_out(DA_WIDTH, BF16), rows_out(DA_WIDTH, BF16),
                (pl.BlockSpec((ppt, DA_WIDTH, PAGE_SIZE), lambda i: (i, 0, 0)),
                 jax.ShapeDtypeStruct((m // PAGE_SIZE, DA_WIDTH, PAGE_SIZE), F32)),
                (pl.BlockSpec((tm * DA_HEADS, DA_DV), lambda i: (i, 0)),
                 jax.ShapeDtypeStruct((m * DA_HEADS, DA_DV), F32))]
    else:
        tab = _const_spec((1, LANES))
        outs = [rows_out(DA_WIDTH, F32), rows_out(DA_WIDTH, F32), rows_out(DA_WIDTH, F32)]
    outs += [rows_out(SSM_WIDTH, F32), rows_out(CONV_CH, F32), rows_out(POOL_WIDTH, F32), rows_out(DT_PAD, F32)]
    return pl.pallas_call(
        functools.partial(_proj_in_kernel, prompt=prompt),
        grid=grid,
        in_specs=[xrow, _const_spec(w_cat[0].shape), _const_spec(w_cat[1].shape), tab, tab, tab],
        out_specs=[o[0] for o in outs],
        out_shape=[o[1] for o in outs],
        compiler_params=_cparams(("parallel",)),
        name="proj_in",
    )(x, *w_cat, *tables)


def _lam_value(lq1, lk1, lq2, lk2, lam_init):
    return (jnp.exp(jnp.sum(lq1 * lk1, axis=1, keepdims=True))
            - jnp.exp(jnp.sum(lq2 * lk2, axis=1, keepdims=True)) + lam_init)


def _flash_kernel(q_ref, k_ref, v_ref, lq1_ref, lk1_ref, lq2_ref, lk2_ref, nw_ref, o_ref,
                  vt_sc, s_sc, p_sc, m_sc, acc_sc, *, tq, tk, lam_init):
    i = pl.program_id(2)
    n_kt, aug, _ = vt_sc.shape

    @pl.when(i == 0)
    def _():
        ones = jnp.ones((aug - DA_DV, tk), BF16)
        for j in range(n_kt):
            vt = v_ref[j * tk:(j + 1) * tk, :].astype(F32).T.astype(BF16)
            vt_sc[j] = jnp.concatenate([vt, ones], axis=0)

    qt = q_ref[...].T
    drow = lax.broadcasted_iota(jnp.int32, qt.shape, 0)
    qts = (jnp.where(drow < DA_DK, qt, 0.0).astype(BF16), jnp.where(drow >= DA_DK, qt, 0.0).astype(BF16))

    def stage(j, cur, mask=None, last=False):
        nxt = 1 - cur
        if not last:
            kt = k_ref[pl.ds(pl.multiple_of((j + 1) * tk, tk), tk), :]
            for c in range(2):
                s_sc[nxt, c] = jnp.dot(kt, qts[c], preferred_element_type=F32)
        vta = vt_sc[jnp.maximum(j - 1, 0)]
        for c in range(2):
            pv = jnp.dot(vta, p_sc[cur, c], preferred_element_type=F32)
            s = s_sc[cur, c]
            if mask is not None:
                s = jnp.where(mask, s, NEG_INF)
            m = m_sc[c]
            m_new = jnp.maximum(m, jnp.max(s, axis=0, keepdims=True))
            p_sc[nxt, c] = jnp.exp2(s - m_new).astype(BF16)
            acc_sc[c] = jnp.exp2(m - m_new) * (acc_sc[c] + pv)
            m_sc[c] = m_new

    kt0 = k_ref[0:tk, :]
    for c in range(2):
        s_sc[0, c] = jnp.dot(kt0, qts[c], preferred_element_type=F32)
        p_sc[0, c] = jnp.zeros((tk, tq), BF16)
        m_sc[c] = jnp.full((1, tq), NEG_INF, F32)
        acc_sc[c] = jnp.zeros((aug, tq), F32)

    def pair(t, carry):
        stage(2 * t, 0)
        stage(2 * t + 1, 1)
        return carry

    lax.fori_loop(0, i // 2, pair, 0)
    odd = i % 2 == 1

    @pl.when(odd)
    def _():
        stage(i - 1, 0)

    krow = lax.broadcasted_iota(jnp.int32, (tk, tq), 0)
    qcol = lax.broadcasted_iota(jnp.int32, (tk, tq), 1)

    def finish(cur):
        stage(i, cur, mask=krow <= qcol, last=True)
        vta = vt_sc[i]
        a0, a1 = (acc_sc[c] + jnp.dot(vta, p_sc[1 - cur, c], preferred_element_type=F32) for c in range(2))
        lam = _lam_value(lq1_ref[...], lk1_ref[...], lq2_ref[...], lk2_ref[...], lam_init)
        o = a0[:DA_DV] * (1.0 / a0[DA_DV:DA_DV + 1]) - lam * (a1[:DA_DV] * (1.0 / a1[DA_DV:DA_DV + 1]))
        ms = jnp.mean(o * o, axis=0, keepdims=True)
        o = o * lax.rsqrt(ms + RMS_EPS) * nw_ref[...] * (1.0 - lam_init)
        o_ref[...] = o.T

    pl.when(odd)(lambda: finish(1))
    pl.when(jnp.logical_not(odd))(lambda: finish(0))


def _flash_attn(q, kb, v, lam_vecs, norm_w_col, *, batch, seq, tq, tk, lam_init):
    assert tq == tk
    q3 = q.reshape(batch, seq, DA_WIDTH)
    k3 = kb.reshape(batch, seq, DA_WIDTH)
    v3 = v.reshape(batch, seq, DA_WIDTH)
    qspec = pl.BlockSpec((None, tq, DA_DV), lambda b, h, i: (b, i, h))
    kvspec = pl.BlockSpec((None, seq, DA_DV), lambda b, h, i: (b, 0, h))
    vec = _const_spec((1, DA_DK))
    aug = DA_DV + 16
    out = pl.pallas_call(
        functools.partial(_flash_kernel, tq=tq, tk=tk, lam_init=lam_init),
        grid=(batch, DA_HEADS, seq // tq),
        in_specs=[qspec, kvspec, kvspec, vec, vec, vec, vec, _const_spec((DA_DV, 1))],
        out_specs=qspec,
        out_shape=jax.ShapeDtypeStruct((batch, seq, DA_WIDTH), F32),
        scratch_shapes=[pltpu.VMEM((seq // tk, aug, tk), BF16), pltpu.VMEM((2, 2, tk, tq), F32),
                        pltpu.VMEM((2, 2, tk, tq), BF16), pltpu.VMEM((2, 1, tq), F32),
                        pltpu.VMEM((2, aug, tq), F32)],
        compiler_params=_cparams(("parallel", "parallel", "arbitrary")),
        name="flash_diff_attn",
    )(q3, k3, v3, *lam_vecs, norm_w_col)
    return out.reshape(batch * seq, DA_WIDTH)


def _paged_kernel(pt_ref, q_ref, ks_ref, vs_ref, lq1_ref, lk1_ref, lq2_ref, lk2_ref, nw_ref, *rest,
                  n_pages, lam_init):
    k_refs = rest[:n_pages]
    v_refs = rest[n_pages:2 * n_pages]
    o_ref = rest[2 * n_pages]
    q = q_ref[...]
    nhc = 2 * DA_HEADS
    jj = lax.broadcasted_iota(jnp.int32, (2 * nhc, DA_WIDTH), 0)
    ll = lax.broadcasted_iota(jnp.int32, (2 * nhc, DA_WIDTH), 1)
    qblk = jnp.where((jj < nhc) & ((ll >> 6) == (jj & (DA_HEADS - 1)) * 2 + (jj >> 2)), q, 0.0)
    qb = qblk.astype(BF16)

    s_all = jnp.concatenate(
        [jnp.dot(qb, k_refs[j][...].astype(BF16), preferred_element_type=F32)[:nhc] for j in range(n_pages)],
        axis=1)
    s_self = jnp.sum(qblk[:nhc] * ks_ref[...], axis=1, keepdims=True)
    m = jnp.maximum(jnp.max(s_all, axis=1, keepdims=True), s_self)
    p = jnp.exp2(s_all - m)
    p_self = jnp.exp2(s_self - m)
    inv_l = 1.0 / (jnp.sum(p, axis=1, keepdims=True) + p_self)
    lam = _lam_value(lq1_ref[...], lk1_ref[...], lq2_ref[...], lk2_ref[...], lam_init)
    pn = p * inv_l
    pn_self = p_self * inv_l
    w = pn - lam * pltpu.roll(pn, DA_HEADS, 0)
    w_self = pn_self - lam * pltpu.roll(pn_self, DA_HEADS, 0)
    w16 = jnp.concatenate([w, jnp.zeros_like(w)], axis=0).astype(BF16)
    accs = [jnp.zeros((2 * nhc, DA_DV), F32) for _ in range(DA_HEADS)]
    for j in range(n_pages):
        wj = w16[:, j * PAGE_SIZE:(j + 1) * PAGE_SIZE]
        for h in range(DA_HEADS):
            vh = v_refs[j][pl.ds(h, PAGE_SIZE, stride=DA_HEADS), :]
            accs[h] = accs[h] + jnp.dot(wj, vh.astype(BF16), preferred_element_type=F32)
    outs = []
    for h in range(DA_HEADS):
        o = accs[h][h:h + 1, :] + w_self[h:h + 1, :] * vs_ref[:, h * DA_DV:(h + 1) * DA_DV]
        ms = jnp.mean(o * o, axis=1, keepdims=True)
        outs.append(o * lax.rsqrt(ms + RMS_EPS) * nw_ref[...] * (1.0 - lam_init))
    o_ref[...] = jnp.concatenate(outs, axis=1)


def _paged_attn(q, k_self, v_self, cache_k4, cache_v4, page_table, lam_vecs, norm_w4, *, layer, lam_init):
    bs, n_pages = page_table.shape
    pt = page_table.reshape(-1)
    one = pl.BlockSpec((None, 1, DA_WIDTH), lambda b, pt: (b, 0, 0))
    vec = pl.BlockSpec((1, DA_DK), lambda b, pt: (0, 0))

    def page_spec(j):
        return pl.BlockSpec((None, None, DA_WIDTH, PAGE_SIZE), lambda b, pt: (layer, pt[b * n_pages + j], 0, 0))

    specs = [one, one, one, vec, vec, vec, vec, pl.BlockSpec((1, DA_DV), lambda b, pt: (0, 0))]
    specs += [page_spec(j) for j in range(n_pages)] * 2
    out = pl.pallas_call(
        functools.partial(_paged_kernel, n_pages=n_pages, lam_init=lam_init),
        grid_spec=pltpu.PrefetchScalarGridSpec(
            num_scalar_prefetch=1, grid=(bs,), in_specs=specs, out_specs=one),
        out_shape=jax.ShapeDtypeStruct((bs, 1, DA_WIDTH), F32),
        compiler_params=_cparams(("arbitrary",)),
        name="paged_diff_attn",
    )(pt, q.reshape(bs, 1, DA_WIDTH), k_self.reshape(bs, 1, DA_WIDTH), v_self.reshape(bs, 1, DA_WIDTH),
      *lam_vecs, norm_w4, *([cache_k4] * n_pages), *([cache_v4] * n_pages))
    return out.reshape(bs, DA_WIDTH)


def _head_expand(cols, width, per):
    rows = cols.shape[0]
    lane = lax.broadcasted_iota(jnp.int32, (rows, width), 1)
    out = jnp.broadcast_to(cols[:, 0:1], (rows, width))
    for h in range(1, width // per):
        out = jnp.where(lane >= h * per, cols[:, h:h + 1], out)
    return out


def _ssd_kernel(xbc_ref, z_ref, dt_ref, cw_ref, cb_ref, dtb_ref, alog_ref, dsk_ref, nw_ref,
                y_ref, st_ref, tail_sc, state_sc):
    c = pl.program_id(0)

    @pl.when(c == 0)
    def _():
        tail_sc[...] = jnp.zeros_like(tail_sc)
        state_sc[...] = jnp.zeros_like(state_sc)

    for b in range(xbc_ref.shape[0]):
        state = _ssd_chunk(xbc_ref.at[b], z_ref.at[b], dt_ref.at[b], cw_ref, cb_ref, dtb_ref, alog_ref, dsk_ref,
                           nw_ref, y_ref.at[b], tail_sc.at[b], state_sc.at[b])

        @pl.when(c == pl.num_programs(0) - 1)
        def _():
            st_ref[b] = state


def _ssd_chunk(xbc_ref, z_ref, dt_ref, cw_ref, cb_ref, dtb_ref, alog_ref, dsk_ref, nw_ref, y_ref, tail_sc,
               state_sc):
    q = SSM_CHUNK
    u = xbc_ref[...]
    full = jnp.concatenate([tail_sc[...], u], axis=0)
    tail_sc[...] = u[q - 8:, :]
    conv = full * cw_ref[SSM_CONV - 1:SSM_CONV, :]
    for j in range(1, SSM_CONV):
        conv = conv + pltpu.roll(full, j, 0) * cw_ref[SSM_CONV - 1 - j:SSM_CONV - j, :]
    xbc = _silu(conv[8:, :] + cb_ref[...])
    xs = xbc[:, :SSM_WIDTH]
    bm = (xbc[:, SSM_WIDTH:SSM_WIDTH + SSM_STATE], xbc[:, SSM_WIDTH + SSM_STATE:SSM_WIDTH + 2 * SSM_STATE])
    cm = (xbc[:, SSM_WIDTH + 2 * SSM_STATE:SSM_WIDTH + 3 * SSM_STATE], xbc[:, SSM_WIDTH + 3 * SSM_STATE:])

    dt = _softplus(dt_ref[...] + dtb_ref[...])
    dta = dt * (-jnp.exp(alog_ref[...]))
    ti = lax.broadcasted_iota(jnp.int32, (q, q), 0)
    si = lax.broadcasted_iota(jnp.int32, (q, q), 1)
    causal = si <= ti
    tri = jnp.where(causal, 1.0, 0.0)
    a_cs = jnp.dot(tri, dta, precision=lax.Precision.HIGHEST, preferred_element_type=F32)
    a_cs_t = a_cs.T
    last = a_cs[q - 1:q, :]

    dt_x = _head_expand(dt, SSM_WIDTH, SSM_HEADDIM)
    xdt = xs * dt_x
    xdt_end = xdt * _head_expand(jnp.exp(last - a_cs), SSM_WIDTH, SSM_HEADDIM)
    lane = lax.broadcasted_iota(jnp.int32, (q, SSM_WIDTH), 1)

    cb = [_dot_nt(cm[g], bm[g]) for g in range(2)]
    y_diag = jnp.zeros((q, SSM_WIDTH), F32)
    for h in range(SSM_HEADS):
        seg = a_cs[:, h:h + 1] - a_cs_t[h:h + 1, :]
        decay = jnp.exp(jnp.where(causal, seg, NEG_INF))
        xh = jnp.where((lane >= h * SSM_HEADDIM) & (lane < (h + 1) * SSM_HEADDIM), xdt, 0.0)
        y_diag = y_diag + _dot(cb[h // 2] * decay, xh)

    state = state_sc[...]
    sb = state.astype(BF16)
    y_off = jnp.where(lane < 2 * SSM_HEADDIM, _dot_nt(cm[0], sb), _dot_nt(cm[1], sb))
    y_off = y_off * _head_expand(jnp.exp(a_cs), SSM_WIDTH, SSM_HEADDIM)

    xt = xdt_end.T
    rowi = lax.broadcasted_iota(jnp.int32, (SSM_WIDTH, SSM_STATE), 0)
    new = jnp.where(rowi < 2 * SSM_HEADDIM, _dot(xt, bm[0]), _dot(xt, bm[1]))
    e_last = jnp.exp(last)
    dec_rows = jnp.broadcast_to(e_last[:, 0:1], (SSM_WIDTH, SSM_STATE))
    for h in range(1, SSM_HEADS):
        dec_rows = jnp.where(rowi >= h * SSM_HEADDIM, e_last[:, h:h + 1], dec_rows)
    state = state * dec_rows + new
    state_sc[...] = state

    y = y_diag + y_off + dsk_ref[...] * xs
    y = y * _silu(z_ref[...])
    ms = jnp.mean(y * y, axis=1, keepdims=True)
    y_ref[...] = y * lax.rsqrt(ms + RMS_EPS) * nw_ref[...]
    return state


def _ssd_prompt(xbc, z, dt_raw, conv_w, conv_b, dt_bias, a_log, d_skip_x, norm_w, *, batch, seq):
    nc = seq // SSM_CHUNK
    blk = lambda w: pl.BlockSpec((batch, SSM_CHUNK, w), lambda c: (0, c, 0))
    y, st = pl.pallas_call(
        _ssd_kernel,
        grid=(nc,),
        in_specs=[blk(CONV_CH), blk(SSM_WIDTH), blk(DT_PAD),
                  _const_spec((SSM_CONV, CONV_CH)), _const_spec((1, CONV_CH)),
                  _const_spec((1, DT_PAD)), _const_spec((1, DT_PAD)),
                  _const_spec((1, SSM_WIDTH)), _const_spec((1, SSM_WIDTH))],
        out_specs=[blk(SSM_WIDTH), pl.BlockSpec((batch, SSM_WIDTH, SSM_STATE), lambda c: (0, 0, 0))],
        out_shape=[jax.ShapeDtypeStruct((batch, seq, SSM_WIDTH), F32),
                   jax.ShapeDtypeStruct((batch, SSM_WIDTH, SSM_STATE), F32)],
        scratch_shapes=[pltpu.VMEM((batch, 8, CONV_CH), F32), pltpu.VMEM((batch, SSM_WIDTH, SSM_STATE), F32)],
        compiler_params=_cparams(("arbitrary",)),
        name="ssd_prompt",
    )(xbc.reshape(batch, seq, CONV_CH), z.reshape(batch, seq, SSM_WIDTH), dt_raw.reshape(batch, seq, DT_PAD),
      conv_w, conv_b, dt_bias, a_log, d_skip_x, norm_w)
    return y.reshape(batch * seq, SSM_WIDTH), st.reshape(batch, SSM_HEADS, SSM_HEADDIM, SSM_STATE)


def _pool_select(sums, inv_cnt, lane):
    out = sums[POOL_WINDOWS[0]] * inv_cnt[POOL_WINDOWS[0]]
    for g in range(1, len(POOL_WINDOWS)):
        w = POOL_WINDOWS[g]
        out = jnp.where(lane >= g * POOL_GDIM, sums[w] * inv_cnt[w], out)
    return out


def _pool_kernel(u_ref, w_ref, sc_ref, y_ref, tail_sc, *, tp):
    c = pl.program_id(1)
    halo = 16

    @pl.when(c == 0)
    def _():
        tail_sc[...] = jnp.zeros_like(tail_sc)

    u = u_ref[...]
    full = jnp.concatenate([tail_sc[...], u], axis=0)
    tail_sc[...] = u[tp - halo:, :]
    sums = {}
    run = full
    w = 1
    while w < max(POOL_WINDOWS):
        run = run + pltpu.roll(run, w, 0)
        w *= 2
        sums[w] = run[halo:, :]
    pos = c * tp + lax.broadcasted_iota(jnp.int32, (tp, 1), 0)
    inv_cnt = {w: 1.0 / jnp.minimum(w, pos + 1).astype(F32) for w in POOL_WINDOWS}
    lane = lax.broadcasted_iota(jnp.int32, (tp, POOL_WIDTH), 1)
    d = _pool_select(sums, inv_cnt, lane) - u
    y_ref[...] = _dot(d, w_ref[...]) * sc_ref[...]


def _pool_prompt(pu, pool_wbd, pool_scale, *, batch, seq, tp):
    blk = pl.BlockSpec((None, tp, POOL_WIDTH), lambda b, c: (b, c, 0))
    y = pl.pallas_call(
        functools.partial(_pool_kernel, tp=tp),
        grid=(batch, seq // tp),
        in_specs=[blk, _const_spec((POOL_WIDTH, POOL_WIDTH)), _const_spec((1, POOL_WIDTH))],
        out_specs=blk,
        out_shape=jax.ShapeDtypeStruct((batch, seq, POOL_WIDTH), F32),
        scratch_shapes=[pltpu.VMEM((16, POOL_WIDTH), F32)],
        compiler_params=_cparams(("parallel", "arbitrary")),
        name="pool_prompt",
    )(pu.reshape(batch, seq, POOL_WIDTH), pool_wbd, pool_scale)
    return y.reshape(batch * seq, POOL_WIDTH)


def _sample_prep_kernel(xbc_ref, cbuf_ref, cw_ref, cb_ref, dt_ref, dtb_ref, alog_ref, pu_ref, pbuf_ref,
                        pw_ref, psc_ref, xact_ref, dto_ref, dec_ref, yp_ref, cnew_ref, pnew_ref, *, pos0):
    u = xbc_ref[...]
    for j in range(SSM_CONV - 2):
        cnew_ref[j] = cbuf_ref[j + 1]
    cnew_ref[SSM_CONV - 2] = u
    for j in range(POOL_BUF - 1):
        pnew_ref[j] = pbuf_ref[j + 1]
    pnew_ref[POOL_BUF - 1] = pu_ref[...]
    conv = u * cw_ref[SSM_CONV - 1:SSM_CONV, :] + cb_ref[...]
    for j in range(SSM_CONV - 1):
        conv = conv + cbuf_ref[j] * cw_ref[j:j + 1, :]
    xact_ref[...] = _silu(conv)
    dt = _softplus(dt_ref[...] + dtb_ref[...])
    dto_ref[...] = dt
    dec_ref[...] = jnp.exp(dt * (-jnp.exp(alog_ref[...])))
    pu = pu_ref[...]
    run = pu
    sums = {}
    for j in range(1, max(POOL_WINDOWS)):
        run = run + pbuf_ref[POOL_BUF - j]
        if j + 1 in POOL_WINDOWS:
            sums[j + 1] = run
    inv_cnt = {w: 1.0 / float(min(w, pos0 + 1)) for w in POOL_WINDOWS}
    lane = lax.broadcasted_iota(jnp.int32, pu.shape, 1)
    d = _pool_select(sums, inv_cnt, lane) - pu
    yp_ref[...] = _dot(d, pw_ref[...]) * psc_ref[...]


def _sample_prep(xbc, conv_buf, conv_w, conv_b, dt_raw, dt_bias, a_log, pu, pool_buf, pool_wbd, pool_scale, *, pos0):
    bs = xbc.shape[0]
    args = (xbc, conv_buf, conv_w, conv_b, dt_raw, dt_bias, a_log, pu, pool_buf, pool_wbd, pool_scale)
    shapes = [(bs, CONV_CH), (bs, DT_PAD), (bs, DT_PAD), (bs, POOL_WIDTH), conv_buf.shape, pool_buf.shape]
    return pl.pallas_call(
        functools.partial(_sample_prep_kernel, pos0=pos0),
        grid=(1,),
        in_specs=[_const_spec(a.shape) for a in args],
        out_specs=[_const_spec(s) for s in shapes],
        out_shape=[jax.ShapeDtypeStruct(s, F32) for s in shapes],
        compiler_params=_cparams(("arbitrary",)),
        name="sample_prep",
    )(*args)


def _ssd_step_kernel(s_ref, x_ref, z_ref, dt_ref, dec_ref, b_ref, c_ref, dsk_ref, nw_ref, y_ref, so_ref, *, group):
    rowi = lax.broadcasted_iota(jnp.int32, (SSM_WIDTH, SSM_STATE), 0)
    for i in range(group):
        brow, crow = b_ref[i], c_ref[i]
        bsel = jnp.where(rowi < 2 * SSM_HEADDIM, brow[:, :SSM_STATE], brow[:, SSM_STATE:])
        csel = jnp.where(rowi < 2 * SSM_HEADDIM, crow[:, :SSM_STATE], crow[:, SSM_STATE:])
        x = x_ref[i]
        sn = s_ref[i] * dec_ref[i] + (x * dt_ref[i]) * bsel
        so_ref[i] = sn
        y = jnp.sum(sn * csel, axis=1, keepdims=True) + dsk_ref[...] * x
        y = y * _silu(z_ref[i])
        ms = jnp.sum(y * y, axis=0, keepdims=True) * (1.0 / SSM_WIDTH)
        y_ref[i] = y * lax.rsqrt(ms + RMS_EPS) * nw_ref[...]


def _ssd_step(state_all, layer, xact, z, dt, dec, d_skip_x, norm_w, *, group=8):
    bs = xact.shape[0]
    nb = bs // group
    rep = lambda t: jnp.repeat(t[:, :SSM_HEADS], SSM_HEADDIM, axis=1).reshape(bs, SSM_WIDTH, 1)
    colb = pl.BlockSpec((group, SSM_WIDTH, 1), lambda b: (b, 0, 0))
    rowb = pl.BlockSpec((group, 1, 2 * SSM_STATE), lambda b: (b, 0, 0))
    st_in = pl.BlockSpec((group, SSM_WIDTH, SSM_STATE), lambda b: (layer * nb + b, 0, 0))
    st_out = pl.BlockSpec((group, SSM_WIDTH, SSM_STATE), lambda b: (b, 0, 0))
    y, st = pl.pallas_call(
        functools.partial(_ssd_step_kernel, group=group),
        grid=(nb,),
        in_specs=[st_in, colb, colb, colb, colb, rowb, rowb,
                  _const_spec((SSM_WIDTH, 1)), _const_spec((SSM_WIDTH, 1))],
        out_specs=[colb, st_out],
        out_shape=[jax.ShapeDtypeStruct((bs, SSM_WIDTH, 1), F32),
                   jax.ShapeDtypeStruct((bs, SSM_WIDTH, SSM_STATE), F32)],
        compiler_params=_cparams(("parallel",)),
        name="ssd_step",
    )(state_all, xact[:, :SSM_WIDTH].reshape(bs, SSM_WIDTH, 1), z.reshape(bs, SSM_WIDTH, 1), rep(dt), rep(dec),
      xact[:, SSM_WIDTH:SSM_WIDTH + 2 * SSM_STATE].reshape(bs, 1, 2 * SSM_STATE),
      xact[:, SSM_WIDTH + 2 * SSM_STATE:].reshape(bs, 1, 2 * SSM_STATE),
      d_skip_x.reshape(SSM_WIDTH, 1), norm_w.reshape(SSM_WIDTH, 1))
    return y.reshape(bs, SSM_WIDTH), st.reshape(bs, SSM_HEADS, SSM_HEADDIM, SSM_STATE)


def _mm_kernel(x_ref, w_ref, *out_refs, scale):
    y = _dot(x_ref[...], w_ref[...])
    if scale != 1.0:
        y = y * scale
    for r in out_refs:
        r[...] = y.astype(r.dtype)


def _matmul(x, w, *, tm, scale=1.0, dtypes=(F32,)):
    m, k = x.shape
    n = w.shape[1]
    res = pl.pallas_call(
        functools.partial(_mm_kernel, scale=scale),
        grid=(m // tm,),
        in_specs=[pl.BlockSpec((tm, k), lambda i: (i, 0)), _const_spec(w.shape)],
        out_specs=[pl.BlockSpec((tm, n), lambda i: (i, 0)) for _ in dtypes],
        out_shape=[jax.ShapeDtypeStruct((m, n), d) for d in dtypes],
        compiler_params=_cparams(("parallel",)),
        name="matmul",
    )(x, w)
    return res


def _mm_res_ln_kernel(*refs, n_in):
    a_refs = refs[:n_in]
    w_refs = refs[n_in:2 * n_in]
    x_ref, g_ref, b_ref, o_ref = refs[2 * n_in:]
    h = _dot(a_refs[0][...], w_refs[0][...])
    for a, w in zip(a_refs[1:], w_refs[1:]):
        h = h + _dot(a[...], w[...])
    o_ref[...] = _layer_norm(ALPHA * x_ref[...] + h, g_ref[...], b_ref[...])


def _mm_res_ln(acts, weights, x, g, b, *, tm, x_row0=0):
    m = acts[0].shape[0]
    row = lambda a: pl.BlockSpec((tm, a.shape[1]), lambda i: (i, 0))
    return pl.pallas_call(
        functools.partial(_mm_res_ln_kernel, n_in=len(acts)),
        grid=(m // tm,),
        in_specs=[row(a) for a in acts] + [_const_spec(w.shape) for w in weights]
        + [pl.BlockSpec((tm, D_MODEL), lambda i: (x_row0 + i, 0)), _const_spec(g.shape), _const_spec(b.shape)],
        out_specs=pl.BlockSpec((tm, D_MODEL), lambda i: (i, 0)),
        out_shape=jax.ShapeDtypeStruct((m, D_MODEL), F32),
        compiler_params=_cparams(("parallel",)),
        name="mm_res_ln",
    )(*acts, *weights, x, g, b)


def _mix_mem_kernel(oa_ref, y_ref, yp_ref, x_ref, wa_ref, wy_ref, wp_ref, g1_ref, b1_ref,
                    mk_ref, mv_ref, wq_ref, wo_ref, g2_ref, b2_ref, *rest, route):
    h = _dot(oa_ref[...], wa_ref[...]) + _dot(y_ref[...], wy_ref[...]) + _dot(yp_ref[...], wp_ref[...])
    x = _layer_norm(ALPHA * x_ref[...] + h, g1_ref[...], b1_ref[...])
    q = (_dot(x, wq_ref[...]) * (MEM_HD ** -0.5)).astype(BF16)
    outs = []
    for hd in range(MEM_HEADS):
        sl = slice(hd * MEM_HD, (hd + 1) * MEM_HD)
        s = lax.dot_general(q[:, sl], mk_ref[:, sl], (((1,), (1,)), ((), ())), preferred_element_type=F32)
        p = jnp.exp(s - jnp.max(s, axis=1, keepdims=True))
        p = p * (1.0 / jnp.sum(p, axis=1, keepdims=True))
        outs.append(jnp.dot(p.astype(BF16), mv_ref[:, sl], preferred_element_type=F32))
    o = jnp.concatenate(outs, axis=1)
    x2 = _layer_norm(ALPHA * x + _dot(o, wo_ref[...]), g2_ref[...], b2_ref[...])
    if route:
        wh_ref, wl_ref, rb_ref, o_ref, r_ref = rest
        r_ref[...] = _route(x2, wh_ref, wl_ref, rb_ref)
    else:
        (o_ref,) = rest
    o_ref[...] = x2


def _mix_mem_prompt(acts, w_parts, x, g1, b1, mkb, mvb, wq, wo, g2, b2, router=None, *, batch, seq, tm):
    n_mem = mkb.shape[0] // batch
    nt = seq // tm
    rblk = lambda w: pl.BlockSpec((tm, w), lambda bb, i: (bb * nt + i, 0))
    mblk = pl.BlockSpec((None, n_mem, D_MODEL), lambda bb, i: (bb, 0, 0))
    const = lambda a: pl.BlockSpec(a.shape, lambda bb, i: (0,) * a.ndim)
    consts = (*w_parts, g1, b1)
    tail = (wq, wo, g2, b2) + (tuple(router) if router else ())
    out_specs = [rblk(D_MODEL)] + ([rblk(LANES)] if router else [])
    out_shape = [jax.ShapeDtypeStruct((batch * seq, D_MODEL), F32)]
    if router:
        out_shape.append(jax.ShapeDtypeStruct((batch * seq, LANES), F32))
    return pl.pallas_call(
        functools.partial(_mix_mem_kernel, route=bool(router)),
        grid=(batch, nt),
        in_specs=[rblk(a.shape[1]) for a in acts] + [rblk(D_MODEL)] + [const(a) for a in consts]
        + [mblk, mblk] + [const(a) for a in tail],
        out_specs=out_specs,
        out_shape=out_shape,
        compiler_params=_cparams(("parallel", "parallel")),
        name="mix_mem_prompt",
    )(*acts, x, *consts, mkb.reshape(batch, n_mem, D_MODEL), mvb.reshape(batch, n_mem, D_MODEL), *tail)


def _mem_sample_kernel(q_ref, mk_ref, mv_ref, o_ref, *, group):
    rows = mk_ref.shape[1] // 8
    for i in range(group):
        q8 = q_ref[i]
        r = jnp.sum(mk_ref[i].reshape(rows, 8, LANES) * q8, axis=2, keepdims=True)
        s = r + pltpu.roll(r, MEM_HEADS, 1)
        p = jnp.exp(s - jnp.max(s, axis=0, keepdims=True))
        p = p * (1.0 / jnp.sum(p, axis=0, keepdims=True))
        o_ref[i] = jnp.sum(mv_ref[i].reshape(rows, 8, LANES) * p, axis=0)


def _mem_rows(cache, n):
    n_mem = cache.shape[2]
    c = cache.reshape(n, n_mem, MEM_HEADS, MEM_HD // LANES, LANES)
    return jnp.transpose(c, (0, 1, 3, 2, 4)).reshape(n, n_mem * 2 * MEM_HEADS, LANES)


def _mem_attn_sample(q, mem_k_rows, mem_v_rows, *, layer, group):
    bs = q.shape[0]
    nb = bs // group
    halves = MEM_HD // LANES
    q8 = jnp.transpose(q.reshape(bs, MEM_HEADS, halves, LANES), (0, 2, 1, 3)).reshape(bs, halves * MEM_HEADS, LANES)
    qblk = pl.BlockSpec((group, halves * MEM_HEADS, LANES), lambda i: (i, 0, 0))
    mblk = pl.BlockSpec((group,) + mem_k_rows.shape[1:], lambda i: (layer * nb + i, 0, 0))
    out = pl.pallas_call(
        functools.partial(_mem_sample_kernel, group=group),
        grid=(nb,),
        in_specs=[qblk, mblk, mblk],
        out_specs=qblk,
        out_shape=jax.ShapeDtypeStruct(q8.shape, F32),
        compiler_params=_cparams(("parallel",)),
        name="mem_attn_sample",
    )(q8, mem_k_rows, mem_v_rows)
    return jnp.transpose(out.reshape(bs, halves, MEM_HEADS, LANES), (0, 2, 1, 3)).reshape(bs, D_MODEL)


def _ffn_kernel(x_ref, w1_ref, w3_ref, w2_ref, g_ref, b_ref, o_ref, *, fc):
    x = x_ref[...]
    xb = x.astype(BF16)
    acc = jnp.zeros(x.shape, F32)
    for c in range(w1_ref.shape[1] // fc):
        sl = slice(c * fc, (c + 1) * fc)
        h = _silu(jnp.dot(xb, w1_ref[:, sl], preferred_element_type=F32)) * jnp.dot(xb, w3_ref[:, sl],
                                                                                    preferred_element_type=F32)
        acc = acc + jnp.dot(h.astype(BF16), w2_ref[sl, :], preferred_element_type=F32)
    o_ref[...] = _layer_norm(ALPHA * x + acc, g_ref[...], b_ref[...])


def _ffn(x, w1, w3, w2, g, b, *, tm, fc):
    m = x.shape[0]
    row = pl.BlockSpec((tm, D_MODEL), lambda i: (i, 0))
    wspec = lambda w: pl.BlockSpec(w.shape, lambda i: (0, 0), pipeline_mode=pl.Buffered(1))
    return pl.pallas_call(
        functools.partial(_ffn_kernel, fc=fc),
        grid=(m // tm,),
        in_specs=[row, wspec(w1), wspec(w3), wspec(w2), _const_spec(g.shape), _const_spec(b.shape)],
        out_specs=row,
        out_shape=jax.ShapeDtypeStruct(x.shape, F32),
        compiler_params=_cparams(("parallel",), vmem_mb=56),
        name="ffn_dense",
    )(x, w1, w3, w2, g, b)


def _route(x, wh_ref, wl_ref, b_ref):
    xh = x.astype(BF16)
    xl = (x - xh.astype(F32)).astype(BF16)
    wh = wh_ref[...]
    logits = (jnp.dot(xh, wh, preferred_element_type=F32) + jnp.dot(xl, wh, preferred_element_type=F32)
              + jnp.dot(xh, wl_ref[...], preferred_element_type=F32))
    lane = lax.broadcasted_iota(jnp.int32, logits.shape, 1)
    logits = jnp.where(lane < N_EXPERTS, logits + b_ref[...], NEG_INF)
    m1 = jnp.max(logits, axis=1, keepdims=True)
    i1 = jnp.min(jnp.where(logits == m1, lane, LANES), axis=1, keepdims=True)
    rest = jnp.where(lane == i1, NEG_INF, logits)
    m2 = jnp.max(rest, axis=1, keepdims=True)
    i2 = jnp.min(jnp.where(rest == m2, lane, LANES), axis=1, keepdims=True)
    e = jnp.exp(m2 - m1)
    g1 = 1.0 / (1.0 + e)
    g2 = e * g1
    return jnp.where(lane == 0, i1.astype(F32), jnp.where(lane == 1, i2.astype(F32),
                     jnp.where(lane == 2, g1, jnp.where(lane == 3, g2, 0.0))))


def _router_kernel(x_ref, wh_ref, wl_ref, b_ref, o_ref):
    o_ref[...] = _route(x_ref[...], wh_ref, wl_ref, b_ref)


def _router(x, w_hi, w_lo, b_pad, *, tm):
    m = x.shape[0]
    return pl.pallas_call(
        _router_kernel,
        grid=(m // tm,),
        in_specs=[pl.BlockSpec((tm, D_MODEL), lambda i: (i, 0)), _const_spec(w_hi.shape), _const_spec(w_lo.shape),
                  _const_spec(b_pad.shape)],
        out_specs=pl.BlockSpec((tm, LANES), lambda i: (i, 0)),
        out_shape=jax.ShapeDtypeStruct((m, LANES), F32),
        compiler_params=_cparams(("parallel",)),
        name="moe_router",
    )(x, w_hi, w_lo, b_pad)


def _expert_kernel(te_ref, tr_ref, x_ref, w1_ref, w3_ref, w2_ref, o_ref, acc_sc, *, sub):
    i = pl.program_id(0)
    j = pl.program_id(1)
    rows = tr_ref[i]
    tm = x_ref.shape[0]

    @pl.when(j == 0)
    def _():
        acc_sc[...] = jnp.zeros_like(acc_sc)

    def swiglu_into_acc(sl, w1, w3, w2):
        xb = x_ref[sl, :].astype(BF16)
        h = _silu(jnp.dot(xb, w1, preferred_element_type=F32)) * jnp.dot(xb, w3, preferred_element_type=F32)
        acc_sc[sl, :] += jnp.dot(h.astype(BF16), w2, preferred_element_type=F32)

    def weights():
        return w1_ref[...].astype(BF16), w3_ref[...].astype(BF16), w2_ref[...].astype(BF16)

    @pl.when(rows == tm)
    def _():
        swiglu_into_acc(slice(None), *weights())

    @pl.when((rows > 0) & (rows < tm))
    def _():
        ws = weights()
        for s in range(tm // sub):
            pl.when(rows > s * sub)(functools.partial(swiglu_into_acc, slice(s * sub, (s + 1) * sub), *ws))

    @pl.when(j == pl.num_programs(1) - 1)
    def _():
        o_ref[...] = acc_sc[...]


def _experts(x_sorted, tile_expert, tile_rows, w1, w3, w2, *, tm, sub, fc):
    n_slots = x_sorted.shape[0]
    nj = w1.shape[2] // fc
    last = nj - 1

    def jj(i, j, tr):
        return jnp.where(tr[i] > 0, j, last)

    xs = pl.BlockSpec((tm, D_MODEL), lambda i, j, te, tr: (i, 0))
    w13 = pl.BlockSpec((None, D_MODEL, fc), lambda i, j, te, tr: (te[i], 0, jj(i, j, tr)))
    w2s = pl.BlockSpec((None, fc, D_MODEL), lambda i, j, te, tr: (te[i], jj(i, j, tr), 0))
    return pl.pallas_call(
        functools.partial(_expert_kernel, sub=sub),
        grid_spec=pltpu.PrefetchScalarGridSpec(
            num_scalar_prefetch=2, grid=(n_slots // tm, nj),
            in_specs=[xs, w13, w13, w2s], out_specs=xs,
            scratch_shapes=[pltpu.VMEM((tm, D_MODEL), F32)]),
        out_shape=jax.ShapeDtypeStruct((n_slots, D_MODEL), F32),
        compiler_params=_cparams(("arbitrary", "arbitrary"), vmem_mb=56),
        name="moe_experts",
    )(tile_expert, tile_rows, x_sorted, w1, w3, w2)


def _dispatch_kernel(slot_ref, x_ref, init_ref, xs_ref, sem, *, tm, tok0):
    del init_ref
    base = (tok0 + pl.program_id(0) * tm) * 2

    def issue(r, carry):
        for k in range(2):
            dst = slot_ref[base + 2 * r + k]
            pltpu.make_async_copy(x_ref.at[pl.ds(r, 1), :], xs_ref.at[pl.ds(dst, 1), :], sem).start()
        return carry

    lax.fori_loop(0, tm, issue, 0, unroll=8)
    for _ in range(2):
        pltpu.make_async_copy(x_ref, x_ref, sem).wait()


def _dispatch(x, slot, buf, *, tm, tok0):
    t = x.shape[0]
    return pl.pallas_call(
        functools.partial(_dispatch_kernel, tm=tm, tok0=tok0),
        grid_spec=pltpu.PrefetchScalarGridSpec(
            num_scalar_prefetch=1, grid=(t // tm,),
            in_specs=[pl.BlockSpec((tm, D_MODEL), lambda i, s: (i, 0)), pl.BlockSpec(memory_space=pl.ANY)],
            out_specs=pl.BlockSpec(memory_space=pl.ANY),
            scratch_shapes=[pltpu.SemaphoreType.DMA(())]),
        out_shape=jax.ShapeDtypeStruct(buf.shape, F32),
        input_output_aliases={2: 0},
        compiler_params=_cparams(("arbitrary",)),
        name="moe_dispatch",
    )(slot, x, buf)


def _combine_ln_kernel(slot_ref, x_ref, r_ref, y_ref, g_ref, b_ref, o_ref, ybuf, sems, *, tm, tok0):
    i = pl.program_id(0)
    cur = i % 2

    def issue(step, half):
        base = (tok0 + step * tm) * 2

        def body(r, carry):
            for k in range(2):
                src = slot_ref[base + 2 * r + k]
                pltpu.make_async_copy(y_ref.at[pl.ds(src, 1), :], ybuf.at[half, k, pl.ds(r, 1), :],
                                      sems.at[half]).start()
            return carry

        lax.fori_loop(0, tm, body, 0, unroll=8)

    @pl.when(i == 0)
    def _():
        issue(0, 0)

    @pl.when(i + 1 < pl.num_programs(0))
    def _():
        issue(i + 1, 1 - cur)

    pltpu.make_async_copy(ybuf.at[cur], ybuf.at[cur], sems.at[cur]).wait()
    r = r_ref[...]
    f = r[:, 2:3] * ybuf[cur, 0] + r[:, 3:4] * ybuf[cur, 1]
    o_ref[...] = _layer_norm(ALPHA * x_ref[...] + f, g_ref[...], b_ref[...])


def _combine_ln(x, y_sorted, slot, route, g, b, *, tm, tok0):
    rows = x.shape[0]
    row = lambda w: pl.BlockSpec((tm, w), lambda i, s: (i, 0))
    const = lambda a: pl.BlockSpec(a.shape, lambda i, s: (0,) * a.ndim)
    return pl.pallas_call(
        functools.partial(_combine_ln_kernel, tm=tm, tok0=tok0),
        grid_spec=pltpu.PrefetchScalarGridSpec(
            num_scalar_prefetch=1, grid=(rows // tm,),
            in_specs=[row(D_MODEL), row(LANES), pl.BlockSpec(memory_space=pl.ANY), const(g), const(b)],
            out_specs=row(D_MODEL),
            scratch_shapes=[pltpu.VMEM((2, 2, tm, D_MODEL), F32), pltpu.SemaphoreType.DMA((2,))]),
        out_shape=jax.ShapeDtypeStruct((rows, D_MODEL), F32),
        compiler_params=_cparams(("arbitrary",)),
        name="moe_combine_ln",
    )(slot, x, route, y_sorted, g, b)


def _router_params(router_w, router_b):
    w_pad = jnp.zeros((D_MODEL, LANES), F32).at[:, :N_EXPERTS].set(router_w)
    w_hi = w_pad.astype(BF16)
    w_lo = (w_pad - w_hi.astype(F32)).astype(BF16)
    return w_hi, w_lo, jnp.zeros((1, LANES), F32).at[0, :N_EXPERTS].set(router_b)


def _moe(xp, xs, route_p, router, w1, w3, w2, g, b, *, tm_e, sub, fc, tm_p):
    n_p, n_s = xp.shape[0], xs.shape[0]
    t = n_p + n_s
    route_s = _router(xs, *router, tm=n_s)
    top_i = jnp.concatenate([route_p[:, :2], route_s[:, :2]], axis=0).astype(jnp.int32).reshape(-1)
    onehot = (top_i[:, None] == jnp.arange(N_EXPERTS)[None, :]).astype(jnp.int32)
    csum = jnp.cumsum(onehot, axis=0)
    counts = csum[-1]
    rank = jnp.sum((csum - onehot) * onehot, axis=1)
    padded = ((counts + tm_e - 1) // tm_e) * tm_e
    ends = jnp.cumsum(padded)
    starts = ends - padded
    slot = (starts[top_i] + rank).astype(jnp.int32)
    n_tiles = (2 * t + N_EXPERTS * (tm_e - 1)) // tm_e + 1
    tile_start = jnp.arange(n_tiles, dtype=jnp.int32) * tm_e
    tile_expert = jnp.sum((tile_start[:, None] >= ends[None, :]).astype(jnp.int32), axis=1)
    tile_expert = jnp.minimum(tile_expert, N_EXPERTS - 1)
    tile_rows = jnp.clip((starts + counts)[tile_expert] - tile_start, 0, tm_e)
    tile_rows = jnp.where(tile_start < ends[-1], tile_rows, 0).astype(jnp.int32)
    last_e = tile_expert[jnp.maximum(ends[-1] // tm_e - 1, 0)]
    tile_expert = jnp.where(tile_rows > 0, tile_expert, last_e).astype(jnp.int32)
    x_sorted = jnp.zeros((n_tiles * tm_e, D_MODEL), F32)
    x_sorted = _dispatch(xp, slot, x_sorted, tm=tm_p, tok0=0)
    x_sorted = _dispatch(xs, slot, x_sorted, tm=n_s, tok0=n_p)
    y_sorted = _experts(x_sorted, tile_expert, tile_rows, w1, w3, w2, tm=tm_e, sub=sub, fc=fc)
    return (_combine_ln(xp, y_sorted, slot, route_p, g, b, tm=tm_p // 2, tok0=0),
            _combine_ln(xs, y_sorted, slot, route_s, g, b, tm=n_s, tok0=n_p))


def _pack_w_in(w_in_t):
    dt, pu = w_in_t[_C_P:_C_P + SSM_HEADS], w_in_t[_C_P + SSM_HEADS:]
    pad = jnp.zeros((DT_PAD - SSM_HEADS, D_MODEL), w_in_t.dtype)
    return w_in_t[:_C_P].astype(BF16), jnp.concatenate([pu, dt, pad], axis=0).astype(BF16)


def _block_diag(pool_w):
    out = jnp.zeros((POOL_WIDTH, POOL_WIDTH), F32)
    for g in range(len(POOL_WINDOWS)):
        out = out.at[g * POOL_GDIM:(g + 1) * POOL_GDIM, g * POOL_GDIM:(g + 1) * POOL_GDIM].set(pool_w[g])
    return out.astype(BF16)


def _pad_heads(v):
    return jnp.zeros((1, DT_PAD), F32).at[0, :SSM_HEADS].set(v)


def kernel(x_prompt, x_sample, cache_k, cache_v, cache_mem_k, cache_mem_v, state_ssm, state_conv, state_pool, page_table, mem_prompt, w_in, lam_q1, lam_k1, lam_q2, lam_k2, da_norm_w, conv_w, conv_b, dt_bias, a_log, d_skip, ssm_norm_w, pool_w, pool_scale, w_out, ln1_g, ln1_b, wq_mem, wk_mem, wv_mem, wo_mem, ln2_g, ln2_b, ffn_w1, ffn_w3, ffn_w2, moe_router, moe_router_b, moe_w1, moe_w3, moe_w2, ln3_g, ln3_b):
    bp, seq, _ = x_prompt.shape
    bs = x_sample.shape[0]
    n_pages = page_table.shape[1]
    past = n_pages * PAGE_SIZE
    n_mem = mem_prompt.shape[1]
    tp_rows = bp * seq
    TM = ROW_TILE

    xp = x_prompt.reshape(tp_rows, D_MODEL)
    xs = x_sample.reshape(bs, D_MODEL)
    tab_p = _rope_tables(jnp.arange(seq))
    tab_s = _rope_tables(jnp.full((1,), past))
    cache_k4 = jnp.transpose(cache_k, (0, 1, 3, 4, 5, 2)).reshape(DEPTH, -1, DA_WIDTH, PAGE_SIZE)
    cache_v4 = cache_v.reshape(DEPTH, -1, PAGE_SIZE * DA_HEADS, DA_DV)
    mem_k_all = _mem_rows(cache_mem_k, DEPTH * bs)
    mem_v_all = _mem_rows(cache_mem_v, DEPTH * bs)
    state_all = state_ssm.reshape(DEPTH * bs, SSM_WIDTH, SSM_STATE)
    conv_hist = jnp.transpose(state_conv, (0, 2, 1, 3))
    pool_hist = jnp.transpose(state_pool, (0, 2, 1, 3))
    w_in_t = jnp.transpose(w_in, (2, 0, 1))
    mem_flat = mem_prompt.reshape(bp * n_mem, D_MODEL)
    row = lambda v: v.reshape(1, -1)

    outs = {k: [] for k in ("kp", "vp", "ks", "vs", "mk", "mv", "hp", "hs", "cp", "cs", "pp", "ps")}
    for l in range(DEPTH):
        lam_init = 0.8 - 0.6 * math.exp(-0.3 * l)
        w_cat = _pack_w_in(w_in_t[:, l, :])
        lam_vecs = (row(lam_q1[l]), row(lam_k1[l]), row(lam_q2[l]), row(lam_k2[l]))
        nw = row(da_norm_w[l])
        dtb, alog = _pad_heads(dt_bias[l]), _pad_heads(a_log[l])
        dsk = row(jnp.repeat(d_skip[l], SSM_HEADDIM))
        snw = row(ssm_norm_w[l])
        pwbd = _block_diag(pool_w[l])
        psc = row(pool_scale[l])
        wo_b = w_out[l].astype(BF16)
        wo_parts = (wo_b[:DA_WIDTH], wo_b[DA_WIDTH:DA_WIDTH + SSM_WIDTH], wo_b[DA_WIDTH + SSM_WIDTH:])
        g1, b1, g2, b2, g3, b3 = (row(ln1_g[l]), row(ln1_b[l]), row(ln2_g[l]), row(ln2_b[l]),
                                  row(ln3_g[l]), row(ln3_b[l]))

        qp, kb, vb, kpg, vrow, zp, xbcp, pup, dtp = _proj_in(xp, w_cat, tab_p, prompt=True, tm=TM, seq=seq,
                                                             rows=tp_rows)
        oa_p = _flash_attn(qp, kb, vb, lam_vecs, nw.reshape(DA_DV, 1), batch=bp, seq=seq, tq=ATTN_TILE, tk=ATTN_TILE,
                           lam_init=lam_init)
        y_p, h_p = _ssd_prompt(xbcp, zp, dtp, conv_w[l], row(conv_b[l]), dtb, alog, dsk, snw, batch=bp, seq=seq)
        yp_p = _pool_prompt(pup, pwbd, psc, batch=bp, seq=seq, tp=POOL_TILE)

        q_s, k_s, v_s, z_s, xbc_s, pu_s, dt_s = _proj_in(xs, w_cat, tab_s, prompt=False, tm=bs, seq=1, rows=bs)
        oa_s = _paged_attn(q_s, k_s, v_s, cache_k4, cache_v4, page_table, lam_vecs, nw, layer=l, lam_init=lam_init)
        xact, dt_a, dec, yp_s, conv_new, pool_new = _sample_prep(
            xbc_s, conv_hist[l], conv_w[l], row(conv_b[l]), dt_s, dtb, alog, pu_s, pool_hist[l], pwbd, psc, pos0=past)
        y_s, h_s = _ssd_step(state_all, l, xact, z_s, dt_a, dec, dsk, snw)
        xs = _mm_res_ln((oa_s, y_s, yp_s), wo_parts, xs, g1, b1, tm=bs)

        wq_b, wk_b, wv_b, wom_b = (wq_mem[l].astype(BF16), wk_mem[l].astype(BF16), wv_mem[l].astype(BF16),
                                   wo_mem[l].astype(BF16))
        mk, mkb = _matmul(mem_flat, wk_b, tm=bp * n_mem // 2, dtypes=(F32, BF16))
        mv, mvb = _matmul(mem_flat, wv_b, tm=bp * n_mem // 2, dtypes=(F32, BF16))
        router = _router_params(moe_router[l // 2], moe_router_b[l // 2]) if l % 2 == 1 else None
        res = _mix_mem_prompt((oa_p, y_p, yp_p), wo_parts, xp, g1, b1, mkb, mvb, wq_b, wom_b, g2, b2, router,
                              batch=bp, seq=seq, tm=TM)
        xp, route_p = res if router else (res[0], None)
        (qm_s,) = _matmul(xs, wq_b, tm=bs, scale=MEM_HD ** -0.5)
        om_s = _mem_attn_sample(qm_s, mem_k_all, mem_v_all, layer=l, group=MEM_SAMPLE_GROUP)
        xs = _mm_res_ln((om_s,), (wom_b,), xs, g2, b2, tm=bs)

        j = l // 2
        if l % 2 == 0:
            fw1, fw3, fw2 = ffn_w1[j].astype(BF16), ffn_w3[j].astype(BF16), ffn_w2[j].astype(BF16)
            xp = _ffn(xp, fw1, fw3, fw2, g3, b3, tm=TM, fc=FFN_CHUNK)
            xs = _ffn(xs, fw1, fw3, fw2, g3, b3, tm=bs, fc=FFN_CHUNK)
        else:
            xp, xs = _moe(xp, xs, route_p, router, moe_w1[j], moe_w3[j], moe_w2[j], g3, b3,
                          tm_e=EXPERT_TILE, sub=EXPERT_SUB, fc=EXPERT_CHUNK, tm_p=TM)

        outs["kp"].append(jnp.transpose(kpg.reshape(bp, seq // PAGE_SIZE, DA_HEADS, 2, DA_DK, PAGE_SIZE),
                                        (0, 1, 5, 2, 3, 4)))
        outs["vp"].append(vrow.reshape(bp, seq // PAGE_SIZE, PAGE_SIZE, DA_HEADS, DA_DV))
        outs["ks"].append(k_s.reshape(bs, 1, DA_HEADS, 2, DA_DK))
        outs["vs"].append(v_s.reshape(bs, 1, DA_HEADS, DA_DV))
        outs["mk"].append(mk.reshape(bp, n_mem, MEM_HEADS, MEM_HD))
        outs["mv"].append(mv.reshape(bp, n_mem, MEM_HEADS, MEM_HD))
        outs["hp"].append(h_p)
        outs["hs"].append(h_s)
        outs["cp"].append(xbcp.reshape(bp, seq, CONV_CH)[:, seq - (SSM_CONV - 1):])
        outs["cs"].append(jnp.transpose(conv_new, (1, 0, 2)))
        outs["pp"].append(pup.reshape(bp, seq, POOL_WIDTH)[:, seq - POOL_BUF:])
        outs["ps"].append(jnp.transpose(pool_new, (1, 0, 2)))

    st = lambda k: jnp.stack(outs[k])
    return (xp.reshape(bp, seq, D_MODEL), xs.reshape(bs, 1, D_MODEL), st("kp"), st("vp"), st("ks"), st("vs"),
            st("mk"), st("mv"), st("hp"), st("hs"), st("cp"), st("cs"), st("pp"), st("ps"))
```
